```python
import jax, jax.numpy as jnp
from jax import lax
import numpy as np

D_MODEL = 1024
BATCH = 8
SEQ = 2048
DEPTH = 2
DEC_BATCH = 128
DEC_SEQ = 1
PAST_LEN = 16384
PAGE_SIZE = 128

N_EVEN = (DEPTH + 1) // 2
N_ODD = DEPTH // 2
GLA_HEADS = 4
GLA_DK = D_MODEL // 2 // GLA_HEADS
GLA_DV = D_MODEL // GLA_HEADS
GLA_QK_WIDTH = GLA_HEADS * GLA_DK
GLA_V_WIDTH = GLA_HEADS * GLA_DV
GATE_RANK = 16
GATE_TEMP = 16.0
GLA_CHUNK = 64
SCONV_WIDTH = D_MODEL
SCONV_K = 3
CCONV_WIDTH = D_MODEL
CCONV_K = 31
RMS_EPS = 1e-6
LN_EPS = 1e-5
EVEN_SPLITS = (GLA_QK_WIDTH, GLA_QK_WIDTH, GLA_V_WIDTH, GLA_V_WIDTH, GATE_RANK,
               SCONV_WIDTH, SCONV_WIDTH, SCONV_WIDTH, SCONV_WIDTH)
EVEN_PROJ = sum(EVEN_SPLITS)
EVEN_MIX_WIDTH = GLA_V_WIDTH + SCONV_WIDTH

kernel_name = 'hybrid_gla_shortconv_conformer_decode_step'


def _split(p, sizes):
    idx = np.cumsum(sizes)[:-1].tolist()
    return jnp.split(p, idx, axis=-1)


def rmsnorm(x, g):
    xf = x.astype(jnp.float32)
    y = xf * lax.rsqrt(jnp.mean(xf * xf, axis=-1, keepdims=True) + RMS_EPS)
    return (y * g.astype(jnp.float32)).astype(x.dtype)


def layernorm(x, g, b):
    xf = x.astype(jnp.float32)
    mu = jnp.mean(xf, axis=-1, keepdims=True)
    xc = xf - mu
    y = xc * lax.rsqrt(jnp.mean(xc * xc, axis=-1, keepdims=True) + LN_EPS)
    return (y * g.astype(jnp.float32) + b.astype(jnp.float32)).astype(x.dtype)


def causal_dwconv(u, buf, w):
    k_width = w.shape[0]
    u_full = jnp.concatenate([buf.astype(u.dtype), u], axis=1)
    y = lax.conv_general_dilated(u_full, w.astype(u.dtype)[:, None, :], (1,), 'VALID',
                                 dimension_numbers=('NWC', 'WIO', 'NWC'),
                                 feature_group_count=u.shape[-1])
    return y, u_full[:, -(k_width - 1):]


def gla_mix(q, k, v, log_a, s0):
    bsz, t_len, n_h, _ = q.shape
    dv = v.shape[-1]
    L = min(GLA_CHUNK, t_len)
    pad = (-t_len) % L
    n_blk = (t_len + pad) // L

    def blocks(t):
        t = jnp.pad(t.astype(jnp.float32), ((0, 0), (0, pad), (0, 0), (0, 0)))
        return t.reshape(bsz, n_blk, L, n_h, t.shape[-1]).transpose(0, 3, 1, 2, 4)

    q, k, v, log_a = blocks(q), blocks(k), blocks(v), blocks(log_a)
    b = jnp.cumsum(log_a, axis=3)
    b_last = b[:, :, :, -1:, :]
    qe = q * jnp.exp(b)
    ke = k * jnp.exp(-b)
    causal = jnp.tril(jnp.ones((L, L), dtype=bool))
    scores = jnp.where(causal, jnp.einsum('bhnld,bhnmd->bhnlm', qe, ke), 0.0)
    o_intra = jnp.einsum('bhnlm,bhnmv->bhnlv', scores, v)
    ds = jnp.einsum('bhnld,bhnlv->bhndv', k * jnp.exp(b_last - b), v)
    decay = jnp.exp(b_last[:, :, :, 0, :])

    def step(s, inp):
        dec, d = inp
        return dec[..., None] * s + d, s

    s_final, s_prev = lax.scan(step, s0.astype(jnp.float32),
                               (jnp.moveaxis(decay, 2, 0), jnp.moveaxis(ds, 2, 0)))
    s_prev = jnp.moveaxis(s_prev, 0, 2)
    o = o_intra + jnp.einsum('bhnld,bhndv->bhnlv', qe, s_prev)
    o = o.transpose(0, 2, 3, 1, 4).reshape(bsz, n_blk * L, n_h, dv)[:, :t_len]
    return o, s_final


def even_layer(h, s0, buf, w_in, w_gate_up, b_gate_up, gla_norm_g, w_sconv, w_out):
    bsz, t_len, _ = h.shape
    q, k, v, g, a_low, hb, gate_b, gate_c, z_b = _split(h @ w_in, EVEN_SPLITS)
    q = q.reshape(bsz, t_len, GLA_HEADS, GLA_DK) * (GLA_DK ** -0.5)
    k = k.reshape(bsz, t_len, GLA_HEADS, GLA_DK)
    v = v.reshape(bsz, t_len, GLA_HEADS, GLA_DV)
    log_a = jax.nn.log_sigmoid((a_low @ w_gate_up + b_gate_up).astype(jnp.float32)) / GATE_TEMP
    log_a = log_a.reshape(bsz, t_len, GLA_HEADS, GLA_DK)
    o, s_new = gla_mix(q, k, v, log_a, s0)
    o = rmsnorm(o, gla_norm_g).reshape(bsz, t_len, GLA_V_WIDTH).astype(h.dtype) * jax.nn.silu(g)
    y, new_buf = causal_dwconv(gate_c * hb, buf, w_sconv)
    y = gate_b * y * jax.nn.silu(z_b)
    out = jnp.concatenate([o, y], axis=-1) @ w_out
    return out, s_new.astype(s0.dtype), new_buf


def odd_layer(h, buf, w_in, b_in, w_dw, b_dw, ln_g, ln_b, w_out, b_out):
    a, a_gate, z = jnp.split(h @ w_in + b_in, 3, axis=-1)
    u = a * jax.nn.sigmoid(a_gate)
    y, new_buf = causal_dwconv(u, buf, w_dw)
    y = jax.nn.silu(layernorm(y + b_dw, ln_g, ln_b)) * jax.nn.silu(z)
    return y @ w_out + b_out, new_buf


def trunk(x, s_gla, buf_s, buf_c, norm_g, w_in_a, w_gate_up, b_gate_up, gla_norm_g, w_sconv,
          w_out_a, w_in_c, b_in_c, w_dwconv, b_dwconv, ln_g, ln_b, w_out_c, b_out_c, final_norm_g):
    new_gla, new_s, new_c = [], [], []
    for layer in range(DEPTH):
        i = layer // 2
        h = rmsnorm(x, norm_g[layer])
        if layer % 2 == 0:
            out, s, bs = even_layer(h, s_gla[i], buf_s[i], w_in_a[i], w_gate_up[i], b_gate_up[i],
                                    gla_norm_g[i], w_sconv[i], w_out_a[i])
            new_gla.append(s)
            new_s.append(bs)
        else:
            out, bc = odd_layer(h, buf_c[i], w_in_c[i], b_in_c[i], w_dwconv[i], b_dwconv[i],
                                ln_g[i], ln_b[i], w_out_c[i], b_out_c[i])
            new_c.append(bc)
        x = x + out
    return rmsnorm(x, final_norm_g), jnp.stack(new_gla), jnp.stack(new_s), jnp.stack(new_c)


def setup_inputs(seed: int = 0) -> dict:
    key = jax.random.key(seed)
    ks = jax.random.split(key, 24)

    def nrm(k, shape, s):
        return jax.random.normal(k, shape, jnp.float32) * s

    return {
        'x_prompt': nrm(ks[0], (BATCH, SEQ, D_MODEL), 1.0),
        'x_sample': nrm(ks[1], (DEC_BATCH, DEC_SEQ, D_MODEL), 1.0),
        'state_gla': nrm(ks[2], (N_EVEN, DEC_BATCH, GLA_HEADS, GLA_DK, GLA_DV), 1.0),
        'state_sconv': nrm(ks[3], (N_EVEN, DEC_BATCH, SCONV_K - 1, SCONV_WIDTH), 1.0),
        'state_cconv': nrm(ks[4], (N_ODD, DEC_BATCH, CCONV_K - 1, CCONV_WIDTH), 0.5),
        'norm_g': 1.0 + nrm(ks[5], (DEPTH, D_MODEL), 0.02),
        'w_in_a': nrm(ks[6], (N_EVEN, D_MODEL, EVEN_PROJ), D_MODEL ** -0.5),
        'w_gate_up': nrm(ks[7], (N_EVEN, GATE_RANK, GLA_QK_WIDTH), GATE_RANK ** -0.5),
        'b_gate_up': nrm(ks[8], (N_EVEN, GLA_QK_WIDTH), 0.1),
        'gla_norm_g': 1.0 + nrm(ks[9], (N_EVEN, GLA_DV), 0.02),
        'w_sconv': nrm(ks[10], (N_EVEN, SCONV_K, SCONV_WIDTH), SCONV_K ** -0.5),
        'w_out_a': nrm(ks[11], (N_EVEN, EVEN_MIX_WIDTH, D_MODEL), EVEN_MIX_WIDTH ** -0.5),
        'w_in_c': nrm(ks[12], (N_ODD, D_MODEL, 3 * CCONV_WIDTH), D_MODEL ** -0.5),
        'b_in_c': nrm(ks[13], (N_ODD, 3 * CCONV_WIDTH), 0.02),
        'w_dwconv': nrm(ks[14], (N_ODD, CCONV_K, CCONV_WIDTH), CCONV_K ** -0.5),
        'b_dwconv': nrm(ks[15], (N_ODD, CCONV_WIDTH), 0.02),
        'ln_g': 1.0 + nrm(ks[16], (N_ODD, CCONV_WIDTH), 0.02),
        'ln_b': nrm(ks[17], (N_ODD, CCONV_WIDTH), 0.02),
        'w_out_c': nrm(ks[18], (N_ODD, CCONV_WIDTH, D_MODEL), CCONV_WIDTH ** -0.5),
        'b_out_c': nrm(ks[19], (N_ODD, D_MODEL), 0.02),
        'final_norm_g': 1.0 + nrm(ks[20], (D_MODEL,), 0.02),
    }


def reference(x_prompt, x_sample, state_gla, state_sconv, state_cconv, norm_g, w_in_a, w_gate_up,
              b_gate_up, gla_norm_g, w_sconv, w_out_a, w_in_c, b_in_c, w_dwconv, b_dwconv, ln_g, ln_b,
              w_out_c, b_out_c, final_norm_g):
    weights = (norm_g, w_in_a, w_gate_up, b_gate_up, gla_norm_g, w_sconv, w_out_a, w_in_c, b_in_c,
               w_dwconv, b_dwconv, ln_g, ln_b, w_out_c, b_out_c, final_norm_g)
    bp = x_prompt.shape[0]
    dt = x_prompt.dtype
    zero_gla = jnp.zeros((N_EVEN, bp, GLA_HEADS, GLA_DK, GLA_DV), dt)
    zero_s = jnp.zeros((N_EVEN, bp, SCONV_K - 1, SCONV_WIDTH), dt)
    zero_c = jnp.zeros((N_ODD, bp, CCONV_K - 1, CCONV_WIDTH), dt)
    y_prompt, gla_p, sconv_p, cconv_p = trunk(x_prompt, zero_gla, zero_s, zero_c, *weights)
    y_sample, gla_s, sconv_s, cconv_s = trunk(x_sample, state_gla, state_sconv, state_cconv, *weights)
    return (y_prompt, y_sample, gla_p, sconv_p, cconv_p, gla_s, sconv_s, cconv_s)
```

```python
import functools

import jax
import jax.numpy as jnp
from jax import lax
from jax.experimental import pallas as pl
from jax.experimental.pallas import tpu as pltpu

F32 = jnp.float32
BF16 = jnp.bfloat16

D_MODEL = 1024
HEADS = 4
DK = 128
DV = 256
QK_WIDTH = HEADS * DK
V_WIDTH = HEADS * DV
GATE_RANK = 16
GATE_RANK_PAD = 128
GATE_TEMP_INV = 1.0 / 16.0
CHUNK = 64
SCONV_K = 3
CCONV_K = 31
CONV_W = 1024
RMS_EPS = 1e-6
LN_EPS = 1e-5
Q_SCALE = DK ** -0.5

PROMPT_TILE = 256
CCONV_HALO = 32
SCONV_HALO = 8
CCONV_ROWS = 16
DECODE_STATE_BLOCK = 8
DECODE_ODD_BLOCK = 32
VMEM_LIMIT = 56 * 1024 * 1024


def _dot(a, b):
    return jnp.dot(a, b, preferred_element_type=F32)


def _dot_nt(a, b):
    return lax.dot_general(a, b, (((1,), (1,)), ((), ())), preferred_element_type=F32)


def _dot_tn(a, b):
    return lax.dot_general(a, b, (((0,), (0,)), ((), ())), preferred_element_type=F32)


def _rmsnorm(x, g):
    ms = jnp.mean(x * x, axis=-1, keepdims=True)
    return x * lax.rsqrt(ms + RMS_EPS) * g


def _silu(x):
    return x * jax.nn.sigmoid(x)


def _log_sigmoid(x):
    return -(jnp.maximum(-x, 0.0) + jnp.log1p(jnp.exp(-jnp.abs(x))))


def _even_front(x, ng, wqk_ref, wv_ref, wg_ref, wal_ref, wup_ref, bup_ref):
    h = _rmsnorm(x, ng).astype(BF16)
    qk = _dot(h, wqk_ref[...])
    q = qk[:, :QK_WIDTH] * Q_SCALE
    k = qk[:, QK_WIDTH:]
    v = _dot(h, wv_ref[...])
    g = _dot(h, wg_ref[...])
    a_low = _dot(h, wal_ref[...]).astype(BF16)
    logit = _dot(a_low, wup_ref[...]) + bup_ref[...]
    log_a = _log_sigmoid(logit) * GATE_TEMP_INV
    return h, q, k, v, g, log_a


def _head_rmsnorm(o, gng):
    ms = jnp.mean(o * o, axis=-1, keepdims=True)
    return o * lax.rsqrt(ms + RMS_EPS) * gng


def _even_prompt_kernel(x_ref, ng_ref, wqk_ref, wv_ref, wg_ref, wal_ref, wcv_ref, wup_ref, bup_ref,
                        gng_ref, wsc_ref, wout_ref,
                        x1_ref, sgla_ref, sconv_ref,
                        st_ref, ubuf_ref, qe_ref, ke_ref, kd_ref, v_ref, dec_ref, mix_ref):
    tm = PROMPT_TILE
    t = pl.program_id(1)
    last_t = pl.num_programs(1) - 1

    @pl.when(t == 0)
    def _():
        st_ref[...] = jnp.zeros_like(st_ref)
        ubuf_ref[0:SCONV_HALO, :] = jnp.zeros((SCONV_HALO, CONV_W), F32)

    x = x_ref[...]
    h, q, k, v, g, log_a = _even_front(x, ng_ref[...], wqk_ref, wv_ref, wg_ref, wal_ref, wup_ref, bup_ref)

    row = lax.broadcasted_iota(jnp.int32, (tm, tm), 0)
    col = lax.broadcasted_iota(jnp.int32, (tm, tm), 1)
    same_chunk = (row >> 6) == (col >> 6)
    blk = jnp.where(same_chunk, 1.0, 0.0).astype(BF16)
    tri = jnp.where(same_chunk & (col <= row), 1.0, 0.0).astype(BF16)
    la_hi = log_a.astype(BF16)
    la_lo = (log_a - la_hi.astype(F32)).astype(BF16)
    b_cum = _dot(tri, la_hi) + _dot(tri, la_lo)
    b_tot = _dot(blk, la_hi) + _dot(blk, la_lo)

    qe_ref[...] = (q * jnp.exp(b_cum)).astype(BF16)
    ke_ref[...] = (k * jnp.exp(-b_cum)).astype(BF16)
    kd_ref[...] = (k * jnp.exp(b_tot - b_cum)).astype(BF16)
    dec_ref[...] = jnp.exp(b_tot)
    v_ref[...] = v.astype(BF16)

    crow = lax.broadcasted_iota(jnp.int32, (CHUNK, CHUNK), 0)
    ccol = lax.broadcasted_iota(jnp.int32, (CHUNK, CHUNK), 1)
    causal = ccol <= crow
    gng = gng_ref[...]

    def chunk_body(c, carry):
        r0 = pl.multiple_of(c * CHUNK, CHUNK)
        rows = pl.ds(r0, CHUNK)
        for hh in range(HEADS):
            kcols = slice(hh * DK, (hh + 1) * DK)
            vcols = slice(hh * DV, (hh + 1) * DV)
            qe = qe_ref[rows, kcols]
            ke = ke_ref[rows, kcols]
            kd = kd_ref[rows, kcols]
            vv = v_ref[rows, vcols]
            s = jnp.where(causal, _dot_nt(qe, ke), 0.0)
            st = st_ref[hh]
            o = _dot(s.astype(BF16), vv) + _dot_nt(qe, st.astype(BF16))
            dec = dec_ref[pl.ds(r0, 1), kcols]
            st_ref[hh] = st * dec + _dot_tn(vv, kd)
            mix_ref[rows, vcols] = _head_rmsnorm(o, gng)
        return carry

    lax.fori_loop(0, tm // CHUNK, chunk_body, 0)

    o_mix = (mix_ref[...] * _silu(g)).astype(BF16)

    hb = _dot(h, wcv_ref[:, 0:CONV_W])
    gate_c = _dot(h, wcv_ref[:, 2 * CONV_W:3 * CONV_W])
    u = gate_c * hb
    ubuf_ref[SCONV_HALO:SCONV_HALO + tm, :] = u
    y = (wsc_ref[2:3, :] * u
         + wsc_ref[1:2, :] * ubuf_ref[pl.ds(SCONV_HALO - 1, tm), :]
         + wsc_ref[0:1, :] * ubuf_ref[pl.ds(SCONV_HALO - 2, tm), :])
    gate_b = _dot(h, wcv_ref[:, CONV_W:2 * CONV_W])
    z_b = _dot(h, wcv_ref[:, 3 * CONV_W:4 * CONV_W])
    y = (gate_b * y * _silu(z_b)).astype(BF16)
    ubuf_ref[0:SCONV_HALO, :] = ubuf_ref[tm:tm + SCONV_HALO, :]

    out = _dot(o_mix, wout_ref[0:V_WIDTH, :]) + _dot(y, wout_ref[V_WIDTH:V_WIDTH + CONV_W, :])
    x1_ref[...] = x + out

    @pl.when(t == last_t)
    def _():
        for hh in range(HEADS):
            sgla_ref[hh] = st_ref[hh].T
        sconv_ref[...] = ubuf_ref[pl.ds(SCONV_HALO + tm - (SCONV_K - 1), SCONV_K - 1), :]


def _odd_tail(x, yc, z, lng, lnb, wout_ref, bout, fng):
    mu = jnp.mean(yc, axis=-1, keepdims=True)
    xc = yc - mu
    var = jnp.mean(xc * xc, axis=-1, keepdims=True)
    yn = xc * lax.rsqrt(var + LN_EPS) * lng + lnb
    act = (_silu(yn) * _silu(z)).astype(BF16)
    x2 = x + _dot(act, wout_ref[...]) + bout
    return _rmsnorm(x2, fng)


def _odd_prompt_kernel(x_ref, ng_ref, win_ref, bin_ref, wdw_ref, bdw_ref, lng_ref, lnb_ref, wout_ref,
                       bout_ref, fng_ref,
                       y_ref, cconv_ref,
                       ubuf_ref, yc_ref):
    tm = PROMPT_TILE
    t = pl.program_id(1)
    last_t = pl.num_programs(1) - 1

    @pl.when(t == 0)
    def _():
        ubuf_ref[0:CCONV_HALO, :] = jnp.zeros((CCONV_HALO, CONV_W), F32)

    x = x_ref[...]
    h = _rmsnorm(x, ng_ref[...]).astype(BF16)
    a = _dot(h, win_ref[:, 0:CONV_W]) + bin_ref[:, 0:CONV_W]
    a_gate = _dot(h, win_ref[:, CONV_W:2 * CONV_W]) + bin_ref[:, CONV_W:2 * CONV_W]
    ubuf_ref[CCONV_HALO:CCONV_HALO + tm, :] = a * jax.nn.sigmoid(a_gate)

    base = CCONV_HALO - (CCONV_K - 1)
    bdw = bdw_ref[...]
    for rb in range(tm // CCONV_ROWS):
        acc = bdw
        for j in range(CCONV_K):
            acc = acc + wdw_ref[j:j + 1, :] * ubuf_ref[pl.ds(rb * CCONV_ROWS + base + j, CCONV_ROWS), :]
        yc_ref[rb * CCONV_ROWS:(rb + 1) * CCONV_ROWS, :] = acc

    @pl.when(t == last_t)
    def _():
        cconv_ref[...] = ubuf_ref[pl.ds(CCONV_HALO + tm - (CCONV_K - 1), CCONV_K - 1), :]

    ubuf_ref[0:CCONV_HALO, :] = ubuf_ref[tm:tm + CCONV_HALO, :]

    z = _dot(h, win_ref[:, 2 * CONV_W:3 * CONV_W]) + bin_ref[:, 2 * CONV_W:3 * CONV_W]
    y_ref[...] = _odd_tail(x, yc_ref[...], z, lng_ref[...], lnb_ref[...], wout_ref, bout_ref[...],
                           fng_ref[...])


def _even_decode_front_kernel(x_ref, ng_ref, wqk_ref, wv_ref, wg_ref, wal_ref, wcv_ref, wup_ref, bup_ref,
                              wsc_ref, sbuf_ref,
                              q_ref, k_ref, a_ref, v_ref, sg_ref, y_ref, snew_ref):
    x = x_ref[...]
    h, q, k, v, g, log_a = _even_front(x, ng_ref[...], wqk_ref, wv_ref, wg_ref, wal_ref, wup_ref, bup_ref)
    q_ref[...] = q
    k_ref[...] = k
    a_ref[...] = jnp.exp(log_a)
    v_ref[...] = v
    sg_ref[...] = _silu(g)
    hb = _dot(h, wcv_ref[:, 0:CONV_W])
    gate_c = _dot(h, wcv_ref[:, 2 * CONV_W:3 * CONV_W])
    u = gate_c * hb
    prev2 = sbuf_ref[:, 0:CONV_W]
    prev1 = sbuf_ref[:, CONV_W:2 * CONV_W]
    y = wsc_ref[2:3, :] * u + wsc_ref[1:2, :] * prev1 + wsc_ref[0:1, :] * prev2
    gate_b = _dot(h, wcv_ref[:, CONV_W:2 * CONV_W])
    z_b = _dot(h, wcv_ref[:, 3 * CONV_W:4 * CONV_W])
    y_ref[...] = gate_b * y * _silu(z_b)
    snew_ref[:, 0:CONV_W] = prev1
    snew_ref[:, CONV_W:2 * CONV_W] = u


def _lane_bcast_column(row):
    return jnp.broadcast_to(row, (DK, DK)).T


def _gla_decode_kernel(q_ref, k_ref, a_ref, v_ref, s_ref, snew_ref, o_ref):
    for b in range(DECODE_STATE_BLOCK):
        for hh in range(HEADS):
            kcols = slice(hh * DK, (hh + 1) * DK)
            vcols = slice(hh * DV, (hh + 1) * DV)
            a_col = _lane_bcast_column(a_ref[b:b + 1, kcols])
            k_col = _lane_bcast_column(k_ref[b:b + 1, kcols])
            q_col = _lane_bcast_column(q_ref[b:b + 1, kcols])
            v_row = v_ref[b:b + 1, vcols]
            halves = []
            for half in range(DV // DK):
                lanes = slice(half * DK, (half + 1) * DK)
                s_new = a_col * s_ref[b, hh, :, lanes] + k_col * v_row[:, lanes]
                snew_ref[b, hh, :, lanes] = s_new
                halves.append(jnp.sum(q_col * s_new, axis=0, keepdims=True))
            o_ref[b:b + 1, vcols] = jnp.concatenate(halves, axis=1)


def _even_decode_out_kernel(x_ref, o_ref, sg_ref, y_ref, gng_ref, wout_ref, x1_ref):
    gng = gng_ref[...]
    parts = [_head_rmsnorm(o_ref[:, hh * DV:(hh + 1) * DV], gng) for hh in range(HEADS)]
    o_mix = (jnp.concatenate(parts, axis=1) * sg_ref[...]).astype(BF16)
    out = _dot(o_mix, wout_ref[0:V_WIDTH, :]) + _dot(y_ref[...].astype(BF16), wout_ref[V_WIDTH:V_WIDTH + CONV_W, :])
    x1_ref[...] = x_ref[...] + out


def _odd_decode_kernel(x_ref, ng_ref, win_ref, bin_ref, wdw_ref, bdw_ref, lng_ref, lnb_ref, wout_ref,
                       bout_ref, fng_ref, cbuf_ref,
                       y_ref, cnew_ref):
    x = x_ref[...]
    h = _rmsnorm(x, ng_ref[...]).astype(BF16)
    a = _dot(h, win_ref[:, 0:CONV_W]) + bin_ref[:, 0:CONV_W]
    a_gate = _dot(h, win_ref[:, CONV_W:2 * CONV_W]) + bin_ref[:, CONV_W:2 * CONV_W]
    u = a * jax.nn.sigmoid(a_gate)
    acc = bdw_ref[...] + wdw_ref[CCONV_K - 1:CCONV_K, :] * u
    for j in range(CCONV_K - 1):
        tap = cbuf_ref[:, j * CONV_W:(j + 1) * CONV_W]
        acc = acc + wdw_ref[j:j + 1, :] * tap
        if j >= 1:
            cnew_ref[:, (j - 1) * CONV_W:j * CONV_W] = tap
    cnew_ref[:, (CCONV_K - 2) * CONV_W:(CCONV_K - 1) * CONV_W] = u
    z = _dot(h, win_ref[:, 2 * CONV_W:3 * CONV_W]) + bin_ref[:, 2 * CONV_W:3 * CONV_W]
    y_ref[...] = _odd_tail(x, acc, z, lng_ref[...], lnb_ref[...], wout_ref, bout_ref[...], fng_ref[...])


def _const_spec(shape):
    nd = len(shape)
    return pl.BlockSpec(shape, lambda *_: (0,) * nd, pipeline_mode=pl.Buffered(1))


def _row(v):
    return v.reshape(1, -1)


def kernel(x_prompt, x_sample, state_gla, state_sconv, state_cconv, norm_g, w_in_a, w_gate_up, b_gate_up, gla_norm_g, w_sconv, w_out_a, w_in_c, b_in_c, w_dwconv, b_dwconv, ln_g, ln_b, w_out_c, b_out_c, final_norm_g):
    bsz, seq, d = x_prompt.shape
    dec_b = x_sample.shape[0]
    assert d == D_MODEL and seq % PROMPT_TILE == 0 and x_sample.shape[1] == 1
    assert w_in_a.shape[0] == 1 and w_in_c.shape[0] == 1 and norm_g.shape[0] == 2
    n_t = seq // PROMPT_TILE

    w_in = w_in_a[0]
    c0 = 2 * QK_WIDTH
    c1 = c0 + V_WIDTH
    c2 = c1 + V_WIDTH
    c3 = c2 + GATE_RANK
    wqk = w_in[:, :c0].astype(BF16)
    wv = w_in[:, c0:c1].astype(BF16)
    wg = w_in[:, c1:c2].astype(BF16)
    wal = jnp.pad(w_in[:, c2:c3], ((0, 0), (0, GATE_RANK_PAD - GATE_RANK))).astype(BF16)
    wcv = w_in[:, c3:].astype(BF16)
    wup = jnp.pad(w_gate_up[0], ((0, GATE_RANK_PAD - GATE_RANK), (0, 0))).astype(BF16)
    bup = _row(b_gate_up[0])
    gng = _row(gla_norm_g[0])
    wsc = w_sconv[0]
    wout_a = w_out_a[0].astype(BF16)
    ng0 = _row(norm_g[0])
    ng1 = _row(norm_g[1])
    win_c = w_in_c[0].astype(BF16)
    bin_c = _row(b_in_c[0])
    wdw = w_dwconv[0]
    bdw = _row(b_dwconv[0])
    lng = _row(ln_g[0])
    lnb = _row(ln_b[0])
    wout_c = w_out_c[0].astype(BF16)
    bout = _row(b_out_c[0])
    fng = _row(final_norm_g)

    even_weights = (ng0, wqk, wv, wg, wal, wcv, wup, bup)
    even_weight_specs = [_const_spec(w.shape) for w in even_weights]
    odd_weights = (ng1, win_c, bin_c, wdw, bdw, lng, lnb, wout_c, bout, fng)
    odd_weight_specs = [_const_spec(w.shape) for w in odd_weights]

    tile_spec = pl.BlockSpec((None, PROMPT_TILE, D_MODEL), lambda b, t: (b, t, 0))
    prompt_params = pltpu.CompilerParams(dimension_semantics=("arbitrary", "arbitrary"),
                                         vmem_limit_bytes=VMEM_LIMIT)

    x1_p, gla_p, sconv_p = pl.pallas_call(
        _even_prompt_kernel,
        grid=(bsz, n_t),
        in_specs=[tile_spec] + even_weight_specs + [_const_spec(gng.shape), _const_spec(wsc.shape),
                                                    _const_spec(wout_a.shape)],
        out_specs=[tile_spec,
                   pl.BlockSpec((None, None, HEADS, DK, DV), lambda b, t: (0, b, 0, 0, 0)),
                   pl.BlockSpec((None, None, SCONV_K - 1, CONV_W), lambda b, t: (0, b, 0, 0))],
        out_shape=[jax.ShapeDtypeStruct((bsz, seq, D_MODEL), F32),
                   jax.ShapeDtypeStruct((1, bsz, HEADS, DK, DV), F32),
                   jax.ShapeDtypeStruct((1, bsz, SCONV_K - 1, CONV_W), F32)],
        scratch_shapes=[pltpu.VMEM((HEADS, DV, DK), F32),
                        pltpu.VMEM((PROMPT_TILE + SCONV_HALO, CONV_W), F32),
                        pltpu.VMEM((PROMPT_TILE, QK_WIDTH), BF16),
                        pltpu.VMEM((PROMPT_TILE, QK_WIDTH), BF16),
                        pltpu.VMEM((PROMPT_TILE, QK_WIDTH), BF16),
                        pltpu.VMEM((PROMPT_TILE, V_WIDTH), BF16),
                        pltpu.VMEM((PROMPT_TILE, QK_WIDTH), F32),
                        pltpu.VMEM((PROMPT_TILE, V_WIDTH), F32)],
        compiler_params=prompt_params,
        name="even_prompt",
    )(x_prompt, *even_weights, gng, wsc, wout_a)

    y_p, cconv_p = pl.pallas_call(
        _odd_prompt_kernel,
        grid=(bsz, n_t),
        in_specs=[tile_spec] + odd_weight_specs,
        out_specs=[tile_spec,
                   pl.BlockSpec((None, None, CCONV_K - 1, CONV_W), lambda b, t: (0, b, 0, 0))],
        out_shape=[jax.ShapeDtypeStruct((bsz, seq, D_MODEL), F32),
                   jax.ShapeDtypeStruct((1, bsz, CCONV_K - 1, CONV_W), F32)],
        scratch_shapes=[pltpu.VMEM((PROMPT_TILE + CCONV_HALO, CONV_W), F32),
                        pltpu.VMEM((PROMPT_TILE, CONV_W), F32)],
        compiler_params=prompt_params,
        name="odd_prompt",
    )(x1_p, *odd_weights)

    xs = x_sample.reshape(dec_b, D_MODEL)
    sbuf = state_sconv.reshape(dec_b, (SCONV_K - 1) * CONV_W)
    single = pltpu.CompilerParams(vmem_limit_bytes=VMEM_LIMIT)
    q_s, k_s, a_s, v_s, sg_s, ysc_s, sconv_s = pl.pallas_call(
        _even_decode_front_kernel,
        out_shape=[jax.ShapeDtypeStruct((dec_b, QK_WIDTH), F32),
                   jax.ShapeDtypeStruct((dec_b, QK_WIDTH), F32),
                   jax.ShapeDtypeStruct((dec_b, QK_WIDTH), F32),
                   jax.ShapeDtypeStruct((dec_b, V_WIDTH), F32),
                   jax.ShapeDtypeStruct((dec_b, V_WIDTH), F32),
                   jax.ShapeDtypeStruct((dec_b, CONV_W), F32),
                   jax.ShapeDtypeStruct((dec_b, (SCONV_K - 1) * CONV_W), F32)],
        compiler_params=single,
        name="even_decode_front",
    )(xs, *even_weights, wsc, sbuf)

    sb = DECODE_STATE_BLOCK
    assert dec_b % sb == 0
    vec_spec = lambda w: pl.BlockSpec((sb, w), lambda i: (i, 0))
    state_spec = pl.BlockSpec((sb, HEADS, DK, DV), lambda i: (i, 0, 0, 0))
    gla_s, o_s = pl.pallas_call(
        _gla_decode_kernel,
        grid=(dec_b // sb,),
        in_specs=[vec_spec(QK_WIDTH), vec_spec(QK_WIDTH), vec_spec(QK_WIDTH), vec_spec(V_WIDTH), state_spec],
        out_specs=[state_spec, vec_spec(V_WIDTH)],
        out_shape=[jax.ShapeDtypeStruct((dec_b, HEADS, DK, DV), F32),
                   jax.ShapeDtypeStruct((dec_b, V_WIDTH), F32)],
        compiler_params=pltpu.CompilerParams(dimension_semantics=("arbitrary",),
                                             vmem_limit_bytes=VMEM_LIMIT),
        name="gla_decode",
    )(q_s, k_s, a_s, v_s, state_gla[0])

    x1_s = pl.pallas_call(
        _even_decode_out_kernel,
        out_shape=jax.ShapeDtypeStruct((dec_b, D_MODEL), F32),
        compiler_params=single,
        name="even_decode_out",
    )(xs, o_s, sg_s, ysc_s, gng, wout_a)

    ob = DECODE_ODD_BLOCK
    assert dec_b % ob == 0
    hist_w = (CCONV_K - 1) * CONV_W
    cbuf = state_cconv.reshape(dec_b, hist_w)
    rows_spec = pl.BlockSpec((ob, D_MODEL), lambda i: (i, 0))
    hist_spec = pl.BlockSpec((ob, hist_w), lambda i: (i, 0))
    y_s, cconv_s = pl.pallas_call(
        _odd_decode_kernel,
        grid=(dec_b // ob,),
        in_specs=[rows_spec] + odd_weight_specs + [hist_spec],
        out_specs=[rows_spec, hist_spec],
        out_shape=[jax.ShapeDtypeStruct((dec_b, D_MODEL), F32),
                   jax.ShapeDtypeStruct((dec_b, hist_w), F32)],
        compiler_params=pltpu.CompilerParams(dimension_semantics=("arbitrary",),
                                             vmem_limit_bytes=VMEM_LIMIT),
        name="odd_decode",
    )(x1_s, *odd_weights, cbuf)

    return (y_p,
            y_s.reshape(dec_b, 1, D_MODEL),
            gla_p,
            sconv_p,
            cconv_p,
            gla_s.reshape(1, dec_b, HEADS, DK, DV),
            sconv_s.reshape(1, dec_b, SCONV_K - 1, CONV_W),
            cconv_s.reshape(1, dec_b, CCONV_K - 1, CONV_W))
```

```python
import functools

import jax
import jax.numpy as jnp
from jax import lax
from jax.experimental import pallas as pl
from jax.experimental.pallas import tpu as pltpu

F32 = jnp.float32
BF16 = jnp.bfloat16

D_MODEL = 1024
HEADS = 4
DK = 128
DV = 256
QK_WIDTH = HEADS * DK
V_WIDTH = HEADS * DV
GATE_RANK = 16
GATE_RANK_PAD = 128
GATE_TEMP_INV = 1.0 / 16.0
CHUNK = 64
SCONV_K = 3
CCONV_K = 31
CONV_W = 1024
RMS_EPS = 1e-6
LN_EPS = 1e-5
Q_SCALE = DK ** -0.5

PROMPT_TILE = 256
CCONV_HALO = 32
SCONV_HALO = 8
SUBLANES = 8
LANES = 128
LANE_TILES = CONV_W // LANES
CCONV_TIME_BLOCK = 16
DECODE_STATE_BLOCK = 8
DECODE_ODD_BLOCK = 32
VMEM_LIMIT = 56 * 1024 * 1024


def _dot(a, b):
    return jnp.dot(a, b, preferred_element_type=F32)


def _dot_nt(a, b):
    return lax.dot_general(a, b, (((1,), (1,)), ((), ())), preferred_element_type=F32)


def _dot_tn(a, b):
    return lax.dot_general(a, b, (((0,), (0,)), ((), ())), preferred_element_type=F32)


def _rmsnorm(x, g):
    ms = jnp.mean(x * x, axis=-1, keepdims=True)
    return x * lax.rsqrt(ms + RMS_EPS) * g


def _silu(x):
    return x * jax.nn.sigmoid(x)


def _log_sigmoid(x):
    return -(jnp.maximum(-x, 0.0) + jnp.log1p(jnp.exp(-jnp.abs(x))))


def _even_front(x, ng, wqk_ref, wv_ref, wg_ref, wal_ref, wup_ref, bup_ref):
    h = _rmsnorm(x, ng).astype(BF16)
    qk = _dot(h, wqk_ref[...])
    q = qk[:, :QK_WIDTH] * Q_SCALE
    k = qk[:, QK_WIDTH:]
    v = _dot(h, wv_ref[...])
    g = _dot(h, wg_ref[...])
    a_low = _dot(h, wal_ref[...]).astype(BF16)
    logit = _dot(a_low, wup_ref[...]) + bup_ref[...]
    log_a = _log_sigmoid(logit) * GATE_TEMP_INV
    return h, q, k, v, g, log_a


def _head_rmsnorm(o, gng):
    ms = jnp.mean(o * o, axis=-1, keepdims=True)
    return o * lax.rsqrt(ms + RMS_EPS) * gng


def _even_prompt_kernel(x_ref, ng_ref, wqk_ref, wv_ref, wg_ref, wal_ref, wcv_ref, wup_ref, bup_ref,
                        gng_ref, wsc_ref, wout_ref,
                        x1_ref, sgla_ref, sconv_ref,
                        st_ref, ubuf_ref, qe_ref, ke_ref, kd_ref, v_ref, dec_ref, mix_ref):
    tm = PROMPT_TILE
    t = pl.program_id(1)
    last_t = pl.num_programs(1) - 1

    @pl.when(t == 0)
    def _():
        st_ref[...] = jnp.zeros_like(st_ref)
        ubuf_ref[0:SCONV_HALO, :] = jnp.zeros((SCONV_HALO, CONV_W), F32)

    x = x_ref[...]
    h, q, k, v, g, log_a = _even_front(x, ng_ref[...], wqk_ref, wv_ref, wg_ref, wal_ref, wup_ref, bup_ref)

    row = lax.broadcasted_iota(jnp.int32, (tm, tm), 0)
    col = lax.broadcasted_iota(jnp.int32, (tm, tm), 1)
    same_chunk = (row >> 6) == (col >> 6)
    in_chunk_causal = same_chunk & (col <= row)
    blk = jnp.where(same_chunk, 1.0, 0.0).astype(BF16)
    tri = jnp.where(in_chunk_causal, 1.0, 0.0).astype(BF16)
    la_hi = log_a.astype(BF16)
    la_lo = (log_a - la_hi.astype(F32)).astype(BF16)
    b_cum = _dot(tri, la_hi) + _dot(tri, la_lo)
    b_tot = _dot(blk, la_hi) + _dot(blk, la_lo)

    qe_ref[...] = (q * jnp.exp(b_cum)).astype(BF16)
    ke_ref[...] = (k * jnp.exp(-b_cum)).astype(BF16)
    kd_ref[...] = (k * jnp.exp(b_tot - b_cum)).astype(BF16)
    dec_ref[...] = jnp.exp(b_tot)
    v_ref[...] = v.astype(BF16)

    gng = gng_ref[...]
    for hh in range(HEADS):
        kcols = slice(hh * DK, (hh + 1) * DK)
        vcols = slice(hh * DV, (hh + 1) * DV)
        s = jnp.where(in_chunk_causal, _dot_nt(qe_ref[:, kcols], ke_ref[:, kcols]), 0.0)
        o_intra = _dot(s.astype(BF16), v_ref[:, vcols])
        st = st_ref[hh]
        for c in range(tm // CHUNK):
            rows = slice(c * CHUNK, (c + 1) * CHUNK)
            o = o_intra[rows, :] + _dot_nt(qe_ref[rows, kcols], st.astype(BF16))
            mix_ref[rows, vcols] = _head_rmsnorm(o, gng)
            dec = dec_ref[c * CHUNK:c * CHUNK + 1, kcols]
            st = st * dec + _dot_tn(v_ref[rows, vcols], kd_ref[rows, kcols])
        st_ref[hh] = st

    o_mix = (mix_ref[...] * _silu(g)).astype(BF16)

    hb = _dot(h, wcv_ref[:, 0:CONV_W])
    gate_c = _dot(h, wcv_ref[:, 2 * CONV_W:3 * CONV_W])
    u = gate_c * hb
    ubuf_ref[SCONV_HALO:SCONV_HALO + tm, :] = u
    y = (wsc_ref[2:3, :] * u
         + wsc_ref[1:2, :] * ubuf_ref[pl.ds(SCONV_HALO - 1, tm), :]
         + wsc_ref[0:1, :] * ubuf_ref[pl.ds(SCONV_HALO - 2, tm), :])
    gate_b = _dot(h, wcv_ref[:, CONV_W:2 * CONV_W])
    z_b = _dot(h, wcv_ref[:, 3 * CONV_W:4 * CONV_W])
    y = (gate_b * y * _silu(z_b)).astype(BF16)
    ubuf_ref[0:SCONV_HALO, :] = ubuf_ref[tm:tm + SCONV_HALO, :]

    out = _dot(o_mix, wout_ref[0:V_WIDTH, :]) + _dot(y, wout_ref[V_WIDTH:V_WIDTH + CONV_W, :])
    x1_ref[...] = x + out

    @pl.when(t == last_t)
    def _():
        for hh in range(HEADS):
            sgla_ref[hh] = st_ref[hh].T
        sconv_ref[...] = ubuf_ref[pl.ds(SCONV_HALO + tm - (SCONV_K - 1), SCONV_K - 1), :]


def _odd_tail(x, yc, z, lng, lnb, wout_ref, bout, fng):
    mu = jnp.mean(yc, axis=-1, keepdims=True)
    xc = yc - mu
    var = jnp.mean(xc * xc, axis=-1, keepdims=True)
    yn = xc * lax.rsqrt(var + LN_EPS) * lng + lnb
    act = (_silu(yn) * _silu(z)).astype(BF16)
    x2 = x + _dot(act, wout_ref[...]) + bout
    return _rmsnorm(x2, fng)


def _odd_prompt_kernel(x_ref, ng_ref, win_ref, bin_ref, wdw_ref, bdw_ref, lng_ref, lnb_ref, wout_ref,
                       bout_ref, fng_ref,
                       y_ref, cconv_ref,
                       u3_ref, y3_ref, yc_ref):
    tm = PROMPT_TILE
    t = pl.program_id(1)
    last_t = pl.num_programs(1) - 1

    @pl.when(t == 0)
    def _():
        u3_ref[0:CCONV_HALO * LANE_TILES, :] = jnp.zeros((CCONV_HALO * LANE_TILES, LANES), F32)

    x = x_ref[...]
    h = _rmsnorm(x, ng_ref[...]).astype(BF16)
    a = _dot(h, win_ref[:, 0:CONV_W]) + bin_ref[:, 0:CONV_W]
    a_gate = _dot(h, win_ref[:, CONV_W:2 * CONV_W]) + bin_ref[:, CONV_W:2 * CONV_W]
    u = a * jax.nn.sigmoid(a_gate)

    @pl.when(t == last_t)
    def _():
        cconv_ref[...] = u[tm - (CCONV_K - 1):, :]

    for r8 in range(tm // SUBLANES):
        for c in range(LANE_TILES):
            dst = pl.ds((CCONV_HALO + r8 * SUBLANES) * LANE_TILES + c, SUBLANES, stride=LANE_TILES)
            u3_ref[dst, :] = u[r8 * SUBLANES:(r8 + 1) * SUBLANES, c * LANES:(c + 1) * LANES]

    base = CCONV_HALO - (CCONV_K - 1)
    tb = CCONV_TIME_BLOCK
    bdw = bdw_ref[...]
    for blk in range(tm // tb):
        acc = jnp.broadcast_to(bdw[None], (tb, LANE_TILES, LANES))
        for j in range(CCONV_K):
            rows = pl.ds((blk * tb + base + j) * LANE_TILES, tb * LANE_TILES)
            w_j = wdw_ref[j * LANE_TILES:(j + 1) * LANE_TILES, :]
            acc = acc + w_j[None] * u3_ref[rows, :].reshape(tb, LANE_TILES, LANES)
        y3_ref[blk * tb * LANE_TILES:(blk + 1) * tb * LANE_TILES, :] = acc.reshape(tb * LANE_TILES, LANES)

    u3_ref[0:CCONV_HALO * LANE_TILES, :] = u3_ref[tm * LANE_TILES:(tm + CCONV_HALO) * LANE_TILES, :]

    for r8 in range(tm // SUBLANES):
        for c in range(LANE_TILES):
            src = pl.ds(r8 * SUBLANES * LANE_TILES + c, SUBLANES, stride=LANE_TILES)
            yc_ref[r8 * SUBLANES:(r8 + 1) * SUBLANES, c * LANES:(c + 1) * LANES] = y3_ref[src, :]

    z = _dot(h, win_ref[:, 2 * CONV_W:3 * CONV_W]) + bin_ref[:, 2 * CONV_W:3 * CONV_W]
    y_ref[...] = _odd_tail(x, yc_ref[...], z, lng_ref[...], lnb_ref[...], wout_ref, bout_ref[...],
                           fng_ref[...])


def _even_decode_front_kernel(x_ref, ng_ref, wqk_ref, wv_ref, wg_ref, wal_ref, wcv_ref, wup_ref, bup_ref,
                              wsc_ref, sbuf_ref,
                              q_ref, k_ref, a_ref, v_ref, sg_ref, y_ref, snew_ref):
    x = x_ref[...]
    h, q, k, v, g, log_a = _even_front(x, ng_ref[...], wqk_ref, wv_ref, wg_ref, wal_ref, wup_ref, bup_ref)
    q_ref[...] = q
    k_ref[...] = k
    a_ref[...] = jnp.exp(log_a)
    v_ref[...] = v
    sg_ref[...] = _silu(g)
    hb = _dot(h, wcv_ref[:, 0:CONV_W])
    gate_c = _dot(h, wcv_ref[:, 2 * CONV_W:3 * CONV_W])
    u = gate_c * hb
    prev2 = sbuf_ref[:, 0:CONV_W]
    prev1 = sbuf_ref[:, CONV_W:2 * CONV_W]
    y = wsc_ref[2:3, :] * u + wsc_ref[1:2, :] * prev1 + wsc_ref[0:1, :] * prev2
    gate_b = _dot(h, wcv_ref[:, CONV_W:2 * CONV_W])
    z_b = _dot(h, wcv_ref[:, 3 * CONV_W:4 * CONV_W])
    y_ref[...] = gate_b * y * _silu(z_b)
    snew_ref[:, 0:CONV_W] = prev1
    snew_ref[:, CONV_W:2 * CONV_W] = u


def _lane_bcast_column(row):
    return jnp.broadcast_to(row, (DK, DK)).T


def _gla_decode_kernel(q_ref, k_ref, a_ref, v_ref, s_ref, snew_ref, o_ref):
    for b in range(DECODE_STATE_BLOCK):
        for hh in range(HEADS):
            kcols = slice(hh * DK, (hh + 1) * DK)
            vcols = slice(hh * DV, (hh + 1) * DV)
            a_col = _lane_bcast_column(a_ref[b:b + 1, kcols])
            k_col = _lane_bcast_column(k_ref[b:b + 1, kcols])
            q_col = _lane_bcast_column(q_ref[b:b + 1, kcols])
            v_row = v_ref[b:b + 1, vcols]
            halves = []
            for half in range(DV // DK):
                lanes = slice(half * DK, (half + 1) * DK)
                s_new = a_col * s_ref[b, hh, :, lanes] + k_col * v_row[:, lanes]
                snew_ref[b, hh, :, lanes] = s_new
                halves.append(jnp.sum(q_col * s_new, axis=0, keepdims=True))
            o_ref[b:b + 1, vcols] = jnp.concatenate(halves, axis=1)


def _even_decode_out_kernel(x_ref, o_ref, sg_ref, y_ref, gng_ref, wout_ref, x1_ref):
    gng = gng_ref[...]
    parts = [_head_rmsnorm(o_ref[:, hh * DV:(hh + 1) * DV], gng) for hh in range(HEADS)]
    o_mix = (jnp.concatenate(parts, axis=1) * sg_ref[...]).astype(BF16)
    out = _dot(o_mix, wout_ref[0:V_WIDTH, :]) + _dot(y_ref[...].astype(BF16), wout_ref[V_WIDTH:V_WIDTH + CONV_W, :])
    x1_ref[...] = x_ref[...] + out


def _odd_decode_kernel(x_ref, ng_ref, win_ref, bin_ref, wdw_ref, bdw_ref, lng_ref, lnb_ref, wout_ref,
                       bout_ref, fng_ref, cbuf_ref,
                       y_ref, cnew_ref,
                       u_ref, hsum_ref):
    n_hist = CCONV_K - 1
    x = x_ref[...]
    h = _rmsnorm(x, ng_ref[...]).astype(BF16)
    a = _dot(h, win_ref[:, 0:CONV_W]) + bin_ref[:, 0:CONV_W]
    a_gate = _dot(h, win_ref[:, CONV_W:2 * CONV_W]) + bin_ref[:, CONV_W:2 * CONV_W]
    u_ref[...] = a * jax.nn.sigmoid(a_gate)
    w_hist = wdw_ref[0:n_hist, :]

    def per_sequence(b, carry):
        hist = cbuf_ref[b]
        hsum_ref[pl.ds(b, 1), :] = jnp.sum(hist * w_hist, axis=0, keepdims=True)
        cnew_ref[b, 0:n_hist - 1, :] = hist[1:, :]
        cnew_ref[b, n_hist - 1:n_hist, :] = u_ref[pl.ds(b, 1), :]
        return carry

    lax.fori_loop(0, DECODE_ODD_BLOCK, per_sequence, 0)
    acc = bdw_ref[...] + wdw_ref[n_hist:CCONV_K, :] * u_ref[...] + hsum_ref[...]
    z = _dot(h, win_ref[:, 2 * CONV_W:3 * CONV_W]) + bin_ref[:, 2 * CONV_W:3 * CONV_W]
    y_ref[...] = _odd_tail(x, acc, z, lng_ref[...], lnb_ref[...], wout_ref, bout_ref[...], fng_ref[...])


def _const_spec(shape):
    nd = len(shape)
    return pl.BlockSpec(shape, lambda *_: (0,) * nd, pipeline_mode=pl.Buffered(1))


def _row(v):
    return v.reshape(1, -1)


def kernel(x_prompt, x_sample, state_gla, state_sconv, state_cconv, norm_g, w_in_a, w_gate_up, b_gate_up, gla_norm_g, w_sconv, w_out_a, w_in_c, b_in_c, w_dwconv, b_dwconv, ln_g, ln_b, w_out_c, b_out_c, final_norm_g):
    bsz, seq, d = x_prompt.shape
    dec_b = x_sample.shape[0]
    assert d == D_MODEL and seq % PROMPT_TILE == 0 and x_sample.shape[1] == 1
    assert w_in_a.shape[0] == 1 and w_in_c.shape[0] == 1 and norm_g.shape[0] == 2
    n_t = seq // PROMPT_TILE

    w_in = w_in_a[0]
    c0 = 2 * QK_WIDTH
    c1 = c0 + V_WIDTH
    c2 = c1 + V_WIDTH
    c3 = c2 + GATE_RANK
    wqk = w_in[:, :c0].astype(BF16)
    wv = w_in[:, c0:c1].astype(BF16)
    wg = w_in[:, c1:c2].astype(BF16)
    wal = jnp.pad(w_in[:, c2:c3], ((0, 0), (0, GATE_RANK_PAD - GATE_RANK))).astype(BF16)
    wcv = w_in[:, c3:].astype(BF16)
    wup = jnp.pad(w_gate_up[0], ((0, GATE_RANK_PAD - GATE_RANK), (0, 0))).astype(BF16)
    bup = _row(b_gate_up[0])
    gng = _row(gla_norm_g[0])
    wsc = w_sconv[0]
    wout_a = w_out_a[0].astype(BF16)
    ng0 = _row(norm_g[0])
    ng1 = _row(norm_g[1])
    win_c = w_in_c[0].astype(BF16)
    bin_c = _row(b_in_c[0])
    wdw = w_dwconv[0]
    bdw = _row(b_dwconv[0])
    lng = _row(ln_g[0])
    lnb = _row(ln_b[0])
    wout_c = w_out_c[0].astype(BF16)
    bout = _row(b_out_c[0])
    fng = _row(final_norm_g)

    even_weights = (ng0, wqk, wv, wg, wal, wcv, wup, bup)
    even_weight_specs = [_const_spec(w.shape) for w in even_weights]
    odd_weights = (ng1, win_c, bin_c, wdw, bdw, lng, lnb, wout_c, bout, fng)
    odd_weight_specs = [_const_spec(w.shape) for w in odd_weights]
    wdw3 = wdw.reshape(CCONV_K * LANE_TILES, LANES)
    bdw3 = b_dwconv[0].reshape(LANE_TILES, LANES)
    odd_prompt_weights = (ng1, win_c, bin_c, wdw3, bdw3, lng, lnb, wout_c, bout, fng)
    odd_prompt_weight_specs = [_const_spec(w.shape) for w in odd_prompt_weights]

    tile_spec = pl.BlockSpec((None, PROMPT_TILE, D_MODEL), lambda b, t: (b, t, 0))
    prompt_params = pltpu.CompilerParams(dimension_semantics=("arbitrary", "arbitrary"),
                                         vmem_limit_bytes=VMEM_LIMIT)

    x1_p, gla_p, sconv_p = pl.pallas_call(
        _even_prompt_kernel,
        grid=(bsz, n_t),
        in_specs=[tile_spec] + even_weight_specs + [_const_spec(gng.shape), _const_spec(wsc.shape),
                                                    _const_spec(wout_a.shape)],
        out_specs=[tile_spec,
                   pl.BlockSpec((None, None, HEADS, DK, DV), lambda b, t: (0, b, 0, 0, 0)),
                   pl.BlockSpec((None, None, SCONV_K - 1, CONV_W), lambda b, t: (0, b, 0, 0))],
        out_shape=[jax.ShapeDtypeStruct((bsz, seq, D_MODEL), F32),
                   jax.ShapeDtypeStruct((1, bsz, HEADS, DK, DV), F32),
                   jax.ShapeDtypeStruct((1, bsz, SCONV_K - 1, CONV_W), F32)],
        scratch_shapes=[pltpu.VMEM((HEADS, DV, DK), F32),
                        pltpu.VMEM((PROMPT_TILE + SCONV_HALO, CONV_W), F32),
                        pltpu.VMEM((PROMPT_TILE, QK_WIDTH), BF16),
                        pltpu.VMEM((PROMPT_TILE, QK_WIDTH), BF16),
                        pltpu.VMEM((PROMPT_TILE, QK_WIDTH), BF16),
                        pltpu.VMEM((PROMPT_TILE, V_WIDTH), BF16),
                        pltpu.VMEM((PROMPT_TILE, QK_WIDTH), F32),
                        pltpu.VMEM((PROMPT_TILE, V_WIDTH), F32)],
        compiler_params=prompt_params,
        name="even_prompt",
    )(x_prompt, *even_weights, gng, wsc, wout_a)

    y_p, cconv_p = pl.pallas_call(
        _odd_prompt_kernel,
        grid=(bsz, n_t),
        in_specs=[tile_spec] + odd_prompt_weight_specs,
        out_specs=[tile_spec,
                   pl.BlockSpec((None, None, CCONV_K - 1, CONV_W), lambda b, t: (0, b, 0, 0))],
        out_shape=[jax.ShapeDtypeStruct((bsz, seq, D_MODEL), F32),
                   jax.ShapeDtypeStruct((1, bsz, CCONV_K - 1, CONV_W), F32)],
        scratch_shapes=[pltpu.VMEM(((PROMPT_TILE + CCONV_HALO) * LANE_TILES, LANES), F32),
                        pltpu.VMEM((PROMPT_TILE * LANE_TILES, LANES), F32),
                        pltpu.VMEM((PROMPT_TILE, CONV_W), F32)],
        compiler_params=prompt_params,
        name="odd_prompt",
    )(x1_p, *odd_prompt_weights)

    xs = x_sample.reshape(dec_b, D_MODEL)
    sbuf = state_sconv.reshape(dec_b, (SCONV_K - 1) * CONV_W)
    single = pltpu.CompilerParams(vmem_limit_bytes=VMEM_LIMIT)
    q_s, k_s, a_s, v_s, sg_s, ysc_s, sconv_s = pl.pallas_call(
        _even_decode_front_kernel,
        out_shape=[jax.ShapeDtypeStruct((dec_b, QK_WIDTH), F32),
                   jax.ShapeDtypeStruct((dec_b, QK_WIDTH), F32),
                   jax.ShapeDtypeStruct((dec_b, QK_WIDTH), F32),
                   jax.ShapeDtypeStruct((dec_b, V_WIDTH), F32),
                   jax.ShapeDtypeStruct((dec_b, V_WIDTH), F32),
                   jax.ShapeDtypeStruct((dec_b, CONV_W), F32),
                   jax.ShapeDtypeStruct((dec_b, (SCONV_K - 1) * CONV_W), F32)],
        compiler_params=single,
        name="even_decode_front",
    )(xs, *even_weights, wsc, sbuf)

    sb = DECODE_STATE_BLOCK
    assert dec_b % sb == 0
    vec_spec = lambda w: pl.BlockSpec((sb, w), lambda i: (i, 0))
    state_spec = pl.BlockSpec((sb, HEADS, DK, DV), lambda i: (i, 0, 0, 0))
    gla_s, o_s = pl.pallas_call(
        _gla_decode_kernel,
        grid=(dec_b // sb,),
        in_specs=[vec_spec(QK_WIDTH), vec_spec(QK_WIDTH), vec_spec(QK_WIDTH), vec_spec(V_WIDTH), state_spec],
        out_specs=[state_spec, vec_spec(V_WIDTH)],
        out_shape=[jax.ShapeDtypeStruct((dec_b, HEADS, DK, DV), F32),
                   jax.ShapeDtypeStruct((dec_b, V_WIDTH), F32)],
        compiler_params=pltpu.CompilerParams(dimension_semantics=("arbitrary",),
                                             vmem_limit_bytes=VMEM_LIMIT),
        name="gla_decode",
    )(q_s, k_s, a_s, v_s, state_gla[0])

    x1_s = pl.pallas_call(
        _even_decode_out_kernel,
        out_shape=jax.ShapeDtypeStruct((dec_b, D_MODEL), F32),
        compiler_params=single,
        name="even_decode_out",
    )(xs, o_s, sg_s, ysc_s, gng, wout_a)

    ob = DECODE_ODD_BLOCK
    assert dec_b % ob == 0
    rows_spec = pl.BlockSpec((ob, D_MODEL), lambda i: (i, 0))
    hist_spec = pl.BlockSpec((None, ob, CCONV_K - 1, CONV_W), lambda i: (0, i, 0, 0))
    y_s, cconv_s = pl.pallas_call(
        _odd_decode_kernel,
        grid=(dec_b // ob,),
        in_specs=[rows_spec] + odd_weight_specs + [hist_spec],
        out_specs=[rows_spec, hist_spec],
        out_shape=[jax.ShapeDtypeStruct((dec_b, D_MODEL), F32),
                   jax.ShapeDtypeStruct((1, dec_b, CCONV_K - 1, CONV_W), F32)],
        scratch_shapes=[pltpu.VMEM((ob, CONV_W), F32),
                        pltpu.VMEM((ob, CONV_W), F32)],
        compiler_params=pltpu.CompilerParams(dimension_semantics=("arbitrary",),
                                             vmem_limit_bytes=VMEM_LIMIT),
        name="odd_decode",
    )(x1_s, *odd_weights, state_cconv)

    return (y_p,
            y_s.reshape(dec_b, 1, D_MODEL),
            gla_p,
            sconv_p,
            cconv_p,
            gla_s.reshape(1, dec_b, HEADS, DK, DV),
            sconv_s.reshape(1, dec_b, SCONV_K - 1, CONV_W),
            cconv_s)
```

```python
import jax
import jax.numpy as jnp
from jax import lax
from jax.experimental import pallas as pl
from jax.experimental.pallas import tpu as pltpu

F32 = jnp.float32
BF16 = jnp.bfloat16

D_MODEL = 1024
HEADS = 4
DK = 128
DV = 256
QK_WIDTH = HEADS * DK
V_WIDTH = HEADS * DV
GATE_RANK = 16
GATE_RANK_PAD = 128
GATE_TEMP_INV = 1.0 / 16.0
CHUNK = 64
CHUNK_SHIFT = 6
SCONV_K = 3
CCONV_K = 31
CONV_W = 1024
RMS_EPS = 1e-6
LN_EPS = 1e-5
Q_SCALE = DK ** -0.5

COL_Q = 0
COL_K = COL_Q + QK_WIDTH
COL_V = COL_K + QK_WIDTH
COL_G = COL_V + V_WIDTH
COL_HB = COL_G + V_WIDTH
COL_GATE_B = COL_HB + CONV_W
COL_GATE_C = COL_GATE_B + CONV_W
COL_ZB = COL_GATE_C + CONV_W
MAIN_W = COL_ZB + CONV_W

SUBLANES = 8
LANES = 128
MXU_K = 256
LANE_TILES = CONV_W // LANES
PROMPT_TILE = 512
CCONV_HALO = 32
SCONV_HALO = 8
CCONV_TIME_BLOCK = 16
DECODE_STATE_BLOCK = 8
DECODE_ODD_BLOCK = 32
VMEM_LIMIT = 56 * 1024 * 1024


def _dot(a, b):
    return jnp.dot(a, b, preferred_element_type=F32)


def _dot_nt(a, b):
    return lax.dot_general(a, b, (((1,), (1,)), ((), ())), preferred_element_type=F32)


def _dot_tn(a, b):
    return lax.dot_general(a, b, (((0,), (0,)), ((), ())), preferred_element_type=F32)


def _rmsnorm(x, g):
    ms = jnp.mean(x * x, axis=-1, keepdims=True)
    return x * lax.rsqrt(ms + RMS_EPS) * g


def _gate(x, y):
    return x / (1.0 + jnp.exp(-y))


def _silu(x):
    return _gate(x, x)


def _log_sigmoid(x):
    return -(jnp.maximum(-x, 0.0) + jnp.log1p(jnp.exp(-jnp.abs(x))))


def _log_decay(h, wal_ref, wup_ref, bup_ref):
    a_low = _dot(h, wal_ref[...]).astype(BF16)
    logit = _dot(a_low, wup_ref[...]) + bup_ref[...]
    return _log_sigmoid(logit) * GATE_TEMP_INV


def _head_rmsnorm(o, gng):
    ms = jnp.mean(o * o, axis=-1, keepdims=True)
    return o * lax.rsqrt(ms + RMS_EPS) * gng


def _layernorm_act(yc, z, lng, lnb):
    mu = jnp.mean(yc, axis=-1, keepdims=True)
    xc = yc - mu
    var = jnp.mean(xc * xc, axis=-1, keepdims=True)
    yn = xc * lax.rsqrt(var + LN_EPS) * lng + lnb
    return (_silu(yn) * _silu(z)).astype(BF16)


def _short_conv_gate(u, prev1, prev2, gate_b, z_b, wsc_ref):
    y = wsc_ref[2:3, :] * u + wsc_ref[1:2, :] * prev1 + wsc_ref[0:1, :] * prev2
    return gate_b * y * _silu(z_b)


def _even_prompt_kernel(x_ref, ng_ref, wmain_ref, wal_ref, wup_ref, bup_ref, gng_ref, wsc_ref, wout_ref,
                        x1_ref, sgla_ref, sconv_ref,
                        st_ref, ubuf_ref, qe_ref, ke_ref, kd_ref, v_ref, dec_ref, mix_ref):
    tm = PROMPT_TILE
    t = pl.program_id(1)
    last_t = pl.num_programs(1) - 1

    @pl.when(t == 0)
    def _():
        st_ref[...] = jnp.zeros_like(st_ref)
        ubuf_ref[0:SCONV_HALO, :] = jnp.zeros((SCONV_HALO, CONV_W), F32)

    x = x_ref[...]
    h = _rmsnorm(x, ng_ref[...]).astype(BF16)
    q = _dot(h, wmain_ref[:, COL_Q:COL_K]) * Q_SCALE
    k = _dot(h, wmain_ref[:, COL_K:COL_V])
    v_ref[...] = _dot(h, wmain_ref[:, COL_V:COL_G]).astype(BF16)
    log_a = _log_decay(h, wal_ref, wup_ref, bup_ref)

    row = lax.broadcasted_iota(jnp.int32, (MXU_K, MXU_K), 0)
    col = lax.broadcasted_iota(jnp.int32, (MXU_K, MXU_K), 1)
    in_chunk_causal = ((row >> CHUNK_SHIFT) == (col >> CHUNK_SHIFT)) & (col <= row)
    tri = jnp.where(in_chunk_causal, 1.0, 0.0).astype(BF16)
    la_hi = log_a.astype(BF16)
    la_lo = (log_a - la_hi.astype(F32)).astype(BF16)
    for sb in range(tm // MXU_K):
        rows = slice(sb * MXU_K, (sb + 1) * MXU_K)
        b_cum = _dot(tri, la_hi[rows, :]) + _dot(tri, la_lo[rows, :])
        b_tot = jnp.concatenate(
            [jnp.broadcast_to(b_cum[(c + 1) * CHUNK - 1:(c + 1) * CHUNK, :], (CHUNK, QK_WIDTH))
             for c in range(MXU_K // CHUNK)], axis=0)
        qe_ref[rows, :] = (q[rows, :] * jnp.exp(b_cum)).astype(BF16)
        ke_ref[rows, :] = (k[rows, :] * jnp.exp(-b_cum)).astype(BF16)
        kd_ref[rows, :] = (k[rows, :] * jnp.exp(b_tot - b_cum)).astype(BF16)
        dec_ref[rows, :] = jnp.exp(b_tot)

    gng = gng_ref[...]
    for hh in range(HEADS):
        kcols = slice(hh * DK, (hh + 1) * DK)
        vcols = slice(hh * DV, (hh + 1) * DV)
        st = st_ref[hh]
        for sb in range(tm // MXU_K):
            rows = slice(sb * MXU_K, (sb + 1) * MXU_K)
            sc = jnp.where(in_chunk_causal, _dot_nt(qe_ref[rows, kcols], ke_ref[rows, kcols]), 0.0)
            o_intra = _dot(sc.astype(BF16), v_ref[rows, vcols])
            for c in range(MXU_K // CHUNK):
                r0 = sb * MXU_K + c * CHUNK
                crow = slice(r0, r0 + CHUNK)
                o = o_intra[c * CHUNK:(c + 1) * CHUNK, :] + _dot_nt(qe_ref[crow, kcols], st.astype(BF16))
                mix_ref[crow, vcols] = _head_rmsnorm(o, gng)
                dec = dec_ref[r0:r0 + 1, kcols]
                st = st * dec + _dot_tn(v_ref[crow, vcols], kd_ref[crow, kcols])
        st_ref[hh] = st

    o_mix = (mix_ref[...] * _silu(_dot(h, wmain_ref[:, COL_G:COL_HB]))).astype(BF16)

    u = _dot(h, wmain_ref[:, COL_GATE_C:COL_ZB]) * _dot(h, wmain_ref[:, COL_HB:COL_GATE_B])
    ubuf_ref[SCONV_HALO:SCONV_HALO + tm, :] = u
    y = _short_conv_gate(u, ubuf_ref[pl.ds(SCONV_HALO - 1, tm), :], ubuf_ref[pl.ds(SCONV_HALO - 2, tm), :],
                         _dot(h, wmain_ref[:, COL_GATE_B:COL_GATE_C]), _dot(h, wmain_ref[:, COL_ZB:MAIN_W]),
                         wsc_ref).astype(BF16)
    ubuf_ref[0:SCONV_HALO, :] = ubuf_ref[tm:tm + SCONV_HALO, :]

    out = _dot(o_mix, wout_ref[0:V_WIDTH, :]) + _dot(y, wout_ref[V_WIDTH:V_WIDTH + CONV_W, :])
    x1_ref[...] = x + out

    @pl.when(t == last_t)
    def _():
        for hh in range(HEADS):
            sgla_ref[hh] = st_ref[hh].T
        sconv_ref[...] = ubuf_ref[pl.ds(SCONV_HALO + tm - (SCONV_K - 1), SCONV_K - 1), :]


def _odd_prompt_kernel(x_ref, ng_ref, win_ref, bin_ref, wdw_ref, bdw_ref, lng_ref, lnb_ref, wout_ref,
                       bout_ref, fng_ref,
                       y_ref, cconv_ref,
                       u3_ref, y3_ref, yc_ref):
    tm = PROMPT_TILE
    t = pl.program_id(1)
    last_t = pl.num_programs(1) - 1

    @pl.when(t == 0)
    def _():
        u3_ref[0:CCONV_HALO * LANE_TILES, :] = jnp.zeros((CCONV_HALO * LANE_TILES, LANES), F32)

    x = x_ref[...]
    h = _rmsnorm(x, ng_ref[...]).astype(BF16)
    a = _dot(h, win_ref[:, 0:CONV_W]) + bin_ref[:, 0:CONV_W]
    a_gate = _dot(h, win_ref[:, CONV_W:2 * CONV_W]) + bin_ref[:, CONV_W:2 * CONV_W]
    u = _gate(a, a_gate)

    @pl.when(t == last_t)
    def _():
        cconv_ref[...] = u[tm - (CCONV_K - 1):, :]

    for r8 in range(tm // SUBLANES):
        for c in range(LANE_TILES):
            dst = pl.ds((CCONV_HALO + r8 * SUBLANES) * LANE_TILES + c, SUBLANES, stride=LANE_TILES)
            u3_ref[dst, :] = u[r8 * SUBLANES:(r8 + 1) * SUBLANES, c * LANES:(c + 1) * LANES]

    base = CCONV_HALO - (CCONV_K - 1)
    tb = CCONV_TIME_BLOCK
    bdw = bdw_ref[...]
    for blk in range(tm // tb):
        acc = jnp.broadcast_to(bdw[None], (tb, LANE_TILES, LANES))
        for j in range(CCONV_K):
            rows = pl.ds((blk * tb + base + j) * LANE_TILES, tb * LANE_TILES)
            w_j = wdw_ref[j * LANE_TILES:(j + 1) * LANE_TILES, :]
            acc = acc + w_j[None] * u3_ref[rows, :].reshape(tb, LANE_TILES, LANES)
        y3_ref[blk * tb * LANE_TILES:(blk + 1) * tb * LANE_TILES, :] = acc.reshape(tb * LANE_TILES, LANES)

    u3_ref[0:CCONV_HALO * LANE_TILES, :] = u3_ref[tm * LANE_TILES:(tm + CCONV_HALO) * LANE_TILES, :]

    for r8 in range(tm // SUBLANES):
        for c in range(LANE_TILES):
            src = pl.ds(r8 * SUBLANES * LANE_TILES + c, SUBLANES, stride=LANE_TILES)
            yc_ref[r8 * SUBLANES:(r8 + 1) * SUBLANES, c * LANES:(c + 1) * LANES] = y3_ref[src, :]

    z = _dot(h, win_ref[:, 2 * CONV_W:3 * CONV_W]) + bin_ref[:, 2 * CONV_W:3 * CONV_W]
    act = _layernorm_act(yc_ref[...], z, lng_ref[...], lnb_ref[...])
    y_ref[...] = _rmsnorm(x + _dot(act, wout_ref[...]) + bout_ref[...], fng_ref[...])


def _even_decode_front_kernel(x_ref, ng_ref, wmain_ref, wal_ref, wup_ref, bup_ref, wsc_ref, sbuf_ref,
                              q_ref, k_ref, a_ref, v_ref, sg_ref, y_ref, snew_ref):
    h = _rmsnorm(x_ref[...], ng_ref[...]).astype(BF16)
    q_ref[...] = _dot(h, wmain_ref[:, COL_Q:COL_K]) * Q_SCALE
    k_ref[...] = _dot(h, wmain_ref[:, COL_K:COL_V])
    v_ref[...] = _dot(h, wmain_ref[:, COL_V:COL_G])
    sg_ref[...] = _silu(_dot(h, wmain_ref[:, COL_G:COL_HB]))
    a_ref[...] = jnp.exp(_log_decay(h, wal_ref, wup_ref, bup_ref))
    u = _dot(h, wmain_ref[:, COL_GATE_C:COL_ZB]) * _dot(h, wmain_ref[:, COL_HB:COL_GATE_B])
    prev2 = sbuf_ref[:, 0:CONV_W]
    prev1 = sbuf_ref[:, CONV_W:2 * CONV_W]
    y_ref[...] = _short_conv_gate(u, prev1, prev2, _dot(h, wmain_ref[:, COL_GATE_B:COL_GATE_C]),
                                  _dot(h, wmain_ref[:, COL_ZB:MAIN_W]), wsc_ref)
    snew_ref[:, 0:CONV_W] = prev1
    snew_ref[:, CONV_W:2 * CONV_W] = u


def _lane_bcast_column(row):
    return jnp.broadcast_to(row, (DK, DK)).T


def _gla_decode_kernel(q_ref, k_ref, a_ref, v_ref, s_ref, snew_ref, o_ref):
    for b in range(DECODE_STATE_BLOCK):
        for hh in range(HEADS):
            kcols = slice(hh * DK, (hh + 1) * DK)
            vcols = slice(hh * DV, (hh + 1) * DV)
            a_col = _lane_bcast_column(a_ref[b:b + 1, kcols])
            k_col = _lane_bcast_column(k_ref[b:b + 1, kcols])
            q_col = _lane_bcast_column(q_ref[b:b + 1, kcols])
            v_row = v_ref[b:b + 1, vcols]
            halves = []
            for half in range(DV // DK):
                lanes = slice(half * DK, (half + 1) * DK)
                s_new = a_col * s_ref[b, hh, :, lanes] + k_col * v_row[:, lanes]
                snew_ref[b, hh, :, lanes] = s_new
                halves.append(jnp.sum(q_col * s_new, axis=0, keepdims=True))
            o_ref[b:b + 1, vcols] = jnp.concatenate(halves, axis=1)


def _even_decode_out_kernel(x_ref, o_ref, sg_ref, y_ref, gng_ref, wout_ref, x1_ref):
    gng = gng_ref[...]
    parts = [_head_rmsnorm(o_ref[:, hh * DV:(hh + 1) * DV], gng) for hh in range(HEADS)]
    o_mix = (jnp.concatenate(parts, axis=1) * sg_ref[...]).astype(BF16)
    out = _dot(o_mix, wout_ref[0:V_WIDTH, :]) + _dot(y_ref[...].astype(BF16), wout_ref[V_WIDTH:V_WIDTH + CONV_W, :])
    x1_ref[...] = x_ref[...] + out


def _odd_decode_kernel(x_ref, ng_ref, win_ref, bin_ref, wdw_ref, bdw_ref, lng_ref, lnb_ref, wout_ref,
                       bout_ref, fng_ref, cbuf_ref,
                       y_ref, cnew_ref,
                       u_ref, hsum_ref):
    n_hist = CCONV_K - 1
    x = x_ref[...]
    h = _rmsnorm(x, ng_ref[...]).astype(BF16)
    a = _dot(h, win_ref[:, 0:CONV_W]) + bin_ref[:, 0:CONV_W]
    a_gate = _dot(h, win_ref[:, CONV_W:2 * CONV_W]) + bin_ref[:, CONV_W:2 * CONV_W]
    u_ref[...] = _gate(a, a_gate)
    w_hist = wdw_ref[0:n_hist, :]

    def per_sequence(b, carry):
        hist = cbuf_ref[b]
        hsum_ref[pl.ds(b, 1), :] = jnp.sum(hist * w_hist, axis=0, keepdims=True)
        cnew_ref[b, 0:n_hist - 1, :] = hist[1:, :]
        cnew_ref[b, n_hist - 1:n_hist, :] = u_ref[pl.ds(b, 1), :]
        return carry

    lax.fori_loop(0, DECODE_ODD_BLOCK, per_sequence, 0)
    yc = bdw_ref[...] + wdw_ref[n_hist:CCONV_K, :] * u_ref[...] + hsum_ref[...]
    z = _dot(h, win_ref[:, 2 * CONV_W:3 * CONV_W]) + bin_ref[:, 2 * CONV_W:3 * CONV_W]
    act = _layernorm_act(yc, z, lng_ref[...], lnb_ref[...])
    y_ref[...] = _rmsnorm(x + _dot(act, wout_ref[...]) + bout_ref[...], fng_ref[...])


def _const_spec(shape):
    nd = len(shape)
    return pl.BlockSpec(shape, lambda *_: (0,) * nd, pipeline_mode=pl.Buffered(1))


def _row(v):
    return v.reshape(1, -1)


def kernel(x_prompt, x_sample, state_gla, state_sconv, state_cconv, norm_g, w_in_a, w_gate_up, b_gate_up, gla_norm_g, w_sconv, w_out_a, w_in_c, b_in_c, w_dwconv, b_dwconv, ln_g, ln_b, w_out_c, b_out_c, final_norm_g):
    bsz, seq, d = x_prompt.shape
    dec_b = x_sample.shape[0]
    assert d == D_MODEL and seq % PROMPT_TILE == 0 and x_sample.shape[1] == 1
    assert w_in_a.shape[0] == 1 and w_in_c.shape[0] == 1 and norm_g.shape[0] == 2
    n_t = seq // PROMPT_TILE

    w_in = w_in_a[0]
    gate_lo = COL_HB
    gate_hi = COL_HB + GATE_RANK
    wmain = jnp.concatenate([w_in[:, :gate_lo], w_in[:, gate_hi:]], axis=1).astype(BF16)
    wal = jnp.pad(w_in[:, gate_lo:gate_hi], ((0, 0), (0, GATE_RANK_PAD - GATE_RANK))).astype(BF16)
    wup = jnp.pad(w_gate_up[0], ((0, GATE_RANK_PAD - GATE_RANK), (0, 0))).astype(BF16)
    bup = _row(b_gate_up[0])
    gng = _row(gla_norm_g[0])
    wsc = w_sconv[0]
    wout_a = w_out_a[0].astype(BF16)
    ng0 = _row(norm_g[0])
    ng1 = _row(norm_g[1])
    win_c = w_in_c[0].astype(BF16)
    bin_c = _row(b_in_c[0])
    wdw = w_dwconv[0]
    bdw = _row(b_dwconv[0])
    lng = _row(ln_g[0])
    lnb = _row(ln_b[0])
    wout_c = w_out_c[0].astype(BF16)
    bout = _row(b_out_c[0])
    fng = _row(final_norm_g)

    even_weights = (ng0, wmain, wal, wup, bup)
    even_prompt_weights = even_weights + (gng, wsc, wout_a)
    odd_weights = (ng1, win_c, bin_c, wdw, bdw, lng, lnb, wout_c, bout, fng)
    odd_weight_specs = [_const_spec(w.shape) for w in odd_weights]
    wdw3 = wdw.reshape(CCONV_K * LANE_TILES, LANES)
    bdw3 = b_dwconv[0].reshape(LANE_TILES, LANES)
    odd_prompt_weights = (ng1, win_c, bin_c, wdw3, bdw3, lng, lnb, wout_c, bout, fng)

    tm = PROMPT_TILE
    tile_spec = pl.BlockSpec((None, tm, D_MODEL), lambda b, t: (b, t, 0))
    prompt_params = pltpu.CompilerParams(dimension_semantics=("arbitrary", "arbitrary"),
                                         vmem_limit_bytes=VMEM_LIMIT)

    x1_p, gla_p, sconv_p = pl.pallas_call(
        _even_prompt_kernel,
        grid=(bsz, n_t),
        in_specs=[tile_spec] + [_const_spec(w.shape) for w in even_prompt_weights],
        out_specs=[tile_spec,
                   pl.BlockSpec((None, None, HEADS, DK, DV), lambda b, t: (0, b, 0, 0, 0)),
                   pl.BlockSpec((None, None, SCONV_K - 1, CONV_W), lambda b, t: (0, b, 0, 0))],
        out_shape=[jax.ShapeDtypeStruct((bsz, seq, D_MODEL), F32),
                   jax.ShapeDtypeStruct((1, bsz, HEADS, DK, DV), F32),
                   jax.ShapeDtypeStruct((1, bsz, SCONV_K - 1, CONV_W), F32)],
        scratch_shapes=[pltpu.VMEM((HEADS, DV, DK), F32),
                        pltpu.VMEM((tm + SCONV_HALO, CONV_W), F32),
                        pltpu.VMEM((tm, QK_WIDTH), BF16),
                        pltpu.VMEM((tm, QK_WIDTH), BF16),
                        pltpu.VMEM((tm, QK_WIDTH), BF16),
                        pltpu.VMEM((tm, V_WIDTH), BF16),
                        pltpu.VMEM((tm, QK_WIDTH), F32),
                        pltpu.VMEM((tm, V_WIDTH), F32)],
        compiler_params=prompt_params,
        name="even_prompt",
    )(x_prompt, *even_prompt_weights)

    y_p, cconv_p = pl.pallas_call(
        _odd_prompt_kernel,
        grid=(bsz, n_t),
        in_specs=[tile_spec] + [_const_spec(w.shape) for w in odd_prompt_weights],
        out_specs=[tile_spec,
                   pl.BlockSpec((None, None, CCONV_K - 1, CONV_W), lambda b, t: (0, b, 0, 0))],
        out_shape=[jax.ShapeDtypeStruct((bsz, seq, D_MODEL), F32),
                   jax.ShapeDtypeStruct((1, bsz, CCONV_K - 1, CONV_W), F32)],
        scratch_shapes=[pltpu.VMEM(((tm + CCONV_HALO) * LANE_TILES, LANES), F32),
                        pltpu.VMEM((tm * LANE_TILES, LANES), F32),
                        pltpu.VMEM((tm, CONV_W), F32)],
        compiler_params=prompt_params,
        name="odd_prompt",
    )(x1_p, *odd_prompt_weights)

    xs = x_sample.reshape(dec_b, D_MODEL)
    sbuf = state_sconv.reshape(dec_b, (SCONV_K - 1) * CONV_W)
    single = pltpu.CompilerParams(vmem_limit_bytes=VMEM_LIMIT)
    q_s, k_s, a_s, v_s, sg_s, ysc_s, sconv_s = pl.pallas_call(
        _even_decode_front_kernel,
        out_shape=[jax.ShapeDtypeStruct((dec_b, QK_WIDTH), F32),
                   jax.ShapeDtypeStruct((dec_b, QK_WIDTH), F32),
                   jax.ShapeDtypeStruct((dec_b, QK_WIDTH), F32),
                   jax.ShapeDtypeStruct((dec_b, V_WIDTH), F32),
                   jax.ShapeDtypeStruct((dec_b, V_WIDTH), F32),
                   jax.ShapeDtypeStruct((dec_b, CONV_W), F32),
                   jax.ShapeDtypeStruct((dec_b, (SCONV_K - 1) * CONV_W), F32)],
        compiler_params=single,
        name="even_decode_front",
    )(xs, *even_weights, wsc, sbuf)

    sb = DECODE_STATE_BLOCK
    assert dec_b % sb == 0
    vec_spec = lambda w: pl.BlockSpec((sb, w), lambda i: (i, 0))
    state_spec = pl.BlockSpec((sb, HEADS, DK, DV), lambda i: (i, 0, 0, 0))
    gla_s, o_s = pl.pallas_call(
        _gla_decode_kernel,
        grid=(dec_b // sb,),
        in_specs=[vec_spec(QK_WIDTH), vec_spec(QK_WIDTH), vec_spec(QK_WIDTH), vec_spec(V_WIDTH), state_spec],
        out_specs=[state_spec, vec_spec(V_WIDTH)],
        out_shape=[jax.ShapeDtypeStruct((dec_b, HEADS, DK, DV), F32),
                   jax.ShapeDtypeStruct((dec_b, V_WIDTH), F32)],
        compiler_params=pltpu.CompilerParams(dimension_semantics=("arbitrary",),
                                             vmem_limit_bytes=VMEM_LIMIT),
        name="gla_decode",
    )(q_s, k_s, a_s, v_s, state_gla[0])

    x1_s = pl.pallas_call(
        _even_decode_out_kernel,
        out_shape=jax.ShapeDtypeStruct((dec_b, D_MODEL), F32),
        compiler_params=single,
        name="even_decode_out",
    )(xs, o_s, sg_s, ysc_s, gng, wout_a)

    ob = DECODE_ODD_BLOCK
    assert dec_b % ob == 0
    rows_spec = pl.BlockSpec((ob, D_MODEL), lambda i: (i, 0))
    hist_spec = pl.BlockSpec((None, ob, CCONV_K - 1, CONV_W), lambda i: (0, i, 0, 0))
    y_s, cconv_s = pl.pallas_call(
        _odd_decode_kernel,
        grid=(dec_b // ob,),
        in_specs=[rows_spec] + odd_weight_specs + [hist_spec],
        out_specs=[rows_spec, hist_spec],
        out_shape=[jax.ShapeDtypeStruct((dec_b, D_MODEL), F32),
                   jax.ShapeDtypeStruct((1, dec_b, CCONV_K - 1, CONV_W), F32)],
        scratch_shapes=[pltpu.VMEM((ob, CONV_W), F32),
                        pltpu.VMEM((ob, CONV_W), F32)],
        compiler_params=pltpu.CompilerParams(dimension_semantics=("arbitrary",),
                                             vmem_limit_bytes=VMEM_LIMIT),
        name="odd_decode",
    )(x1_s, *odd_weights, state_cconv)

    return (y_p,
            y_s.reshape(dec_b, 1, D_MODEL),
            gla_p,
            sconv_p,
            cconv_p,
            gla_s.reshape(1, dec_b, HEADS, DK, DV),
            sconv_s.reshape(1, dec_b, SCONV_K - 1, CONV_W),
            cconv_s)
```

```python
import jax
import jax.numpy as jnp
from jax import lax
from jax.experimental import pallas as pl
from jax.experimental.pallas import tpu as pltpu

F32 = jnp.float32
BF16 = jnp.bfloat16

D_MODEL = 1024
HEADS = 4
DK = 128
DV = 256
QK_WIDTH = HEADS * DK
V_WIDTH = HEADS * DV
GATE_RANK = 16
GATE_RANK_PAD = 128
GATE_TEMP_INV = 1.0 / 16.0
CHUNK = 64
CHUNK_SHIFT = 6
SCONV_K = 3
CCONV_K = 31
CONV_W = 1024
RMS_EPS = 1e-6
LN_EPS = 1e-5
Q_SCALE = DK ** -0.5

COL_Q = 0
COL_K = COL_Q + QK_WIDTH
COL_V = COL_K + QK_WIDTH
COL_G = COL_V + V_WIDTH
COL_A_LOW = COL_G + V_WIDTH
COL_HB = COL_A_LOW + GATE_RANK
COL_GATE_B = COL_HB + CONV_W
COL_GATE_C = COL_GATE_B + CONV_W
COL_ZB = COL_GATE_C + CONV_W
MAIN_W = COL_ZB + CONV_W

SUBLANES = 8
LANES = 128
MXU_K = 256
LANE_TILES = CONV_W // LANES
PROMPT_TILE = 512
CCONV_HALO = 32
SCONV_HALO = 8
CCONV_TIME_BLOCK = 16
DECODE_STATE_BLOCK = 8
DECODE_ODD_BLOCK = 32
VMEM_LIMIT = 56 * 1024 * 1024


def _dot(a, b):
    return jnp.dot(a, b, preferred_element_type=F32)


def _dot_nt(a, b):
    return lax.dot_general(a, b, (((1,), (1,)), ((), ())), preferred_element_type=F32)


def _dot_tn(a, b):
    return lax.dot_general(a, b, (((0,), (0,)), ((), ())), preferred_element_type=F32)


def _proj(h, wt_ref, lo, hi):
    return _dot_nt(h, wt_ref[lo:hi, :])


def _rmsnorm(x, g):
    ms = jnp.mean(x * x, axis=-1, keepdims=True)
    return x * lax.rsqrt(ms + RMS_EPS) * g


def _gate(x, y):
    return x / (1.0 + jnp.exp(-y))


def _silu(x):
    return _gate(x, x)


def _log_sigmoid(x):
    return -(jnp.maximum(-x, 0.0) + jnp.log1p(jnp.exp(-jnp.abs(x))))


def _log_decay(h, wmain_ref, wup_ref, bup_ref):
    a_low = _proj(h, wmain_ref, COL_A_LOW, COL_A_LOW + GATE_RANK_PAD).astype(BF16)
    logit = _dot(a_low, wup_ref[...]) + bup_ref[...]
    return _log_sigmoid(logit) * GATE_TEMP_INV


def _head_rmsnorm(o, gng):
    ms = jnp.mean(o * o, axis=-1, keepdims=True)
    return o * lax.rsqrt(ms + RMS_EPS) * gng


def _layernorm_act(yc, z, lng, lnb):
    mu = jnp.mean(yc, axis=-1, keepdims=True)
    xc = yc - mu
    var = jnp.mean(xc * xc, axis=-1, keepdims=True)
    yn = xc * lax.rsqrt(var + LN_EPS) * lng + lnb
    return (_silu(yn) * _silu(z)).astype(BF16)


def _short_conv_gate(u, prev1, prev2, gate_b, z_b, wsc_ref):
    y = wsc_ref[2:3, :] * u + wsc_ref[1:2, :] * prev1 + wsc_ref[0:1, :] * prev2
    return gate_b * y * _silu(z_b)


def _even_prompt_kernel(x_ref, ng_ref, wmain_ref, wup_ref, bup_ref, gng_ref, wsc_ref, wout_ref,
                        x1_ref, sgla_ref, sconv_ref,
                        st_ref, ubuf_ref, qe_ref, ke_ref, kd_ref, v_ref, dec_ref, mix_ref):
    tm = PROMPT_TILE
    t = pl.program_id(1)
    last_t = pl.num_programs(1) - 1

    @pl.when(t == 0)
    def _():
        st_ref[...] = jnp.zeros_like(st_ref)
        ubuf_ref[0:SCONV_HALO, :] = jnp.zeros((SCONV_HALO, CONV_W), F32)

    x = x_ref[...]
    h = _rmsnorm(x, ng_ref[...]).astype(BF16)
    q = _proj(h, wmain_ref, COL_Q, COL_K) * Q_SCALE
    k = _proj(h, wmain_ref, COL_K, COL_V)
    v_ref[...] = _proj(h, wmain_ref, COL_V, COL_G).astype(BF16)
    log_a = _log_decay(h, wmain_ref, wup_ref, bup_ref)

    row = lax.broadcasted_iota(jnp.int32, (MXU_K, MXU_K), 0)
    col = lax.broadcasted_iota(jnp.int32, (MXU_K, MXU_K), 1)
    in_chunk_causal = ((row >> CHUNK_SHIFT) == (col >> CHUNK_SHIFT)) & (col <= row)
    tri = jnp.where(in_chunk_causal, 1.0, 0.0).astype(BF16)
    la_hi = log_a.astype(BF16)
    la_lo = (log_a - la_hi.astype(F32)).astype(BF16)
    for sb in range(tm // MXU_K):
        rows = slice(sb * MXU_K, (sb + 1) * MXU_K)
        b_cum = _dot(tri, la_hi[rows, :]) + _dot(tri, la_lo[rows, :])
        b_tot = jnp.concatenate(
            [jnp.broadcast_to(b_cum[(c + 1) * CHUNK - 1:(c + 1) * CHUNK, :], (CHUNK, QK_WIDTH))
             for c in range(MXU_K // CHUNK)], axis=0)
        qe_ref[rows, :] = (q[rows, :] * jnp.exp(b_cum)).astype(BF16)
        ke_ref[rows, :] = (k[rows, :] * jnp.exp(-b_cum)).astype(BF16)
        kd_ref[rows, :] = (k[rows, :] * jnp.exp(b_tot - b_cum)).astype(BF16)
        dec_ref[rows, :] = jnp.exp(b_tot)

    gng = gng_ref[...]
    for hh in range(HEADS):
        kcols = slice(hh * DK, (hh + 1) * DK)
        vcols = slice(hh * DV, (hh + 1) * DV)
        st = st_ref[hh]
        for sb in range(tm // MXU_K):
            rows = slice(sb * MXU_K, (sb + 1) * MXU_K)
            sc = jnp.where(in_chunk_causal, _dot_nt(qe_ref[rows, kcols], ke_ref[rows, kcols]), 0.0)
            o_intra = _dot(sc.astype(BF16), v_ref[rows, vcols])
            for c in range(MXU_K // CHUNK):
                r0 = sb * MXU_K + c * CHUNK
                crow = slice(r0, r0 + CHUNK)
                o = o_intra[c * CHUNK:(c + 1) * CHUNK, :] + _dot_nt(qe_ref[crow, kcols], st.astype(BF16))
                mix_ref[crow, vcols] = _head_rmsnorm(o, gng)
                dec = dec_ref[r0:r0 + 1, kcols]
                st = st * dec + _dot_tn(v_ref[crow, vcols], kd_ref[crow, kcols])
        st_ref[hh] = st

    o_mix = (mix_ref[...] * _silu(_proj(h, wmain_ref, COL_G, COL_A_LOW))).astype(BF16)

    u = _proj(h, wmain_ref, COL_GATE_C, COL_ZB) * _proj(h, wmain_ref, COL_HB, COL_GATE_B)
    ubuf_ref[SCONV_HALO:SCONV_HALO + tm, :] = u
    y = _short_conv_gate(u, ubuf_ref[pl.ds(SCONV_HALO - 1, tm), :], ubuf_ref[pl.ds(SCONV_HALO - 2, tm), :],
                         _proj(h, wmain_ref, COL_GATE_B, COL_GATE_C), _proj(h, wmain_ref, COL_ZB, MAIN_W),
                         wsc_ref).astype(BF16)
    ubuf_ref[0:SCONV_HALO, :] = ubuf_ref[tm:tm + SCONV_HALO, :]

    out = _dot(o_mix, wout_ref[0:V_WIDTH, :]) + _dot(y, wout_ref[V_WIDTH:V_WIDTH + CONV_W, :])
    x1_ref[...] = x + out

    @pl.when(t == last_t)
    def _():
        for hh in range(HEADS):
            sgla_ref[hh] = st_ref[hh].T
        sconv_ref[...] = ubuf_ref[pl.ds(SCONV_HALO + tm - (SCONV_K - 1), SCONV_K - 1), :]


def _odd_prompt_kernel(x_ref, ng_ref, win_ref, bin_ref, wdw_ref, bdw_ref, lng_ref, lnb_ref, wout_ref,
                       bout_ref, fng_ref,
                       y_ref, cconv_ref,
                       u3_ref, y3_ref, yc_ref):
    tm = PROMPT_TILE
    t = pl.program_id(1)
    last_t = pl.num_programs(1) - 1

    @pl.when(t == 0)
    def _():
        u3_ref[0:CCONV_HALO * LANE_TILES, :] = jnp.zeros((CCONV_HALO * LANE_TILES, LANES), F32)

    x = x_ref[...]
    h = _rmsnorm(x, ng_ref[...]).astype(BF16)
    a = _dot(h, win_ref[:, 0:CONV_W]) + bin_ref[:, 0:CONV_W]
    a_gate = _dot(h, win_ref[:, CONV_W:2 * CONV_W]) + bin_ref[:, CONV_W:2 * CONV_W]
    u = _gate(a, a_gate)

    @pl.when(t == last_t)
    def _():
        cconv_ref[...] = u[tm - (CCONV_K - 1):, :]

    for r8 in range(tm // SUBLANES):
        for c in range(LANE_TILES):
            dst = pl.ds((CCONV_HALO + r8 * SUBLANES) * LANE_TILES + c, SUBLANES, stride=LANE_TILES)
            u3_ref[dst, :] = u[r8 * SUBLANES:(r8 + 1) * SUBLANES, c * LANES:(c + 1) * LANES]

    base = CCONV_HALO - (CCONV_K - 1)
    tb = CCONV_TIME_BLOCK
    bdw = bdw_ref[...]
    for blk in range(tm // tb):
        acc = jnp.broadcast_to(bdw[None], (tb, LANE_TILES, LANES))
        for j in range(CCONV_K):
            rows = pl.ds((blk * tb + base + j) * LANE_TILES, tb * LANE_TILES)
            w_j = wdw_ref[j * LANE_TILES:(j + 1) * LANE_TILES, :]
            acc = acc + w_j[None] * u3_ref[rows, :].reshape(tb, LANE_TILES, LANES)
        y3_ref[blk * tb * LANE_TILES:(blk + 1) * tb * LANE_TILES, :] = acc.reshape(tb * LANE_TILES, LANES)

    u3_ref[0:CCONV_HALO * LANE_TILES, :] = u3_ref[tm * LANE_TILES:(tm + CCONV_HALO) * LANE_TILES, :]

    for r8 in range(tm // SUBLANES):
        for c in range(LANE_TILES):
            src = pl.ds(r8 * SUBLANES * LANE_TILES + c, SUBLANES, stride=LANE_TILES)
            yc_ref[r8 * SUBLANES:(r8 + 1) * SUBLANES, c * LANES:(c + 1) * LANES] = y3_ref[src, :]

    z = _dot(h, win_ref[:, 2 * CONV_W:3 * CONV_W]) + bin_ref[:, 2 * CONV_W:3 * CONV_W]
    act = _layernorm_act(yc_ref[...], z, lng_ref[...], lnb_ref[...])
    y_ref[...] = _rmsnorm(x + _dot(act, wout_ref[...]) + bout_ref[...], fng_ref[...])


def _even_decode_front_kernel(x_ref, ng_ref, wmain_ref, wup_ref, bup_ref, wsc_ref, sbuf_ref,
                              q_ref, k_ref, a_ref, v_ref, sg_ref, y_ref, snew_ref):
    h = _rmsnorm(x_ref[...], ng_ref[...]).astype(BF16)
    q_ref[...] = _proj(h, wmain_ref, COL_Q, COL_K) * Q_SCALE
    k_ref[...] = _proj(h, wmain_ref, COL_K, COL_V)
    v_ref[...] = _proj(h, wmain_ref, COL_V, COL_G)
    sg_ref[...] = _silu(_proj(h, wmain_ref, COL_G, COL_A_LOW))
    a_ref[...] = jnp.exp(_log_decay(h, wmain_ref, wup_ref, bup_ref))
    u = _proj(h, wmain_ref, COL_GATE_C, COL_ZB) * _proj(h, wmain_ref, COL_HB, COL_GATE_B)
    prev2 = sbuf_ref[:, 0:CONV_W]
    prev1 = sbuf_ref[:, CONV_W:2 * CONV_W]
    y_ref[...] = _short_conv_gate(u, prev1, prev2, _proj(h, wmain_ref, COL_GATE_B, COL_GATE_C),
                                  _proj(h, wmain_ref, COL_ZB, MAIN_W), wsc_ref)
    snew_ref[:, 0:CONV_W] = prev1
    snew_ref[:, CONV_W:2 * CONV_W] = u


def _lane_bcast_column(row):
    return jnp.broadcast_to(row, (DK, DK)).T


def _gla_decode_kernel(q_ref, k_ref, a_ref, v_ref, s_ref, snew_ref, o_ref):
    for b in range(DECODE_STATE_BLOCK):
        for hh in range(HEADS):
            kcols = slice(hh * DK, (hh + 1) * DK)
            vcols = slice(hh * DV, (hh + 1) * DV)
            a_col = _lane_bcast_column(a_ref[b:b + 1, kcols])
            k_col = _lane_bcast_column(k_ref[b:b + 1, kcols])
            q_col = _lane_bcast_column(q_ref[b:b + 1, kcols])
            v_row = v_ref[b:b + 1, vcols]
            halves = []
            for half in range(DV // DK):
                lanes = slice(half * DK, (half + 1) * DK)
                s_new = a_col * s_ref[b, hh, :, lanes] + k_col * v_row[:, lanes]
                snew_ref[b, hh, :, lanes] = s_new
                halves.append(jnp.sum(q_col * s_new, axis=0, keepdims=True))
            o_ref[b:b + 1, vcols] = jnp.concatenate(halves, axis=1)


def _even_decode_out_kernel(x_ref, o_ref, sg_ref, y_ref, gng_ref, wout_ref, x1_ref):
    gng = gng_ref[...]
    parts = [_head_rmsnorm(o_ref[:, hh * DV:(hh + 1) * DV], gng) for hh in range(HEADS)]
    o_mix = (jnp.concatenate(parts, axis=1) * sg_ref[...]).astype(BF16)
    out = _dot(o_mix, wout_ref[0:V_WIDTH, :]) + _dot(y_ref[...].astype(BF16), wout_ref[V_WIDTH:V_WIDTH + CONV_W, :])
    x1_ref[...] = x_ref[...] + out


def _odd_decode_kernel(x_ref, ng_ref, win_ref, bin_ref, wdw_ref, bdw_ref, lng_ref, lnb_ref, wout_ref,
                       bout_ref, fng_ref, cbuf_ref,
                       y_ref, cnew_ref):
    n_hist = CCONV_K - 1
    x = x_ref[...]
    h = _rmsnorm(x, ng_ref[...]).astype(BF16)
    a = _dot(h, win_ref[:, 0:CONV_W]) + bin_ref[:, 0:CONV_W]
    a_gate = _dot(h, win_ref[:, CONV_W:2 * CONV_W]) + bin_ref[:, CONV_W:2 * CONV_W]
    u = _gate(a, a_gate)
    yc = bdw_ref[...] + wdw_ref[n_hist:CCONV_K, :] * u
    for j in range(n_hist):
        tap = cbuf_ref[j]
        yc = yc + wdw_ref[j:j + 1, :] * tap
        if j >= 1:
            cnew_ref[j - 1] = tap
    cnew_ref[n_hist - 1] = u
    z = _dot(h, win_ref[:, 2 * CONV_W:3 * CONV_W]) + bin_ref[:, 2 * CONV_W:3 * CONV_W]
    act = _layernorm_act(yc, z, lng_ref[...], lnb_ref[...])
    y_ref[...] = _rmsnorm(x + _dot(act, wout_ref[...]) + bout_ref[...], fng_ref[...])


def _const_spec(shape):
    nd = len(shape)
    return pl.BlockSpec(shape, lambda *_: (0,) * nd, pipeline_mode=pl.Buffered(1))


def _row(v):
    return v.reshape(1, -1)


def kernel(x_prompt, x_sample, state_gla, state_sconv, state_cconv, norm_g, w_in_a, w_gate_up, b_gate_up, gla_norm_g, w_sconv, w_out_a, w_in_c, b_in_c, w_dwconv, b_dwconv, ln_g, ln_b, w_out_c, b_out_c, final_norm_g):
    bsz, seq, d = x_prompt.shape
    dec_b = x_sample.shape[0]
    assert d == D_MODEL and seq % PROMPT_TILE == 0 and x_sample.shape[1] == 1
    assert w_in_a.shape[0] == 1 and w_in_c.shape[0] == 1 and norm_g.shape[0] == 2
    n_t = seq // PROMPT_TILE

    assert w_in_a.shape[2] == MAIN_W
    wmain = w_in_a[0].T.astype(BF16)
    wup = jnp.pad(w_gate_up[0], ((0, GATE_RANK_PAD - GATE_RANK), (0, 0))).astype(BF16)
    bup = _row(b_gate_up[0])
    gng = _row(gla_norm_g[0])
    wsc = w_sconv[0]
    wout_a = w_out_a[0].astype(BF16)
    ng0 = _row(norm_g[0])
    ng1 = _row(norm_g[1])
    win_c = w_in_c[0].astype(BF16)
    bin_c = _row(b_in_c[0])
    wdw = w_dwconv[0]
    bdw = _row(b_dwconv[0])
    lng = _row(ln_g[0])
    lnb = _row(ln_b[0])
    wout_c = w_out_c[0].astype(BF16)
    bout = _row(b_out_c[0])
    fng = _row(final_norm_g)

    even_weights = (ng0, wmain, wup, bup)
    even_prompt_weights = even_weights + (gng, wsc, wout_a)
    odd_weights = (ng1, win_c, bin_c, wdw, bdw, lng, lnb, wout_c, bout, fng)
    odd_weight_specs = [_const_spec(w.shape) for w in odd_weights]
    wdw3 = wdw.reshape(CCONV_K * LANE_TILES, LANES)
    bdw3 = b_dwconv[0].reshape(LANE_TILES, LANES)
    odd_prompt_weights = (ng1, win_c, bin_c, wdw3, bdw3, lng, lnb, wout_c, bout, fng)

    tm = PROMPT_TILE
    tile_spec = pl.BlockSpec((None, tm, D_MODEL), lambda b, t: (b, t, 0))
    prompt_params = pltpu.CompilerParams(dimension_semantics=("arbitrary", "arbitrary"),
                                         vmem_limit_bytes=VMEM_LIMIT)

    x1_p, gla_p, sconv_p = pl.pallas_call(
        _even_prompt_kernel,
        grid=(bsz, n_t),
        in_specs=[tile_spec] + [_const_spec(w.shape) for w in even_prompt_weights],
        out_specs=[tile_spec,
                   pl.BlockSpec((None, None, HEADS, DK, DV), lambda b, t: (0, b, 0, 0, 0)),
                   pl.BlockSpec((None, None, SCONV_K - 1, CONV_W), lambda b, t: (0, b, 0, 0))],
        out_shape=[jax.ShapeDtypeStruct((bsz, seq, D_MODEL), F32),
                   jax.ShapeDtypeStruct((1, bsz, HEADS, DK, DV), F32),
                   jax.ShapeDtypeStruct((1, bsz, SCONV_K - 1, CONV_W), F32)],
        scratch_shapes=[pltpu.VMEM((HEADS, DV, DK), F32),
                        pltpu.VMEM((tm + SCONV_HALO, CONV_W), F32),
                        pltpu.VMEM((tm, QK_WIDTH), BF16),
                        pltpu.VMEM((tm, QK_WIDTH), BF16),
                        pltpu.VMEM((tm, QK_WIDTH), BF16),
                        pltpu.VMEM((tm, V_WIDTH), BF16),
                        pltpu.VMEM((tm, QK_WIDTH), F32),
                        pltpu.VMEM((tm, V_WIDTH), F32)],
        compiler_params=prompt_params,
        name="even_prompt",
    )(x_prompt, *even_prompt_weights)

    y_p, cconv_p = pl.pallas_call(
        _odd_prompt_kernel,
        grid=(bsz, n_t),
        in_specs=[tile_spec] + [_const_spec(w.shape) for w in odd_prompt_weights],
        out_specs=[tile_spec,
                   pl.BlockSpec((None, None, CCONV_K - 1, CONV_W), lambda b, t: (0, b, 0, 0))],
        out_shape=[jax.ShapeDtypeStruct((bsz, seq, D_MODEL), F32),
                   jax.ShapeDtypeStruct((1, bsz, CCONV_K - 1, CONV_W), F32)],
        scratch_shapes=[pltpu.VMEM(((tm + CCONV_HALO) * LANE_TILES, LANES), F32),
                        pltpu.VMEM((tm * LANE_TILES, LANES), F32),
                        pltpu.VMEM((tm, CONV_W), F32)],
        compiler_params=prompt_params,
        name="odd_prompt",
    )(x1_p, *odd_prompt_weights)

    xs = x_sample.reshape(dec_b, D_MODEL)
    sbuf = state_sconv.reshape(dec_b, (SCONV_K - 1) * CONV_W)
    single = pltpu.CompilerParams(vmem_limit_bytes=VMEM_LIMIT)
    q_s, k_s, a_s, v_s, sg_s, ysc_s, sconv_s = pl.pallas_call(
        _even_decode_front_kernel,
        out_shape=[jax.ShapeDtypeStruct((dec_b, QK_WIDTH), F32),
                   jax.ShapeDtypeStruct((dec_b, QK_WIDTH), F32),
                   jax.ShapeDtypeStruct((dec_b, QK_WIDTH), F32),
                   jax.ShapeDtypeStruct((dec_b, V_WIDTH), F32),
                   jax.ShapeDtypeStruct((dec_b, V_WIDTH), F32),
                   jax.ShapeDtypeStruct((dec_b, CONV_W), F32),
                   jax.ShapeDtypeStruct((dec_b, (SCONV_K - 1) * CONV_W), F32)],
        compiler_params=single,
        name="even_decode_front",
    )(xs, *even_weights, wsc, sbuf)

    sb = DECODE_STATE_BLOCK
    assert dec_b % sb == 0
    vec_spec = lambda w: pl.BlockSpec((sb, w), lambda i: (i, 0))
    state_spec = pl.BlockSpec((sb, HEADS, DK, DV), lambda i: (i, 0, 0, 0))
    gla_s, o_s = pl.pallas_call(
        _gla_decode_kernel,
        grid=(dec_b // sb,),
        in_specs=[vec_spec(QK_WIDTH), vec_spec(QK_WIDTH), vec_spec(QK_WIDTH), vec_spec(V_WIDTH), state_spec],
        out_specs=[state_spec, vec_spec(V_WIDTH)],
        out_shape=[jax.ShapeDtypeStruct((dec_b, HEADS, DK, DV), F32),
                   jax.ShapeDtypeStruct((dec_b, V_WIDTH), F32)],
        compiler_params=pltpu.CompilerParams(dimension_semantics=("arbitrary",),
                                             vmem_limit_bytes=VMEM_LIMIT),
        name="gla_decode",
    )(q_s, k_s, a_s, v_s, state_gla[0])

    x1_s = pl.pallas_call(
        _even_decode_out_kernel,
        out_shape=jax.ShapeDtypeStruct((dec_b, D_MODEL), F32),
        compiler_params=single,
        name="even_decode_out",
    )(xs, o_s, sg_s, ysc_s, gng, wout_a)

    ob = DECODE_ODD_BLOCK
    assert dec_b % ob == 0
    rows_spec = pl.BlockSpec((ob, D_MODEL), lambda i: (i, 0))
    hist_spec = pl.BlockSpec((CCONV_K - 1, ob, CONV_W), lambda i: (0, i, 0))
    cbuf = jnp.transpose(state_cconv[0], (1, 0, 2))
    y_s, cconv_t = pl.pallas_call(
        _odd_decode_kernel,
        grid=(dec_b // ob,),
        in_specs=[rows_spec] + odd_weight_specs + [hist_spec],
        out_specs=[rows_spec, hist_spec],
        out_shape=[jax.ShapeDtypeStruct((dec_b, D_MODEL), F32),
                   jax.ShapeDtypeStruct((CCONV_K - 1, dec_b, CONV_W), F32)],
        compiler_params=pltpu.CompilerParams(dimension_semantics=("arbitrary",),
                                             vmem_limit_bytes=VMEM_LIMIT),
        name="odd_decode",
    )(x1_s, *odd_weights, cbuf)
    cconv_s = jnp.transpose(cconv_t, (1, 0, 2))[None]

    return (y_p,
            y_s.reshape(dec_b, 1, D_MODEL),
            gla_p,
            sconv_p,
            cconv_p,
            gla_s.reshape(1, dec_b, HEADS, DK, DV),
            sconv_s.reshape(1, dec_b, SCONV_K - 1, CONV_W),
            cconv_s)
```

```python
import jax
import jax.numpy as jnp
from jax import lax
from jax.experimental import pallas as pl
from jax.experimental.pallas import tpu as pltpu

F32 = jnp.float32
BF16 = jnp.bfloat16

D_MODEL = 1024
HEADS = 4
DK = 128
DV = 256
QK_WIDTH = HEADS * DK
V_WIDTH = HEADS * DV
GATE_RANK = 16
GATE_RANK_PAD = 128
GATE_TEMP_INV = 1.0 / 16.0
CHUNK = 64
CHUNK_SHIFT = 6
SCONV_K = 3
CCONV_K = 31
CONV_W = 1024
RMS_EPS = 1e-6
LN_EPS = 1e-5
Q_SCALE = DK ** -0.5

COL_Q = 0
COL_K = COL_Q + QK_WIDTH
COL_V = COL_K + QK_WIDTH
COL_G = COL_V + V_WIDTH
COL_A_LOW = COL_G + V_WIDTH
COL_HB = COL_A_LOW + GATE_RANK
COL_GATE_B = COL_HB + CONV_W
COL_GATE_C = COL_GATE_B + CONV_W
COL_ZB = COL_GATE_C + CONV_W
MAIN_W = COL_ZB + CONV_W

SUBLANES = 8
LANES = 128
MXU_K = 256
MXU_N = 256
LANE_TILES = CONV_W // LANES
PROMPT_TILE = 512
CCONV_HALO = 32
SCONV_HALO = 8
CCONV_TIME_BLOCK = 16
DECODE_STATE_BLOCK = 8
DECODE_ODD_BLOCK = 32
VMEM_LIMIT = 56 * 1024 * 1024


def _dot(a, b):
    return jnp.dot(a, b, preferred_element_type=F32)


def _dot_nt(a, b):
    return lax.dot_general(a, b, (((1,), (1,)), ((), ())), preferred_element_type=F32)


def _dot_tn(a, b):
    return lax.dot_general(a, b, (((0,), (0,)), ((), ())), preferred_element_type=F32)


def _proj(h, wt_ref, lo, hi):
    return _dot_nt(h, wt_ref[lo:hi, :])


def _rmsnorm(x, g):
    ms = jnp.mean(x * x, axis=-1, keepdims=True)
    return x * lax.rsqrt(ms + RMS_EPS) * g


def _gate(x, y):
    return x / (1.0 + jnp.exp(-y))


def _silu(x):
    return _gate(x, x)


def _log_sigmoid(x):
    return -(jnp.maximum(-x, 0.0) + jnp.log(1.0 + jnp.exp(-jnp.abs(x))))


def _log_decay(h, wmain_ref, wup_ref, bup_ref):
    a_low = _proj(h, wmain_ref, COL_A_LOW, COL_A_LOW + GATE_RANK_PAD).astype(BF16)
    logit = _dot(a_low, wup_ref[...]) + bup_ref[...]
    return _log_sigmoid(logit) * GATE_TEMP_INV


def _head_rmsnorm(o, gng):
    ms = jnp.mean(o * o, axis=-1, keepdims=True)
    return o * lax.rsqrt(ms + RMS_EPS) * gng


def _layernorm_act(yc, z, lng, lnb):
    mu = jnp.mean(yc, axis=-1, keepdims=True)
    xc = yc - mu
    var = jnp.mean(xc * xc, axis=-1, keepdims=True)
    yn = xc * lax.rsqrt(var + LN_EPS) * lng + lnb
    return (_silu(yn) * _silu(z)).astype(BF16)


def _short_conv_gate(u, prev1, prev2, gate_b, z_b, wsc_ref):
    y = wsc_ref[2:3, :] * u + wsc_ref[1:2, :] * prev1 + wsc_ref[0:1, :] * prev2
    return gate_b * y * _silu(z_b)


def _even_prompt_kernel(x_ref, ng_ref, wmain_ref, wup_ref, bup_ref, gng_ref, wsc_ref, wout_ref,
                        x1_ref, sgla_ref, sconv_ref,
                        st_ref, ubuf_ref, qe_ref, ke_ref, kd_ref, v_ref, dec_ref, mix_ref):
    tm = PROMPT_TILE
    t = pl.program_id(1)
    last_t = pl.num_programs(1) - 1

    @pl.when(t == 0)
    def _():
        st_ref[...] = jnp.zeros_like(st_ref)
        ubuf_ref[0:SCONV_HALO, :] = jnp.zeros((SCONV_HALO, CONV_W), F32)

    x = x_ref[...]
    h = _rmsnorm(x, ng_ref[...]).astype(BF16)
    q = _proj(h, wmain_ref, COL_Q, COL_K) * Q_SCALE
    k = _proj(h, wmain_ref, COL_K, COL_V)
    v_ref[...] = _proj(h, wmain_ref, COL_V, COL_G).astype(BF16)
    log_a = _log_decay(h, wmain_ref, wup_ref, bup_ref)

    row = lax.broadcasted_iota(jnp.int32, (MXU_K, MXU_K), 0)
    col = lax.broadcasted_iota(jnp.int32, (MXU_K, MXU_K), 1)
    in_chunk_causal = ((row >> CHUNK_SHIFT) == (col >> CHUNK_SHIFT)) & (col <= row)
    tri = jnp.where(in_chunk_causal, 1.0, 0.0).astype(BF16)
    la_hi = log_a.astype(BF16)
    la_lo = (log_a - la_hi.astype(F32)).astype(BF16)
    for sb in range(tm // MXU_K):
        rows = slice(sb * MXU_K, (sb + 1) * MXU_K)
        b_cum = _dot(tri, la_hi[rows, :]) + _dot(tri, la_lo[rows, :])
        b_tot = jnp.concatenate(
            [jnp.broadcast_to(b_cum[(c + 1) * CHUNK - 1:(c + 1) * CHUNK, :], (CHUNK, QK_WIDTH))
             for c in range(MXU_K // CHUNK)], axis=0)
        qe_ref[rows, :] = (q[rows, :] * jnp.exp(b_cum)).astype(BF16)
        ke_ref[rows, :] = (k[rows, :] * jnp.exp(-b_cum)).astype(BF16)
        kd_ref[rows, :] = (k[rows, :] * jnp.exp(b_tot - b_cum)).astype(BF16)
        dec_ref[rows, :] = jnp.exp(b_tot)

    gng = gng_ref[...]
    for hh in range(HEADS):
        kcols = slice(hh * DK, (hh + 1) * DK)
        vcols = slice(hh * DV, (hh + 1) * DV)
        st = st_ref[hh]
        for sb in range(tm // MXU_K):
            rows = slice(sb * MXU_K, (sb + 1) * MXU_K)
            sc = jnp.where(in_chunk_causal, _dot_nt(qe_ref[rows, kcols], ke_ref[rows, kcols]), 0.0)
            o_intra = _dot(sc.astype(BF16), v_ref[rows, vcols])
            for c in range(MXU_K // CHUNK):
                r0 = sb * MXU_K + c * CHUNK
                crow = slice(r0, r0 + CHUNK)
                o = o_intra[c * CHUNK:(c + 1) * CHUNK, :] + _dot_nt(qe_ref[crow, kcols], st.astype(BF16))
                mix_ref[crow, vcols] = _head_rmsnorm(o, gng)
                dec = dec_ref[r0:r0 + 1, kcols]
                st = st * dec + _dot_tn(v_ref[crow, vcols], kd_ref[crow, kcols])
        st_ref[hh] = st

    o_mix = (mix_ref[...] * _silu(_proj(h, wmain_ref, COL_G, COL_A_LOW))).astype(BF16)

    u = _proj(h, wmain_ref, COL_GATE_C, COL_ZB) * _proj(h, wmain_ref, COL_HB, COL_GATE_B)
    ubuf_ref[SCONV_HALO:SCONV_HALO + tm, :] = u
    y = _short_conv_gate(u, ubuf_ref[pl.ds(SCONV_HALO - 1, tm), :], ubuf_ref[pl.ds(SCONV_HALO - 2, tm), :],
                         _proj(h, wmain_ref, COL_GATE_B, COL_GATE_C), _proj(h, wmain_ref, COL_ZB, MAIN_W),
                         wsc_ref).astype(BF16)
    ubuf_ref[0:SCONV_HALO, :] = ubuf_ref[tm:tm + SCONV_HALO, :]

    out = _dot(o_mix, wout_ref[0:V_WIDTH, :]) + _dot(y, wout_ref[V_WIDTH:V_WIDTH + CONV_W, :])
    x1_ref[...] = x + out

    @pl.when(t == last_t)
    def _():
        for hh in range(HEADS):
            sgla_ref[hh] = st_ref[hh].T
        sconv_ref[...] = ubuf_ref[pl.ds(SCONV_HALO + tm - (SCONV_K - 1), SCONV_K - 1), :]


def _odd_prompt_kernel(x_ref, ng_ref, win_ref, bin_ref, wdw_ref, bdw_ref, lng_ref, lnb_ref, wout_ref,
                       bout_ref, fng_ref,
                       y_ref, cconv_ref,
                       u3_ref, y3_ref, yc_ref, tail_ref):
    tm = PROMPT_TILE
    t = pl.program_id(1)
    last_t = pl.num_programs(1) - 1

    @pl.when(t == 0)
    def _():
        u3_ref[0:CCONV_HALO * LANE_TILES, :] = jnp.zeros((CCONV_HALO * LANE_TILES, LANES), F32)

    x = x_ref[...]
    h = _rmsnorm(x, ng_ref[...]).astype(BF16)
    for g0 in range(0, CONV_W, MXU_N):
        cols = slice(g0, g0 + MXU_N)
        gcols = slice(CONV_W + g0, CONV_W + g0 + MXU_N)
        u = _gate(_dot(h, win_ref[:, cols]) + bin_ref[:, cols], _dot(h, win_ref[:, gcols]) + bin_ref[:, gcols])
        tail_ref[:, cols] = u[tm - CCONV_HALO:, :]
        for r8 in range(tm // SUBLANES):
            for c in range(MXU_N // LANES):
                dst = pl.ds((CCONV_HALO + r8 * SUBLANES) * LANE_TILES + g0 // LANES + c, SUBLANES,
                            stride=LANE_TILES)
                u3_ref[dst, :] = u[r8 * SUBLANES:(r8 + 1) * SUBLANES, c * LANES:(c + 1) * LANES]

    @pl.when(t == last_t)
    def _():
        cconv_ref[...] = tail_ref[CCONV_HALO - (CCONV_K - 1):, :]

    base = CCONV_HALO - (CCONV_K - 1)
    tb = CCONV_TIME_BLOCK
    bdw = bdw_ref[...]
    for blk in range(tm // tb):
        acc = jnp.broadcast_to(bdw[None], (tb, LANE_TILES, LANES))
        for j in range(CCONV_K):
            rows = pl.ds((blk * tb + base + j) * LANE_TILES, tb * LANE_TILES)
            w_j = wdw_ref[j * LANE_TILES:(j + 1) * LANE_TILES, :]
            acc = acc + w_j[None] * u3_ref[rows, :].reshape(tb, LANE_TILES, LANES)
        y3_ref[blk * tb * LANE_TILES:(blk + 1) * tb * LANE_TILES, :] = acc.reshape(tb * LANE_TILES, LANES)

    u3_ref[0:CCONV_HALO * LANE_TILES, :] = u3_ref[tm * LANE_TILES:(tm + CCONV_HALO) * LANE_TILES, :]

    for r8 in range(tm // SUBLANES):
        for c in range(LANE_TILES):
            src = pl.ds(r8 * SUBLANES * LANE_TILES + c, SUBLANES, stride=LANE_TILES)
            yc_ref[r8 * SUBLANES:(r8 + 1) * SUBLANES, c * LANES:(c + 1) * LANES] = y3_ref[src, :]

    z = _dot(h, win_ref[:, 2 * CONV_W:3 * CONV_W]) + bin_ref[:, 2 * CONV_W:3 * CONV_W]
    act = _layernorm_act(yc_ref[...], z, lng_ref[...], lnb_ref[...])
    y_ref[...] = _rmsnorm(x + _dot(act, wout_ref[...]) + bout_ref[...], fng_ref[...])


def _even_decode_front_kernel(x_ref, ng_ref, wmain_ref, wup_ref, bup_ref, wsc_ref, sbuf_ref,
                              q_ref, k_ref, a_ref, v_ref, sg_ref, y_ref, snew_ref):
    h = _rmsnorm(x_ref[...], ng_ref[...]).astype(BF16)
    q_ref[...] = _proj(h, wmain_ref, COL_Q, COL_K) * Q_SCALE
    k_ref[...] = _proj(h, wmain_ref, COL_K, COL_V)
    v_ref[...] = _proj(h, wmain_ref, COL_V, COL_G)
    sg_ref[...] = _silu(_proj(h, wmain_ref, COL_G, COL_A_LOW))
    a_ref[...] = jnp.exp(_log_decay(h, wmain_ref, wup_ref, bup_ref))
    u = _proj(h, wmain_ref, COL_GATE_C, COL_ZB) * _proj(h, wmain_ref, COL_HB, COL_GATE_B)
    prev2 = sbuf_ref[:, 0:CONV_W]
    prev1 = sbuf_ref[:, CONV_W:2 * CONV_W]
    y_ref[...] = _short_conv_gate(u, prev1, prev2, _proj(h, wmain_ref, COL_GATE_B, COL_GATE_C),
                                  _proj(h, wmain_ref, COL_ZB, MAIN_W), wsc_ref)
    snew_ref[:, 0:CONV_W] = prev1
    snew_ref[:, CONV_W:2 * CONV_W] = u


def _lane_bcast_column(row):
    return jnp.broadcast_to(row, (DK, DK)).T


def _gla_decode_kernel(q_ref, k_ref, a_ref, v_ref, s_ref, snew_ref, o_ref):
    for b in range(DECODE_STATE_BLOCK):
        for hh in range(HEADS):
            kcols = slice(hh * DK, (hh + 1) * DK)
            vcols = slice(hh * DV, (hh + 1) * DV)
            a_col = _lane_bcast_column(a_ref[b:b + 1, kcols])
            k_col = _lane_bcast_column(k_ref[b:b + 1, kcols])
            q_col = _lane_bcast_column(q_ref[b:b + 1, kcols])
            v_row = v_ref[b:b + 1, vcols]
            halves = []
            for half in range(DV // DK):
                lanes = slice(half * DK, (half + 1) * DK)
                s_new = a_col * s_ref[b, hh, :, lanes] + k_col * v_row[:, lanes]
                snew_ref[b, hh, :, lanes] = s_new
                halves.append(jnp.sum(q_col * s_new, axis=0, keepdims=True))
            o_ref[b:b + 1, vcols] = jnp.concatenate(halves, axis=1)


def _even_decode_out_kernel(x_ref, o_ref, sg_ref, y_ref, gng_ref, wout_ref, x1_ref):
    gng = gng_ref[...]
    parts = [_head_rmsnorm(o_ref[:, hh * DV:(hh + 1) * DV], gng) for hh in range(HEADS)]
    o_mix = (jnp.concatenate(parts, axis=1) * sg_ref[...]).astype(BF16)
    out = _dot(o_mix, wout_ref[0:V_WIDTH, :]) + _dot(y_ref[...].astype(BF16), wout_ref[V_WIDTH:V_WIDTH + CONV_W, :])
    x1_ref[...] = x_ref[...] + out


def _odd_decode_kernel(x_ref, ng_ref, win_ref, bin_ref, wdw_ref, bdw_ref, lng_ref, lnb_ref, wout_ref,
                       bout_ref, fng_ref, cbuf_ref,
                       y_ref, cnew_ref):
    n_hist = CCONV_K - 1
    x = x_ref[...]
    h = _rmsnorm(x, ng_ref[...]).astype(BF16)
    a = _dot(h, win_ref[:, 0:CONV_W]) + bin_ref[:, 0:CONV_W]
    a_gate = _dot(h, win_ref[:, CONV_W:2 * CONV_W]) + bin_ref[:, CONV_W:2 * CONV_W]
    u = _gate(a, a_gate)
    yc = bdw_ref[...] + wdw_ref[n_hist:CCONV_K, :] * u
    for j in range(n_hist):
        tap = cbuf_ref[j]
        yc = yc + wdw_ref[j:j + 1, :] * tap
        if j >= 1:
            cnew_ref[j - 1] = tap
    cnew_ref[n_hist - 1] = u
    z = _dot(h, win_ref[:, 2 * CONV_W:3 * CONV_W]) + bin_ref[:, 2 * CONV_W:3 * CONV_W]
    act = _layernorm_act(yc, z, lng_ref[...], lnb_ref[...])
    y_ref[...] = _rmsnorm(x + _dot(act, wout_ref[...]) + bout_ref[...], fng_ref[...])


def _const_spec(shape):
    nd = len(shape)
    return pl.BlockSpec(shape, lambda *_: (0,) * nd, pipeline_mode=pl.Buffered(1))


def _row(v):
    return v.reshape(1, -1)


def kernel(x_prompt, x_sample, state_gla, state_sconv, state_cconv, norm_g, w_in_a, w_gate_up, b_gate_up, gla_norm_g, w_sconv, w_out_a, w_in_c, b_in_c, w_dwconv, b_dwconv, ln_g, ln_b, w_out_c, b_out_c, final_norm_g):
    bsz, seq, d = x_prompt.shape
    dec_b = x_sample.shape[0]
    assert d == D_MODEL and seq % PROMPT_TILE == 0 and x_sample.shape[1] == 1
    assert w_in_a.shape[0] == 1 and w_in_c.shape[0] == 1 and norm_g.shape[0] == 2
    n_t = seq // PROMPT_TILE

    assert w_in_a.shape[2] == MAIN_W
    wmain = w_in_a[0].T.astype(BF16)
    wup = jnp.pad(w_gate_up[0], ((0, GATE_RANK_PAD - GATE_RANK), (0, 0))).astype(BF16)
    bup = _row(b_gate_up[0])
    gng = _row(gla_norm_g[0])
    wsc = w_sconv[0]
    wout_a = w_out_a[0].astype(BF16)
    ng0 = _row(norm_g[0])
    ng1 = _row(norm_g[1])
    win_c = w_in_c[0].astype(BF16)
    bin_c = _row(b_in_c[0])
    wdw = w_dwconv[0]
    bdw = _row(b_dwconv[0])
    lng = _row(ln_g[0])
    lnb = _row(ln_b[0])
    wout_c = w_out_c[0].astype(BF16)
    bout = _row(b_out_c[0])
    fng = _row(final_norm_g)

    even_weights = (ng0, wmain, wup, bup)
    even_prompt_weights = even_weights + (gng, wsc, wout_a)
    odd_weights = (ng1, win_c, bin_c, wdw, bdw, lng, lnb, wout_c, bout, fng)
    odd_weight_specs = [_const_spec(w.shape) for w in odd_weights]
    wdw3 = wdw.reshape(CCONV_K * LANE_TILES, LANES)
    bdw3 = b_dwconv[0].reshape(LANE_TILES, LANES)
    odd_prompt_weights = (ng1, win_c, bin_c, wdw3, bdw3, lng, lnb, wout_c, bout, fng)

    tm = PROMPT_TILE
    tile_spec = pl.BlockSpec((None, tm, D_MODEL), lambda b, t: (b, t, 0))
    prompt_params = pltpu.CompilerParams(dimension_semantics=("arbitrary", "arbitrary"),
                                         vmem_limit_bytes=VMEM_LIMIT)

    x1_p, gla_p, sconv_p = pl.pallas_call(
        _even_prompt_kernel,
        grid=(bsz, n_t),
        in_specs=[tile_spec] + [_const_spec(w.shape) for w in even_prompt_weights],
        out_specs=[tile_spec,
                   pl.BlockSpec((None, None, HEADS, DK, DV), lambda b, t: (0, b, 0, 0, 0)),
                   pl.BlockSpec((None, None, SCONV_K - 1, CONV_W), lambda b, t: (0, b, 0, 0))],
        out_shape=[jax.ShapeDtypeStruct((bsz, seq, D_MODEL), F32),
                   jax.ShapeDtypeStruct((1, bsz, HEADS, DK, DV), F32),
                   jax.ShapeDtypeStruct((1, bsz, SCONV_K - 1, CONV_W), F32)],
        scratch_shapes=[pltpu.VMEM((HEADS, DV, DK), F32),
                        pltpu.VMEM((tm + SCONV_HALO, CONV_W), F32),
                        pltpu.VMEM((tm, QK_WIDTH), BF16),
                        pltpu.VMEM((tm, QK_WIDTH), BF16),
                        pltpu.VMEM((tm, QK_WIDTH), BF16),
                        pltpu.VMEM((tm, V_WIDTH), BF16),
                        pltpu.VMEM((tm, QK_WIDTH), F32),
                        pltpu.VMEM((tm, V_WIDTH), F32)],
        compiler_params=prompt_params,
        name="even_prompt",
    )(x_prompt, *even_prompt_weights)

    y_p, cconv_p = pl.pallas_call(
        _odd_prompt_kernel,
        grid=(bsz, n_t),
        in_specs=[tile_spec] + [_const_spec(w.shape) for w in odd_prompt_weights],
        out_specs=[tile_spec,
                   pl.BlockSpec((None, None, CCONV_K - 1, CONV_W), lambda b, t: (0, b, 0, 0))],
        out_shape=[jax.ShapeDtypeStruct((bsz, seq, D_MODEL), F32),
                   jax.ShapeDtypeStruct((1, bsz, CCONV_K - 1, CONV_W), F32)],
        scratch_shapes=[pltpu.VMEM(((tm + CCONV_HALO) * LANE_TILES, LANES), F32),
                        pltpu.VMEM((tm * LANE_TILES, LANES), F32),
                        pltpu.VMEM((tm, CONV_W), F32),
                        pltpu.VMEM((CCONV_HALO, CONV_W), F32)],
        compiler_params=prompt_params,
        name="odd_prompt",
    )(x1_p, *odd_prompt_weights)

    xs = x_sample.reshape(dec_b, D_MODEL)
    sbuf = state_sconv.reshape(dec_b, (SCONV_K - 1) * CONV_W)
    single = pltpu.CompilerParams(vmem_limit_bytes=VMEM_LIMIT)
    q_s, k_s, a_s, v_s, sg_s, ysc_s, sconv_s = pl.pallas_call(
        _even_decode_front_kernel,
        out_shape=[jax.ShapeDtypeStruct((dec_b, QK_WIDTH), F32),
                   jax.ShapeDtypeStruct((dec_b, QK_WIDTH), F32),
                   jax.ShapeDtypeStruct((dec_b, QK_WIDTH), F32),
                   jax.ShapeDtypeStruct((dec_b, V_WIDTH), F32),
                   jax.ShapeDtypeStruct((dec_b, V_WIDTH), F32),
                   jax.ShapeDtypeStruct((dec_b, CONV_W), F32),
                   jax.ShapeDtypeStruct((dec_b, (SCONV_K - 1) * CONV_W), F32)],
        compiler_params=single,
        name="even_decode_front",
    )(xs, *even_weights, wsc, sbuf)

    sb = DECODE_STATE_BLOCK
    assert dec_b % sb == 0
    vec_spec = lambda w: pl.BlockSpec((sb, w), lambda i: (i, 0))
    state_spec = pl.BlockSpec((sb, HEADS, DK, DV), lambda i: (i, 0, 0, 0))
    gla_s, o_s = pl.pallas_call(
        _gla_decode_kernel,
        grid=(dec_b // sb,),
        in_specs=[vec_spec(QK_WIDTH), vec_spec(QK_WIDTH), vec_spec(QK_WIDTH), vec_spec(V_WIDTH), state_spec],
        out_specs=[state_spec, vec_spec(V_WIDTH)],
        out_shape=[jax.ShapeDtypeStruct((dec_b, HEADS, DK, DV), F32),
                   jax.ShapeDtypeStruct((dec_b, V_WIDTH), F32)],
        compiler_params=pltpu.CompilerParams(dimension_semantics=("arbitrary",),
                                             vmem_limit_bytes=VMEM_LIMIT),
        name="gla_decode",
    )(q_s, k_s, a_s, v_s, state_gla[0])

    x1_s = pl.pallas_call(
        _even_decode_out_kernel,
        out_shape=jax.ShapeDtypeStruct((dec_b, D_MODEL), F32),
        compiler_params=single,
        name="even_decode_out",
    )(xs, o_s, sg_s, ysc_s, gng, wout_a)

    ob = DECODE_ODD_BLOCK
    assert dec_b % ob == 0
    rows_spec = pl.BlockSpec((ob, D_MODEL), lambda i: (i, 0))
    hist_spec = pl.BlockSpec((CCONV_K - 1, ob, CONV_W), lambda i: (0, i, 0))
    cbuf = jnp.transpose(state_cconv[0], (1, 0, 2))
    y_s, cconv_t = pl.pallas_call(
        _odd_decode_kernel,
        grid=(dec_b // ob,),
        in_specs=[rows_spec] + odd_weight_specs + [hist_spec],
        out_specs=[rows_spec, hist_spec],
        out_shape=[jax.ShapeDtypeStruct((dec_b, D_MODEL), F32),
                   jax.ShapeDtypeStruct((CCONV_K - 1, dec_b, CONV_W), F32)],
        compiler_params=pltpu.CompilerParams(dimension_semantics=("arbitrary",),
                                             vmem_limit_bytes=VMEM_LIMIT),
        name="odd_decode",
    )(x1_s, *odd_weights, cbuf)
    cconv_s = jnp.transpose(cconv_t, (1, 0, 2))[None]

    return (y_p,
            y_s.reshape(dec_b, 1, D_MODEL),
            gla_p,
            sconv_p,
            cconv_p,
            gla_s.reshape(1, dec_b, HEADS, DK, DV),
            sconv_s.reshape(1, dec_b, SCONV_K - 1, CONV_W),
            cconv_s)
```

```python
import jax
import jax.numpy as jnp
from jax import lax
from jax.experimental import pallas as pl
from jax.experimental.pallas import tpu as pltpu

F32 = jnp.float32
BF16 = jnp.bfloat16

D_MODEL = 1024
HEADS = 4
DK = 128
DV = 256
QK_WIDTH = HEADS * DK
V_WIDTH = HEADS * DV
GATE_RANK = 16
GATE_RANK_PAD = 128
GATE_TEMP_INV = 1.0 / 16.0
CHUNK = 64
CHUNK_SHIFT = 6
SCONV_K = 3
CCONV_K = 31
CONV_W = 1024
RMS_EPS = 1e-6
LN_EPS = 1e-5
Q_SCALE = DK ** -0.5

COL_Q = 0
COL_K = COL_Q + QK_WIDTH
COL_V = COL_K + QK_WIDTH
COL_G = COL_V + V_WIDTH
COL_A_LOW = COL_G + V_WIDTH
COL_HB = COL_A_LOW + GATE_RANK
COL_GATE_B = COL_HB + CONV_W
COL_GATE_C = COL_GATE_B + CONV_W
COL_ZB = COL_GATE_C + CONV_W
MAIN_W = COL_ZB + CONV_W

SUBLANES = 8
LANES = 128
MXU_K = 256
MXU_N = 256
LANE_TILES = CONV_W // LANES
PROMPT_TILE = 512
CCONV_HALO = 32
SCONV_HALO = 8
CCONV_TIME_BLOCK = 16
DECODE_STATE_BLOCK = 16
DECODE_ODD_BLOCK = 64
VMEM_LIMIT = 56 * 1024 * 1024


def _dot(a, b):
    return jnp.dot(a, b, preferred_element_type=F32)


def _dot_nt(a, b):
    return lax.dot_general(a, b, (((1,), (1,)), ((), ())), preferred_element_type=F32)


def _dot_tn(a, b):
    return lax.dot_general(a, b, (((0,), (0,)), ((), ())), preferred_element_type=F32)


def _proj(h, wt_ref, lo, hi):
    return _dot_nt(h, wt_ref[lo:hi, :])


def _rmsnorm(x, g):
    ms = jnp.mean(x * x, axis=-1, keepdims=True)
    return x * lax.rsqrt(ms + RMS_EPS) * g


def _gate(x, y):
    return x / (1.0 + jnp.exp(-y))


def _silu(x):
    return _gate(x, x)


def _log_sigmoid(x):
    return -(jnp.maximum(-x, 0.0) + jnp.log(1.0 + jnp.exp(-jnp.abs(x))))


def _log_decay(h, wmain_ref, wup_ref, bup_ref):
    a_low = _proj(h, wmain_ref, COL_A_LOW, COL_A_LOW + GATE_RANK_PAD).astype(BF16)
    logit = _dot(a_low, wup_ref[...]) + bup_ref[...]
    return _log_sigmoid(logit) * GATE_TEMP_INV


def _head_rmsnorm(o, gng):
    ms = jnp.mean(o * o, axis=-1, keepdims=True)
    return o * lax.rsqrt(ms + RMS_EPS) * gng


def _layernorm_act(yc, z, lng, lnb):
    mu = jnp.mean(yc, axis=-1, keepdims=True)
    xc = yc - mu
    var = jnp.mean(xc * xc, axis=-1, keepdims=True)
    yn = xc * lax.rsqrt(var + LN_EPS) * lng + lnb
    return (_silu(yn) * _silu(z)).astype(BF16)


def _short_conv_gate(u, prev1, prev2, gate_b, z_b, wsc_ref):
    y = wsc_ref[2:3, :] * u + wsc_ref[1:2, :] * prev1 + wsc_ref[0:1, :] * prev2
    return gate_b * y * _silu(z_b)


def _even_prompt_kernel(x_ref, ng_ref, wmain_ref, wup_ref, bup_ref, gng_ref, wsc_ref, wout_ref,
                        x1_ref, sgla_ref, sconv_ref,
                        st_ref, ubuf_ref, qe_ref, ke_ref, kd_ref, v_ref, dec_ref, mix_ref):
    tm = PROMPT_TILE
    t = pl.program_id(1)
    last_t = pl.num_programs(1) - 1

    @pl.when(t == 0)
    def _():
        st_ref[...] = jnp.zeros_like(st_ref)
        ubuf_ref[0:SCONV_HALO, :] = jnp.zeros((SCONV_HALO, CONV_W), F32)

    x = x_ref[...]
    h = _rmsnorm(x, ng_ref[...]).astype(BF16)
    q = _proj(h, wmain_ref, COL_Q, COL_K) * Q_SCALE
    k = _proj(h, wmain_ref, COL_K, COL_V)
    v_ref[...] = _proj(h, wmain_ref, COL_V, COL_G).astype(BF16)
    log_a = _log_decay(h, wmain_ref, wup_ref, bup_ref)

    row = lax.broadcasted_iota(jnp.int32, (MXU_K, MXU_K), 0)
    col = lax.broadcasted_iota(jnp.int32, (MXU_K, MXU_K), 1)
    in_chunk_causal = ((row >> CHUNK_SHIFT) == (col >> CHUNK_SHIFT)) & (col <= row)
    tri = jnp.where(in_chunk_causal, 1.0, 0.0).astype(BF16)
    la_hi = log_a.astype(BF16)
    la_lo = (log_a - la_hi.astype(F32)).astype(BF16)
    for sb in range(tm // MXU_K):
        rows = slice(sb * MXU_K, (sb + 1) * MXU_K)
        b_cum = _dot(tri, la_hi[rows, :]) + _dot(tri, la_lo[rows, :])
        b_tot = jnp.concatenate(
            [jnp.broadcast_to(b_cum[(c + 1) * CHUNK - 1:(c + 1) * CHUNK, :], (CHUNK, QK_WIDTH))
             for c in range(MXU_K // CHUNK)], axis=0)
        qe_ref[rows, :] = (q[rows, :] * jnp.exp(b_cum)).astype(BF16)
        ke_ref[rows, :] = (k[rows, :] * jnp.exp(-b_cum)).astype(BF16)
        kd_ref[rows, :] = (k[rows, :] * jnp.exp(b_tot - b_cum)).astype(BF16)
        dec_ref[rows, :] = jnp.exp(b_tot)

    gng = gng_ref[...]
    for hh in range(HEADS):
        kcols = slice(hh * DK, (hh + 1) * DK)
        vcols = slice(hh * DV, (hh + 1) * DV)
        st = st_ref[hh]
        for sb in range(tm // MXU_K):
            rows = slice(sb * MXU_K, (sb + 1) * MXU_K)
            sc = jnp.where(in_chunk_causal, _dot_nt(qe_ref[rows, kcols], ke_ref[rows, kcols]), 0.0)
            o_intra = _dot(sc.astype(BF16), v_ref[rows, vcols])
            for c in range(MXU_K // CHUNK):
                r0 = sb * MXU_K + c * CHUNK
                crow = slice(r0, r0 + CHUNK)
                o = o_intra[c * CHUNK:(c + 1) * CHUNK, :] + _dot_nt(qe_ref[crow, kcols], st.astype(BF16))
                mix_ref[crow, vcols] = _head_rmsnorm(o, gng)
                dec = dec_ref[r0:r0 + 1, kcols]
                st = st * dec + _dot_tn(v_ref[crow, vcols], kd_ref[crow, kcols])
        st_ref[hh] = st

    o_mix = (mix_ref[...] * _silu(_proj(h, wmain_ref, COL_G, COL_A_LOW))).astype(BF16)

    u = _proj(h, wmain_ref, COL_GATE_C, COL_ZB) * _proj(h, wmain_ref, COL_HB, COL_GATE_B)
    ubuf_ref[SCONV_HALO:SCONV_HALO + tm, :] = u
    y = _short_conv_gate(u, ubuf_ref[pl.ds(SCONV_HALO - 1, tm), :], ubuf_ref[pl.ds(SCONV_HALO - 2, tm), :],
                         _proj(h, wmain_ref, COL_GATE_B, COL_GATE_C), _proj(h, wmain_ref, COL_ZB, MAIN_W),
                         wsc_ref).astype(BF16)
    ubuf_ref[0:SCONV_HALO, :] = ubuf_ref[tm:tm + SCONV_HALO, :]

    out = _dot(o_mix, wout_ref[0:V_WIDTH, :]) + _dot(y, wout_ref[V_WIDTH:V_WIDTH + CONV_W, :])
    x1_ref[...] = x + out

    @pl.when(t == last_t)
    def _():
        for hh in range(HEADS):
            sgla_ref[hh] = st_ref[hh].T
        sconv_ref[...] = ubuf_ref[pl.ds(SCONV_HALO + tm - (SCONV_K - 1), SCONV_K - 1), :]


def _odd_prompt_kernel(x_ref, ng_ref, win_ref, bin_ref, wdw_ref, bdw_ref, lng_ref, lnb_ref, wout_ref,
                       bout_ref, fng_ref,
                       y_ref, cconv_ref,
                       u3_ref, y3_ref, yc_ref, tail_ref):
    tm = PROMPT_TILE
    t = pl.program_id(1)
    last_t = pl.num_programs(1) - 1

    @pl.when(t == 0)
    def _():
        u3_ref[0:CCONV_HALO * LANE_TILES, :] = jnp.zeros((CCONV_HALO * LANE_TILES, LANES), F32)

    x = x_ref[...]
    h = _rmsnorm(x, ng_ref[...]).astype(BF16)
    for g0 in range(0, CONV_W, MXU_N):
        cols = slice(g0, g0 + MXU_N)
        gcols = slice(CONV_W + g0, CONV_W + g0 + MXU_N)
        u = _gate(_dot(h, win_ref[:, cols]) + bin_ref[:, cols], _dot(h, win_ref[:, gcols]) + bin_ref[:, gcols])
        tail_ref[:, cols] = u[tm - CCONV_HALO:, :]
        for r8 in range(tm // SUBLANES):
            for c in range(MXU_N // LANES):
                dst = pl.ds((CCONV_HALO + r8 * SUBLANES) * LANE_TILES + g0 // LANES + c, SUBLANES,
                            stride=LANE_TILES)
                u3_ref[dst, :] = u[r8 * SUBLANES:(r8 + 1) * SUBLANES, c * LANES:(c + 1) * LANES]

    @pl.when(t == last_t)
    def _():
        cconv_ref[...] = tail_ref[CCONV_HALO - (CCONV_K - 1):, :]

    base = CCONV_HALO - (CCONV_K - 1)
    tb = CCONV_TIME_BLOCK
    bdw = bdw_ref[...]
    for blk in range(tm // tb):
        acc = jnp.broadcast_to(bdw[None], (tb, LANE_TILES, LANES))
        for j in range(CCONV_K):
            rows = pl.ds((blk * tb + base + j) * LANE_TILES, tb * LANE_TILES)
            w_j = wdw_ref[j * LANE_TILES:(j + 1) * LANE_TILES, :]
            acc = acc + w_j[None] * u3_ref[rows, :].reshape(tb, LANE_TILES, LANES)
        y3_ref[blk * tb * LANE_TILES:(blk + 1) * tb * LANE_TILES, :] = acc.reshape(tb * LANE_TILES, LANES)

    u3_ref[0:CCONV_HALO * LANE_TILES, :] = u3_ref[tm * LANE_TILES:(tm + CCONV_HALO) * LANE_TILES, :]

    for r8 in range(tm // SUBLANES):
        for c in range(LANE_TILES):
            src = pl.ds(r8 * SUBLANES * LANE_TILES + c, SUBLANES, stride=LANE_TILES)
            yc_ref[r8 * SUBLANES:(r8 + 1) * SUBLANES, c * LANES:(c + 1) * LANES] = y3_ref[src, :]

    z = _dot(h, win_ref[:, 2 * CONV_W:3 * CONV_W]) + bin_ref[:, 2 * CONV_W:3 * CONV_W]
    act = _layernorm_act(yc_ref[...], z, lng_ref[...], lnb_ref[...])
    y_ref[...] = _rmsnorm(x + _dot(act, wout_ref[...]) + bout_ref[...], fng_ref[...])


def _even_decode_front_kernel(x_ref, ng_ref, wmain_ref, wup_ref, bup_ref, wsc_ref, sbuf_ref,
                              q_ref, k_ref, a_ref, v_ref, sg_ref, y_ref, snew_ref):
    h = _rmsnorm(x_ref[...], ng_ref[...]).astype(BF16)
    q_ref[...] = _proj(h, wmain_ref, COL_Q, COL_K) * Q_SCALE
    k_ref[...] = _proj(h, wmain_ref, COL_K, COL_V)
    v_ref[...] = _proj(h, wmain_ref, COL_V, COL_G)
    sg_ref[...] = _silu(_proj(h, wmain_ref, COL_G, COL_A_LOW))
    a_ref[...] = jnp.exp(_log_decay(h, wmain_ref, wup_ref, bup_ref))
    u = _proj(h, wmain_ref, COL_GATE_C, COL_ZB) * _proj(h, wmain_ref, COL_HB, COL_GATE_B)
    prev2 = sbuf_ref[:, 0:CONV_W]
    prev1 = sbuf_ref[:, CONV_W:2 * CONV_W]
    y_ref[...] = _short_conv_gate(u, prev1, prev2, _proj(h, wmain_ref, COL_GATE_B, COL_GATE_C),
                                  _proj(h, wmain_ref, COL_ZB, MAIN_W), wsc_ref)
    snew_ref[:, 0:CONV_W] = prev1
    snew_ref[:, CONV_W:2 * CONV_W] = u


def _lane_bcast_column(row):
    return jnp.broadcast_to(row, (DK, DK)).T


def _gla_decode_kernel(q_ref, k_ref, a_ref, v_ref, s_ref, snew_ref, o_ref):
    for b in range(DECODE_STATE_BLOCK):
        for hh in range(HEADS):
            kcols = slice(hh * DK, (hh + 1) * DK)
            vcols = slice(hh * DV, (hh + 1) * DV)
            a_col = _lane_bcast_column(a_ref[b:b + 1, kcols])
            k_col = _lane_bcast_column(k_ref[b:b + 1, kcols])
            q_col = _lane_bcast_column(q_ref[b:b + 1, kcols])
            v_row = v_ref[b:b + 1, vcols]
            halves = []
            for half in range(DV // DK):
                lanes = slice(half * DK, (half + 1) * DK)
                s_new = a_col * s_ref[b, hh, :, lanes] + k_col * v_row[:, lanes]
                snew_ref[b, hh, :, lanes] = s_new
                halves.append(jnp.sum(q_col * s_new, axis=0, keepdims=True))
            o_ref[b:b + 1, vcols] = jnp.concatenate(halves, axis=1)


def _even_decode_out_kernel(x_ref, o_ref, sg_ref, y_ref, gng_ref, wout_ref, x1_ref):
    gng = gng_ref[...]
    parts = [_head_rmsnorm(o_ref[:, hh * DV:(hh + 1) * DV], gng) for hh in range(HEADS)]
    o_mix = (jnp.concatenate(parts, axis=1) * sg_ref[...]).astype(BF16)
    out = _dot(o_mix, wout_ref[0:V_WIDTH, :]) + _dot(y_ref[...].astype(BF16), wout_ref[V_WIDTH:V_WIDTH + CONV_W, :])
    x1_ref[...] = x_ref[...] + out


def _odd_decode_kernel(x_ref, ng_ref, win_ref, bin_ref, wdw_ref, bdw_ref, lng_ref, lnb_ref, wout_ref,
                       bout_ref, fng_ref, cbuf_ref,
                       y_ref, cnew_ref):
    n_hist = CCONV_K - 1
    x = x_ref[...]
    h = _rmsnorm(x, ng_ref[...]).astype(BF16)
    a = _dot(h, win_ref[:, 0:CONV_W]) + bin_ref[:, 0:CONV_W]
    a_gate = _dot(h, win_ref[:, CONV_W:2 * CONV_W]) + bin_ref[:, CONV_W:2 * CONV_W]
    u = _gate(a, a_gate)
    yc = bdw_ref[...] + wdw_ref[n_hist:CCONV_K, :] * u
    for j in range(n_hist):
        tap = cbuf_ref[j]
        yc = yc + wdw_ref[j:j + 1, :] * tap
        if j >= 1:
            cnew_ref[j - 1] = tap
    cnew_ref[n_hist - 1] = u
    z = _dot(h, win_ref[:, 2 * CONV_W:3 * CONV_W]) + bin_ref[:, 2 * CONV_W:3 * CONV_W]
    act = _layernorm_act(yc, z, lng_ref[...], lnb_ref[...])
    y_ref[...] = _rmsnorm(x + _dot(act, wout_ref[...]) + bout_ref[...], fng_ref[...])


def _const_spec(shape):
    nd = len(shape)
    return pl.BlockSpec(shape, lambda *_: (0,) * nd, pipeline_mode=pl.Buffered(1))


def _row(v):
    return v.reshape(1, -1)


def kernel(x_prompt, x_sample, state_gla, state_sconv, state_cconv, norm_g, w_in_a, w_gate_up, b_gate_up, gla_norm_g, w_sconv, w_out_a, w_in_c, b_in_c, w_dwconv, b_dwconv, ln_g, ln_b, w_out_c, b_out_c, final_norm_g):
    bsz, seq, d = x_prompt.shape
    dec_b = x_sample.shape[0]
    assert d == D_MODEL and seq % PROMPT_TILE == 0 and x_sample.shape[1] == 1
    assert w_in_a.shape[0] == 1 and w_in_c.shape[0] == 1 and norm_g.shape[0] == 2
    n_t = seq // PROMPT_TILE

    assert w_in_a.shape[2] == MAIN_W
    wmain = w_in_a[0].T.astype(BF16)
    wup = jnp.pad(w_gate_up[0], ((0, GATE_RANK_PAD - GATE_RANK), (0, 0))).astype(BF16)
    bup = _row(b_gate_up[0])
    gng = _row(gla_norm_g[0])
    wsc = w_sconv[0]
    wout_a = w_out_a[0].astype(BF16)
    ng0 = _row(norm_g[0])
    ng1 = _row(norm_g[1])
    win_c = w_in_c[0].astype(BF16)
    bin_c = _row(b_in_c[0])
    wdw = w_dwconv[0]
    bdw = _row(b_dwconv[0])
    lng = _row(ln_g[0])
    lnb = _row(ln_b[0])
    wout_c = w_out_c[0].astype(BF16)
    bout = _row(b_out_c[0])
    fng = _row(final_norm_g)

    even_weights = (ng0, wmain, wup, bup)
    even_prompt_weights = even_weights + (gng, wsc, wout_a)
    odd_weights = (ng1, win_c, bin_c, wdw, bdw, lng, lnb, wout_c, bout, fng)
    odd_weight_specs = [_const_spec(w.shape) for w in odd_weights]
    wdw3 = wdw.reshape(CCONV_K * LANE_TILES, LANES)
    bdw3 = b_dwconv[0].reshape(LANE_TILES, LANES)
    odd_prompt_weights = (ng1, win_c, bin_c, wdw3, bdw3, lng, lnb, wout_c, bout, fng)

    tm = PROMPT_TILE
    tile_spec = pl.BlockSpec((None, tm, D_MODEL), lambda b, t: (b, t, 0))
    prompt_params = pltpu.CompilerParams(dimension_semantics=("arbitrary", "arbitrary"),
                                         vmem_limit_bytes=VMEM_LIMIT)

    x1_p, gla_p, sconv_p = pl.pallas_call(
        _even_prompt_kernel,
        grid=(bsz, n_t),
        in_specs=[tile_spec] + [_const_spec(w.shape) for w in even_prompt_weights],
        out_specs=[tile_spec,
                   pl.BlockSpec((None, None, HEADS, DK, DV), lambda b, t: (0, b, 0, 0, 0)),
                   pl.BlockSpec((None, None, SCONV_K - 1, CONV_W), lambda b, t: (0, b, 0, 0))],
        out_shape=[jax.ShapeDtypeStruct((bsz, seq, D_MODEL), F32),
                   jax.ShapeDtypeStruct((1, bsz, HEADS, DK, DV), F32),
                   jax.ShapeDtypeStruct((1, bsz, SCONV_K - 1, CONV_W), F32)],
        scratch_shapes=[pltpu.VMEM((HEADS, DV, DK), F32),
                        pltpu.VMEM((tm + SCONV_HALO, CONV_W), F32),
                        pltpu.VMEM((tm, QK_WIDTH), BF16),
                        pltpu.VMEM((tm, QK_WIDTH), BF16),
                        pltpu.VMEM((tm, QK_WIDTH), BF16),
                        pltpu.VMEM((tm, V_WIDTH), BF16),
                        pltpu.VMEM((tm, QK_WIDTH), F32),
                        pltpu.VMEM((tm, V_WIDTH), F32)],
        compiler_params=prompt_params,
        name="even_prompt",
    )(x_prompt, *even_prompt_weights)

    y_p, cconv_p = pl.pallas_call(
        _odd_prompt_kernel,
        grid=(bsz, n_t),
        in_specs=[tile_spec] + [_const_spec(w.shape) for w in odd_prompt_weights],
        out_specs=[tile_spec,
                   pl.BlockSpec((None, None, CCONV_K - 1, CONV_W), lambda b, t: (0, b, 0, 0))],
        out_shape=[jax.ShapeDtypeStruct((bsz, seq, D_MODEL), F32),
                   jax.ShapeDtypeStruct((1, bsz, CCONV_K - 1, CONV_W), F32)],
        scratch_shapes=[pltpu.VMEM(((tm + CCONV_HALO) * LANE_TILES, LANES), F32),
                        pltpu.VMEM((tm * LANE_TILES, LANES), F32),
                        pltpu.VMEM((tm, CONV_W), F32),
                        pltpu.VMEM((CCONV_HALO, CONV_W), F32)],
        compiler_params=prompt_params,
        name="odd_prompt",
    )(x1_p, *odd_prompt_weights)

    xs = x_sample.reshape(dec_b, D_MODEL)
    sbuf = state_sconv.reshape(dec_b, (SCONV_K - 1) * CONV_W)
    single = pltpu.CompilerParams(vmem_limit_bytes=VMEM_LIMIT)
    q_s, k_s, a_s, v_s, sg_s, ysc_s, sconv_s = pl.pallas_call(
        _even_decode_front_kernel,
        out_shape=[jax.ShapeDtypeStruct((dec_b, QK_WIDTH), F32),
                   jax.ShapeDtypeStruct((dec_b, QK_WIDTH), F32),
                   jax.ShapeDtypeStruct((dec_b, QK_WIDTH), F32),
                   jax.ShapeDtypeStruct((dec_b, V_WIDTH), F32),
                   jax.ShapeDtypeStruct((dec_b, V_WIDTH), F32),
                   jax.ShapeDtypeStruct((dec_b, CONV_W), F32),
                   jax.ShapeDtypeStruct((dec_b, (SCONV_K - 1) * CONV_W), F32)],
        compiler_params=single,
        name="even_decode_front",
    )(xs, *even_weights, wsc, sbuf)

    sb = DECODE_STATE_BLOCK
    assert dec_b % sb == 0
    vec_spec = lambda w: pl.BlockSpec((sb, w), lambda i: (i, 0))
    state_spec = pl.BlockSpec((sb, HEADS, DK, DV), lambda i: (i, 0, 0, 0))
    gla_s, o_s = pl.pallas_call(
        _gla_decode_kernel,
        grid=(dec_b // sb,),
        in_specs=[vec_spec(QK_WIDTH), vec_spec(QK_WIDTH), vec_spec(QK_WIDTH), vec_spec(V_WIDTH), state_spec],
        out_specs=[state_spec, vec_spec(V_WIDTH)],
        out_shape=[jax.ShapeDtypeStruct((dec_b, HEADS, DK, DV), F32),
                   jax.ShapeDtypeStruct((dec_b, V_WIDTH), F32)],
        compiler_params=pltpu.CompilerParams(dimension_semantics=("arbitrary",),
                                             vmem_limit_bytes=VMEM_LIMIT),
        name="gla_decode",
    )(q_s, k_s, a_s, v_s, state_gla[0])

    x1_s = pl.pallas_call(
        _even_decode_out_kernel,
        out_shape=jax.ShapeDtypeStruct((dec_b, D_MODEL), F32),
        compiler_params=single,
        name="even_decode_out",
    )(xs, o_s, sg_s, ysc_s, gng, wout_a)

    ob = DECODE_ODD_BLOCK
    assert dec_b % ob == 0
    rows_spec = pl.BlockSpec((ob, D_MODEL), lambda i: (i, 0))
    hist_spec = pl.BlockSpec((CCONV_K - 1, ob, CONV_W), lambda i: (0, i, 0))
    cbuf = jnp.transpose(state_cconv[0], (1, 0, 2))
    y_s, cconv_t = pl.pallas_call(
        _odd_decode_kernel,
        grid=(dec_b // ob,),
        in_specs=[rows_spec] + odd_weight_specs + [hist_spec],
        out_specs=[rows_spec, hist_spec],
        out_shape=[jax.ShapeDtypeStruct((dec_b, D_MODEL), F32),
                   jax.ShapeDtypeStruct((CCONV_K - 1, dec_b, CONV_W), F32)],
        compiler_params=pltpu.CompilerParams(dimension_semantics=("arbitrary",),
                                             vmem_limit_bytes=VMEM_LIMIT),
        name="odd_decode",
    )(x1_s, *odd_weights, cbuf)
    cconv_s = jnp.transpose(cconv_t, (1, 0, 2))[None]

    return (y_p,
            y_s.reshape(dec_b, 1, D_MODEL),
            gla_p,
            sconv_p,
            cconv_p,
            gla_s.reshape(1, dec_b, HEADS, DK, DV),
            sconv_s.reshape(1, dec_b, SCONV_K - 1, CONV_W),
            cconv_s)
```

```python
import jax
import jax.numpy as jnp
from jax import lax
from jax.experimental import pallas as pl
from jax.experimental.pallas import tpu as pltpu

F32 = jnp.float32
BF16 = jnp.bfloat16

D_MODEL = 1024
HEADS = 4
DK = 128
DV = 256
QK_WIDTH = HEADS * DK
V_WIDTH = HEADS * DV
GATE_RANK = 16
GATE_RANK_PAD = 128
GATE_TEMP_INV = 1.0 / 16.0
CHUNK = 64
CHUNK_SHIFT = 6
SCONV_K = 3
CCONV_K = 31
CONV_W = 1024
RMS_EPS = 1e-6
LN_EPS = 1e-5
Q_SCALE = DK ** -0.5

COL_Q = 0
COL_K = COL_Q + QK_WIDTH
COL_V = COL_K + QK_WIDTH
COL_G = COL_V + V_WIDTH
COL_A_LOW = COL_G + V_WIDTH
COL_HB = COL_A_LOW + GATE_RANK
COL_GATE_B = COL_HB + CONV_W
COL_GATE_C = COL_GATE_B + CONV_W
COL_ZB = COL_GATE_C + CONV_W
MAIN_W = COL_ZB + CONV_W

SUBLANES = 8
LANES = 128
MXU_K = 256
MXU_N = 256
LANE_TILES = CONV_W // LANES
PROMPT_TILE = 1024
CCONV_HALO = 32
SCONV_HALO = 8
CCONV_TIME_BLOCK = 16
DECODE_STATE_BLOCK = 16
DECODE_ODD_BLOCK = 64
VMEM_LIMIT = 60 * 1024 * 1024


def _dot(a, b):
    return jnp.dot(a, b, preferred_element_type=F32)


def _dot_nt(a, b):
    return lax.dot_general(a, b, (((1,), (1,)), ((), ())), preferred_element_type=F32)


def _dot_tn(a, b):
    return lax.dot_general(a, b, (((0,), (0,)), ((), ())), preferred_element_type=F32)


def _proj(h, wt_ref, lo, hi):
    return _dot_nt(h, wt_ref[lo:hi, :])


def _rmsnorm(x, g):
    ms = jnp.mean(x * x, axis=-1, keepdims=True)
    return x * lax.rsqrt(ms + RMS_EPS) * g


def _gate(x, y):
    return x / (1.0 + jnp.exp(-y))


def _silu(x):
    return _gate(x, x)


def _log_sigmoid(x):
    return -(jnp.maximum(-x, 0.0) + jnp.log(1.0 + jnp.exp(-jnp.abs(x))))


def _log_decay(h, wmain_ref, wup_ref, bup_ref):
    a_low = _proj(h, wmain_ref, COL_A_LOW, COL_A_LOW + GATE_RANK_PAD).astype(BF16)
    logit = _dot(a_low, wup_ref[...]) + bup_ref[...]
    return _log_sigmoid(logit) * GATE_TEMP_INV


def _head_rmsnorm(o, gng):
    ms = jnp.mean(o * o, axis=-1, keepdims=True)
    return o * lax.rsqrt(ms + RMS_EPS) * gng


def _layernorm_act(yc, z, lng, lnb):
    mu = jnp.mean(yc, axis=-1, keepdims=True)
    xc = yc - mu
    var = jnp.mean(xc * xc, axis=-1, keepdims=True)
    yn = xc * lax.rsqrt(var + LN_EPS) * lng + lnb
    return (_silu(yn) * _silu(z)).astype(BF16)


def _short_conv_gate(u, prev1, prev2, gate_b, z_b, wsc_ref):
    y = wsc_ref[2:3, :] * u + wsc_ref[1:2, :] * prev1 + wsc_ref[0:1, :] * prev2
    return gate_b * y * _silu(z_b)


def _even_prompt_kernel(x_ref, ng_ref, wmain_ref, wup_ref, bup_ref, gng_ref, wsc_ref, wout_ref,
                        x1_ref, sgla_ref, sconv_ref,
                        st_ref, ubuf_ref, qe_ref, ke_ref, kd_ref, v_ref, dec_ref, mix_ref):
    tm = PROMPT_TILE
    t = pl.program_id(1)
    last_t = pl.num_programs(1) - 1

    @pl.when(t == 0)
    def _():
        st_ref[...] = jnp.zeros_like(st_ref)
        ubuf_ref[0:SCONV_HALO, :] = jnp.zeros((SCONV_HALO, CONV_W), F32)

    x = x_ref[...]
    h = _rmsnorm(x, ng_ref[...]).astype(BF16)
    q = _proj(h, wmain_ref, COL_Q, COL_K) * Q_SCALE
    k = _proj(h, wmain_ref, COL_K, COL_V)
    v_ref[...] = _proj(h, wmain_ref, COL_V, COL_G).astype(BF16)
    log_a = _log_decay(h, wmain_ref, wup_ref, bup_ref)

    row = lax.broadcasted_iota(jnp.int32, (MXU_K, MXU_K), 0)
    col = lax.broadcasted_iota(jnp.int32, (MXU_K, MXU_K), 1)
    in_chunk_causal = ((row >> CHUNK_SHIFT) == (col >> CHUNK_SHIFT)) & (col <= row)
    tri = jnp.where(in_chunk_causal, 1.0, 0.0).astype(BF16)
    la_hi = log_a.astype(BF16)
    la_lo = (log_a - la_hi.astype(F32)).astype(BF16)
    for sb in range(tm // MXU_K):
        rows = slice(sb * MXU_K, (sb + 1) * MXU_K)
        b_cum = _dot(tri, la_hi[rows, :]) + _dot(tri, la_lo[rows, :])
        b_tot = jnp.concatenate(
            [jnp.broadcast_to(b_cum[(c + 1) * CHUNK - 1:(c + 1) * CHUNK, :], (CHUNK, QK_WIDTH))
             for c in range(MXU_K // CHUNK)], axis=0)
        qe_ref[rows, :] = (q[rows, :] * jnp.exp(b_cum)).astype(BF16)
        ke_ref[rows, :] = (k[rows, :] * jnp.exp(-b_cum)).astype(BF16)
        kd_ref[rows, :] = (k[rows, :] * jnp.exp(b_tot - b_cum)).astype(BF16)
        dec_ref[rows, :] = jnp.exp(b_tot)

    gng = gng_ref[...]
    for hh in range(HEADS):
        kcols = slice(hh * DK, (hh + 1) * DK)
        vcols = slice(hh * DV, (hh + 1) * DV)
        st = st_ref[hh]
        for sb in range(tm // MXU_K):
            rows = slice(sb * MXU_K, (sb + 1) * MXU_K)
            sc = jnp.where(in_chunk_causal, _dot_nt(qe_ref[rows, kcols], ke_ref[rows, kcols]), 0.0)
            o_intra = _dot(sc.astype(BF16), v_ref[rows, vcols])
            for c in range(MXU_K // CHUNK):
                r0 = sb * MXU_K + c * CHUNK
                crow = slice(r0, r0 + CHUNK)
                o = o_intra[c * CHUNK:(c + 1) * CHUNK, :] + _dot_nt(qe_ref[crow, kcols], st.astype(BF16))
                mix_ref[crow, vcols] = _head_rmsnorm(o, gng)
                dec = dec_ref[r0:r0 + 1, kcols]
                st = st * dec + _dot_tn(v_ref[crow, vcols], kd_ref[crow, kcols])
        st_ref[hh] = st

    o_mix = (mix_ref[...] * _silu(_proj(h, wmain_ref, COL_G, COL_A_LOW))).astype(BF16)

    u = _proj(h, wmain_ref, COL_GATE_C, COL_ZB) * _proj(h, wmain_ref, COL_HB, COL_GATE_B)
    ubuf_ref[SCONV_HALO:SCONV_HALO + tm, :] = u
    y = _short_conv_gate(u, ubuf_ref[pl.ds(SCONV_HALO - 1, tm), :], ubuf_ref[pl.ds(SCONV_HALO - 2, tm), :],
                         _proj(h, wmain_ref, COL_GATE_B, COL_GATE_C), _proj(h, wmain_ref, COL_ZB, MAIN_W),
                         wsc_ref).astype(BF16)
    ubuf_ref[0:SCONV_HALO, :] = ubuf_ref[tm:tm + SCONV_HALO, :]

    out = _dot(o_mix, wout_ref[0:V_WIDTH, :]) + _dot(y, wout_ref[V_WIDTH:V_WIDTH + CONV_W, :])
    x1_ref[...] = x + out

    @pl.when(t == last_t)
    def _():
        for hh in range(HEADS):
            sgla_ref[hh] = st_ref[hh].T
        sconv_ref[...] = ubuf_ref[pl.ds(SCONV_HALO + tm - (SCONV_K - 1), SCONV_K - 1), :]


def _odd_prompt_kernel(x_ref, ng_ref, win_ref, bin_ref, wdw_ref, bdw_ref, lng_ref, lnb_ref, wout_ref,
                       bout_ref, fng_ref,
                       y_ref, cconv_ref,
                       u3_ref, y3_ref, yc_ref, tail_ref):
    tm = PROMPT_TILE
    t = pl.program_id(1)
    last_t = pl.num_programs(1) - 1

    @pl.when(t == 0)
    def _():
        u3_ref[0:CCONV_HALO * LANE_TILES, :] = jnp.zeros((CCONV_HALO * LANE_TILES, LANES), F32)

    x = x_ref[...]
    h = _rmsnorm(x, ng_ref[...]).astype(BF16)
    for g0 in range(0, CONV_W, MXU_N):
        cols = slice(g0, g0 + MXU_N)
        gcols = slice(CONV_W + g0, CONV_W + g0 + MXU_N)
        u = _gate(_dot(h, win_ref[:, cols]) + bin_ref[:, cols], _dot(h, win_ref[:, gcols]) + bin_ref[:, gcols])
        tail_ref[:, cols] = u[tm - CCONV_HALO:, :]
        for r8 in range(tm // SUBLANES):
            for c in range(MXU_N // LANES):
                dst = pl.ds((CCONV_HALO + r8 * SUBLANES) * LANE_TILES + g0 // LANES + c, SUBLANES,
                            stride=LANE_TILES)
                u3_ref[dst, :] = u[r8 * SUBLANES:(r8 + 1) * SUBLANES, c * LANES:(c + 1) * LANES]

    @pl.when(t == last_t)
    def _():
        cconv_ref[...] = tail_ref[CCONV_HALO - (CCONV_K - 1):, :]

    base = CCONV_HALO - (CCONV_K - 1)
    tb = CCONV_TIME_BLOCK
    bdw = bdw_ref[...]
    for blk in range(tm // tb):
        acc = jnp.broadcast_to(bdw[None], (tb, LANE_TILES, LANES))
        for j in range(CCONV_K):
            rows = pl.ds((blk * tb + base + j) * LANE_TILES, tb * LANE_TILES)
            w_j = wdw_ref[j * LANE_TILES:(j + 1) * LANE_TILES, :]
            acc = acc + w_j[None] * u3_ref[rows, :].reshape(tb, LANE_TILES, LANES)
        y3_ref[blk * tb * LANE_TILES:(blk + 1) * tb * LANE_TILES, :] = acc.reshape(tb * LANE_TILES, LANES)

    u3_ref[0:CCONV_HALO * LANE_TILES, :] = u3_ref[tm * LANE_TILES:(tm + CCONV_HALO) * LANE_TILES, :]

    for r8 in range(tm // SUBLANES):
        for c in range(LANE_TILES):
            src = pl.ds(r8 * SUBLANES * LANE_TILES + c, SUBLANES, stride=LANE_TILES)
            yc_ref[r8 * SUBLANES:(r8 + 1) * SUBLANES, c * LANES:(c + 1) * LANES] = y3_ref[src, :]

    z = _dot(h, win_ref[:, 2 * CONV_W:3 * CONV_W]) + bin_ref[:, 2 * CONV_W:3 * CONV_W]
    act = _layernorm_act(yc_ref[...], z, lng_ref[...], lnb_ref[...])
    y_ref[...] = _rmsnorm(x + _dot(act, wout_ref[...]) + bout_ref[...], fng_ref[...])


def _even_decode_front_kernel(x_ref, ng_ref, wmain_ref, wup_ref, bup_ref, wsc_ref, sbuf_ref,
                              q_ref, k_ref, a_ref, v_ref, sg_ref, y_ref, snew_ref):
    h = _rmsnorm(x_ref[...], ng_ref[...]).astype(BF16)
    q_ref[...] = _proj(h, wmain_ref, COL_Q, COL_K) * Q_SCALE
    k_ref[...] = _proj(h, wmain_ref, COL_K, COL_V)
    v_ref[...] = _proj(h, wmain_ref, COL_V, COL_G)
    sg_ref[...] = _silu(_proj(h, wmain_ref, COL_G, COL_A_LOW))
    a_ref[...] = jnp.exp(_log_decay(h, wmain_ref, wup_ref, bup_ref))
    u = _proj(h, wmain_ref, COL_GATE_C, COL_ZB) * _proj(h, wmain_ref, COL_HB, COL_GATE_B)
    prev2 = sbuf_ref[:, 0:CONV_W]
    prev1 = sbuf_ref[:, CONV_W:2 * CONV_W]
    y_ref[...] = _short_conv_gate(u, prev1, prev2, _proj(h, wmain_ref, COL_GATE_B, COL_GATE_C),
                                  _proj(h, wmain_ref, COL_ZB, MAIN_W), wsc_ref)
    snew_ref[:, 0:CONV_W] = prev1
    snew_ref[:, CONV_W:2 * CONV_W] = u


def _lane_bcast_column(row):
    return jnp.broadcast_to(row, (DK, DK)).T


def _gla_decode_kernel(q_ref, k_ref, a_ref, v_ref, s_ref, snew_ref, o_ref):
    for b in range(DECODE_STATE_BLOCK):
        for hh in range(HEADS):
            kcols = slice(hh * DK, (hh + 1) * DK)
            vcols = slice(hh * DV, (hh + 1) * DV)
            a_col = _lane_bcast_column(a_ref[b:b + 1, kcols])
            k_col = _lane_bcast_column(k_ref[b:b + 1, kcols])
            q_col = _lane_bcast_column(q_ref[b:b + 1, kcols])
            v_row = v_ref[b:b + 1, vcols]
            halves = []
            for half in range(DV // DK):
                lanes = slice(half * DK, (half + 1) * DK)
                s_new = a_col * s_ref[b, hh, :, lanes] + k_col * v_row[:, lanes]
                snew_ref[b, hh, :, lanes] = s_new
                halves.append(jnp.sum(q_col * s_new, axis=0, keepdims=True))
            o_ref[b:b + 1, vcols] = jnp.concatenate(halves, axis=1)


def _even_decode_out_kernel(x_ref, o_ref, sg_ref, y_ref, gng_ref, wout_ref, x1_ref):
    gng = gng_ref[...]
    parts = [_head_rmsnorm(o_ref[:, hh * DV:(hh + 1) * DV], gng) for hh in range(HEADS)]
    o_mix = (jnp.concatenate(parts, axis=1) * sg_ref[...]).astype(BF16)
    out = _dot(o_mix, wout_ref[0:V_WIDTH, :]) + _dot(y_ref[...].astype(BF16), wout_ref[V_WIDTH:V_WIDTH + CONV_W, :])
    x1_ref[...] = x_ref[...] + out


def _odd_decode_kernel(x_ref, ng_ref, win_ref, bin_ref, wdw_ref, bdw_ref, lng_ref, lnb_ref, wout_ref,
                       bout_ref, fng_ref, cbuf_ref,
                       y_ref, cnew_ref):
    n_hist = CCONV_K - 1
    x = x_ref[...]
    h = _rmsnorm(x, ng_ref[...]).astype(BF16)
    a = _dot(h, win_ref[:, 0:CONV_W]) + bin_ref[:, 0:CONV_W]
    a_gate = _dot(h, win_ref[:, CONV_W:2 * CONV_W]) + bin_ref[:, CONV_W:2 * CONV_W]
    u = _gate(a, a_gate)
    yc = bdw_ref[...] + wdw_ref[n_hist:CCONV_K, :] * u
    for j in range(n_hist):
        tap = cbuf_ref[j]
        yc = yc + wdw_ref[j:j + 1, :] * tap
        if j >= 1:
            cnew_ref[j - 1] = tap
    cnew_ref[n_hist - 1] = u
    z = _dot(h, win_ref[:, 2 * CONV_W:3 * CONV_W]) + bin_ref[:, 2 * CONV_W:3 * CONV_W]
    act = _layernorm_act(yc, z, lng_ref[...], lnb_ref[...])
    y_ref[...] = _rmsnorm(x + _dot(act, wout_ref[...]) + bout_ref[...], fng_ref[...])


def _const_spec(shape):
    nd = len(shape)
    return pl.BlockSpec(shape, lambda *_: (0,) * nd, pipeline_mode=pl.Buffered(1))


def _row(v):
    return v.reshape(1, -1)


def kernel(x_prompt, x_sample, state_gla, state_sconv, state_cconv, norm_g, w_in_a, w_gate_up, b_gate_up, gla_norm_g, w_sconv, w_out_a, w_in_c, b_in_c, w_dwconv, b_dwconv, ln_g, ln_b, w_out_c, b_out_c, final_norm_g):
    bsz, seq, d = x_prompt.shape
    dec_b = x_sample.shape[0]
    assert d == D_MODEL and seq % PROMPT_TILE == 0 and x_sample.shape[1] == 1
    assert w_in_a.shape[0] == 1 and w_in_c.shape[0] == 1 and norm_g.shape[0] == 2
    n_t = seq // PROMPT_TILE

    assert w_in_a.shape[2] == MAIN_W
    wmain = w_in_a[0].T.astype(BF16)
    wup = jnp.pad(w_gate_up[0], ((0, GATE_RANK_PAD - GATE_RANK), (0, 0))).astype(BF16)
    bup = _row(b_gate_up[0])
    gng = _row(gla_norm_g[0])
    wsc = w_sconv[0]
    wout_a = w_out_a[0].astype(BF16)
    ng0 = _row(norm_g[0])
    ng1 = _row(norm_g[1])
    win_c = w_in_c[0].astype(BF16)
    bin_c = _row(b_in_c[0])
    wdw = w_dwconv[0]
    bdw = _row(b_dwconv[0])
    lng = _row(ln_g[0])
    lnb = _row(ln_b[0])
    wout_c = w_out_c[0].astype(BF16)
    bout = _row(b_out_c[0])
    fng = _row(final_norm_g)

    even_weights = (ng0, wmain, wup, bup)
    even_prompt_weights = even_weights + (gng, wsc, wout_a)
    odd_weights = (ng1, win_c, bin_c, wdw, bdw, lng, lnb, wout_c, bout, fng)
    odd_weight_specs = [_const_spec(w.shape) for w in odd_weights]
    wdw3 = wdw.reshape(CCONV_K * LANE_TILES, LANES)
    bdw3 = b_dwconv[0].reshape(LANE_TILES, LANES)
    odd_prompt_weights = (ng1, win_c, bin_c, wdw3, bdw3, lng, lnb, wout_c, bout, fng)

    tm = PROMPT_TILE
    tile_spec = pl.BlockSpec((None, tm, D_MODEL), lambda b, t: (b, t, 0))
    prompt_params = pltpu.CompilerParams(dimension_semantics=("arbitrary", "arbitrary"),
                                         vmem_limit_bytes=VMEM_LIMIT)

    x1_p, gla_p, sconv_p = pl.pallas_call(
        _even_prompt_kernel,
        grid=(bsz, n_t),
        in_specs=[tile_spec] + [_const_spec(w.shape) for w in even_prompt_weights],
        out_specs=[tile_spec,
                   pl.BlockSpec((None, None, HEADS, DK, DV), lambda b, t: (0, b, 0, 0, 0)),
                   pl.BlockSpec((None, None, SCONV_K - 1, CONV_W), lambda b, t: (0, b, 0, 0))],
        out_shape=[jax.ShapeDtypeStruct((bsz, seq, D_MODEL), F32),
                   jax.ShapeDtypeStruct((1, bsz, HEADS, DK, DV), F32),
                   jax.ShapeDtypeStruct((1, bsz, SCONV_K - 1, CONV_W), F32)],
        scratch_shapes=[pltpu.VMEM((HEADS, DV, DK), F32),
                        pltpu.VMEM((tm + SCONV_HALO, CONV_W), F32),
                        pltpu.VMEM((tm, QK_WIDTH), BF16),
                        pltpu.VMEM((tm, QK_WIDTH), BF16),
                        pltpu.VMEM((tm, QK_WIDTH), BF16),
                        pltpu.VMEM((tm, V_WIDTH), BF16),
                        pltpu.VMEM((tm, QK_WIDTH), F32),
                        pltpu.VMEM((tm, V_WIDTH), F32)],
        compiler_params=prompt_params,
        name="even_prompt",
    )(x_prompt, *even_prompt_weights)

    y_p, cconv_p = pl.pallas_call(
        _odd_prompt_kernel,
        grid=(bsz, n_t),
        in_specs=[tile_spec] + [_const_spec(w.shape) for w in odd_prompt_weights],
        out_specs=[tile_spec,
                   pl.BlockSpec((None, None, CCONV_K - 1, CONV_W), lambda b, t: (0, b, 0, 0))],
        out_shape=[jax.ShapeDtypeStruct((bsz, seq, D_MODEL), F32),
                   jax.ShapeDtypeStruct((1, bsz, CCONV_K - 1, CONV_W), F32)],
        scratch_shapes=[pltpu.VMEM(((tm + CCONV_HALO) * LANE_TILES, LANES), F32),
                        pltpu.VMEM((tm * LANE_TILES, LANES), F32),
                        pltpu.VMEM((tm, CONV_W), F32),
                        pltpu.VMEM((CCONV_HALO, CONV_W), F32)],
        compiler_params=prompt_params,
        name="odd_prompt",
    )(x1_p, *odd_prompt_weights)

    xs = x_sample.reshape(dec_b, D_MODEL)
    sbuf = state_sconv.reshape(dec_b, (SCONV_K - 1) * CONV_W)
    single = pltpu.CompilerParams(vmem_limit_bytes=VMEM_LIMIT)
    q_s, k_s, a_s, v_s, sg_s, ysc_s, sconv_s = pl.pallas_call(
        _even_decode_front_kernel,
        out_shape=[jax.ShapeDtypeStruct((dec_b, QK_WIDTH), F32),
                   jax.ShapeDtypeStruct((dec_b, QK_WIDTH), F32),
                   jax.ShapeDtypeStruct((dec_b, QK_WIDTH), F32),
                   jax.ShapeDtypeStruct((dec_b, V_WIDTH), F32),
                   jax.ShapeDtypeStruct((dec_b, V_WIDTH), F32),
                   jax.ShapeDtypeStruct((dec_b, CONV_W), F32),
                   jax.ShapeDtypeStruct((dec_b, (SCONV_K - 1) * CONV_W), F32)],
        compiler_params=single,
        name="even_decode_front",
    )(xs, *even_weights, wsc, sbuf)

    sb = DECODE_STATE_BLOCK
    assert dec_b % sb == 0
    vec_spec = lambda w: pl.BlockSpec((sb, w), lambda i: (i, 0))
    state_spec = pl.BlockSpec((sb, HEADS, DK, DV), lambda i: (i, 0, 0, 0))
    gla_s, o_s = pl.pallas_call(
        _gla_decode_kernel,
        grid=(dec_b // sb,),
        in_specs=[vec_spec(QK_WIDTH), vec_spec(QK_WIDTH), vec_spec(QK_WIDTH), vec_spec(V_WIDTH), state_spec],
        out_specs=[state_spec, vec_spec(V_WIDTH)],
        out_shape=[jax.ShapeDtypeStruct((dec_b, HEADS, DK, DV), F32),
                   jax.ShapeDtypeStruct((dec_b, V_WIDTH), F32)],
        compiler_params=pltpu.CompilerParams(dimension_semantics=("arbitrary",),
                                             vmem_limit_bytes=VMEM_LIMIT),
        name="gla_decode",
    )(q_s, k_s, a_s, v_s, state_gla[0])

    x1_s = pl.pallas_call(
        _even_decode_out_kernel,
        out_shape=jax.ShapeDtypeStruct((dec_b, D_MODEL), F32),
        compiler_params=single,
        name="even_decode_out",
    )(xs, o_s, sg_s, ysc_s, gng, wout_a)

    ob = DECODE_ODD_BLOCK
    assert dec_b % ob == 0
    rows_spec = pl.BlockSpec((ob, D_MODEL), lambda i: (i, 0))
    hist_spec = pl.BlockSpec((CCONV_K - 1, ob, CONV_W), lambda i: (0, i, 0))
    cbuf = jnp.transpose(state_cconv[0], (1, 0, 2))
    y_s, cconv_t = pl.pallas_call(
        _odd_decode_kernel,
        grid=(dec_b // ob,),
        in_specs=[rows_spec] + odd_weight_specs + [hist_spec],
        out_specs=[rows_spec, hist_spec],
        out_shape=[jax.ShapeDtypeStruct((dec_b, D_MODEL), F32),
                   jax.ShapeDtypeStruct((CCONV_K - 1, dec_b, CONV_W), F32)],
        compiler_params=pltpu.CompilerParams(dimension_semantics=("arbitrary",),
                                             vmem_limit_bytes=VMEM_LIMIT),
        name="odd_decode",
    )(x1_s, *odd_weights, cbuf)
    cconv_s = jnp.transpose(cconv_t, (1, 0, 2))[None]

    return (y_p,
            y_s.reshape(dec_b, 1, D_MODEL),
            gla_p,
            sconv_p,
            cconv_p,
            gla_s.reshape(1, dec_b, HEADS, DK, DV),
            sconv_s.reshape(1, dec_b, SCONV_K - 1, CONV_W),
            cconv_s)
```

```python
import jax
import jax.numpy as jnp
from jax import lax
from jax.experimental import pallas as pl
from jax.experimental.pallas import tpu as pltpu

F32 = jnp.float32
BF16 = jnp.bfloat16

D_MODEL = 1024
HEADS = 4
DK = 128
DV = 256
QK_WIDTH = HEADS * DK
V_WIDTH = HEADS * DV
GATE_RANK = 16
GATE_RANK_PAD = 128
GATE_TEMP_INV = 1.0 / 16.0
CHUNK = 64
CHUNK_SHIFT = 6
SCONV_K = 3
CCONV_K = 31
CONV_W = 1024
RMS_EPS = 1e-6
LN_EPS = 1e-5
Q_SCALE = DK ** -0.5

COL_Q = 0
COL_K = COL_Q + QK_WIDTH
COL_V = COL_K + QK_WIDTH
COL_G = COL_V + V_WIDTH
COL_A_LOW = COL_G + V_WIDTH
COL_HB = COL_A_LOW + GATE_RANK
COL_GATE_B = COL_HB + CONV_W
COL_GATE_C = COL_GATE_B + CONV_W
COL_ZB = COL_GATE_C + CONV_W
MAIN_W = COL_ZB + CONV_W

SUBLANES = 8
LANES = 128
MXU_K = 256
MXU_N = 256
LANE_TILES = CONV_W // LANES
EVEN_TILE = 512
ODD_TILE = 1024
CCONV_HALO = 32
SCONV_HALO = 8
CCONV_TIME_BLOCK = 16
DECODE_STATE_BLOCK = 16
DECODE_ODD_BLOCK = 64
VMEM_LIMIT = 60 * 1024 * 1024


def _dot(a, b):
    return jnp.dot(a, b, preferred_element_type=F32)


def _dot_nt(a, b):
    return lax.dot_general(a, b, (((1,), (1,)), ((), ())), preferred_element_type=F32)


def _dot_tn(a, b):
    return lax.dot_general(a, b, (((0,), (0,)), ((), ())), preferred_element_type=F32)


def _proj(h, wt_ref, lo, hi):
    return _dot_nt(h, wt_ref[lo:hi, :])


def _rmsnorm(x, g):
    ms = jnp.mean(x * x, axis=-1, keepdims=True)
    return x * lax.rsqrt(ms + RMS_EPS) * g


def _gate(x, y):
    return x / (1.0 + jnp.exp(-y))


def _silu(x):
    return _gate(x, x)


def _log_sigmoid(x):
    return -(jnp.maximum(-x, 0.0) + jnp.log(1.0 + jnp.exp(-jnp.abs(x))))


def _log_decay(h, wmain_ref, wup_ref, bup_ref):
    a_low = _proj(h, wmain_ref, COL_A_LOW, COL_A_LOW + GATE_RANK_PAD).astype(BF16)
    logit = _dot(a_low, wup_ref[...]) + bup_ref[...]
    return _log_sigmoid(logit) * GATE_TEMP_INV


def _head_rmsnorm(o, gng):
    ms = jnp.mean(o * o, axis=-1, keepdims=True)
    return o * lax.rsqrt(ms + RMS_EPS) * gng


def _layernorm_act(yc, z, lng, lnb):
    mu = jnp.mean(yc, axis=-1, keepdims=True)
    xc = yc - mu
    var = jnp.mean(xc * xc, axis=-1, keepdims=True)
    yn = xc * lax.rsqrt(var + LN_EPS) * lng + lnb
    return (_silu(yn) * _silu(z)).astype(BF16)


def _short_conv_gate(u, prev1, prev2, gate_b, z_b, wsc_ref):
    y = wsc_ref[2:3, :] * u + wsc_ref[1:2, :] * prev1 + wsc_ref[0:1, :] * prev2
    return gate_b * y * _silu(z_b)


def _even_prompt_kernel(x_ref, ng_ref, wmain_ref, wup_ref, bup_ref, gng_ref, wsc_ref, wout_ref,
                        x1_ref, sgla_ref, sconv_ref,
                        st_ref, ubuf_ref, qe_ref, ke_ref, kd_ref, v_ref, dec_ref, mix_ref):
    tm = EVEN_TILE
    t = pl.program_id(1)
    last_t = pl.num_programs(1) - 1

    @pl.when(t == 0)
    def _():
        st_ref[...] = jnp.zeros_like(st_ref)
        ubuf_ref[0:SCONV_HALO, :] = jnp.zeros((SCONV_HALO, CONV_W), F32)

    x = x_ref[...]
    h = _rmsnorm(x, ng_ref[...]).astype(BF16)
    q = _proj(h, wmain_ref, COL_Q, COL_K) * Q_SCALE
    k = _proj(h, wmain_ref, COL_K, COL_V)
    v_ref[...] = _proj(h, wmain_ref, COL_V, COL_G).astype(BF16)
    log_a = _log_decay(h, wmain_ref, wup_ref, bup_ref)

    row = lax.broadcasted_iota(jnp.int32, (MXU_K, MXU_K), 0)
    col = lax.broadcasted_iota(jnp.int32, (MXU_K, MXU_K), 1)
    in_chunk_causal = ((row >> CHUNK_SHIFT) == (col >> CHUNK_SHIFT)) & (col <= row)
    tri = jnp.where(in_chunk_causal, 1.0, 0.0).astype(BF16)
    la_hi = log_a.astype(BF16)
    la_lo = (log_a - la_hi.astype(F32)).astype(BF16)
    for sb in range(tm // MXU_K):
        rows = slice(sb * MXU_K, (sb + 1) * MXU_K)
        b_cum = _dot(tri, la_hi[rows, :]) + _dot(tri, la_lo[rows, :])
        b_tot = jnp.concatenate(
            [jnp.broadcast_to(b_cum[(c + 1) * CHUNK - 1:(c + 1) * CHUNK, :], (CHUNK, QK_WIDTH))
             for c in range(MXU_K // CHUNK)], axis=0)
        qe_ref[rows, :] = (q[rows, :] * jnp.exp(b_cum)).astype(BF16)
        ke_ref[rows, :] = (k[rows, :] * jnp.exp(-b_cum)).astype(BF16)
        kd_ref[rows, :] = (k[rows, :] * jnp.exp(b_tot - b_cum)).astype(BF16)
        dec_ref[rows, :] = jnp.exp(b_tot)

    gng = gng_ref[...]
    for hh in range(HEADS):
        kcols = slice(hh * DK, (hh + 1) * DK)
        vcols = slice(hh * DV, (hh + 1) * DV)
        st = st_ref[hh]
        for sb in range(tm // MXU_K):
            rows = slice(sb * MXU_K, (sb + 1) * MXU_K)
            sc = jnp.where(in_chunk_causal, _dot_nt(qe_ref[rows, kcols], ke_ref[rows, kcols]), 0.0)
            o_intra = _dot(sc.astype(BF16), v_ref[rows, vcols])
            for c in range(MXU_K // CHUNK):
                r0 = sb * MXU_K + c * CHUNK
                crow = slice(r0, r0 + CHUNK)
                o = o_intra[c * CHUNK:(c + 1) * CHUNK, :] + _dot_nt(qe_ref[crow, kcols], st.astype(BF16))
                mix_ref[crow, vcols] = _head_rmsnorm(o, gng)
                dec = dec_ref[r0:r0 + 1, kcols]
                st = st * dec + _dot_tn(v_ref[crow, vcols], kd_ref[crow, kcols])
        st_ref[hh] = st

    o_mix = (mix_ref[...] * _silu(_proj(h, wmain_ref, COL_G, COL_A_LOW))).astype(BF16)

    u = _proj(h, wmain_ref, COL_GATE_C, COL_ZB) * _proj(h, wmain_ref, COL_HB, COL_GATE_B)
    ubuf_ref[SCONV_HALO:SCONV_HALO + tm, :] = u
    y = _short_conv_gate(u, ubuf_ref[pl.ds(SCONV_HALO - 1, tm), :], ubuf_ref[pl.ds(SCONV_HALO - 2, tm), :],
                         _proj(h, wmain_ref, COL_GATE_B, COL_GATE_C), _proj(h, wmain_ref, COL_ZB, MAIN_W),
                         wsc_ref).astype(BF16)
    ubuf_ref[0:SCONV_HALO, :] = ubuf_ref[tm:tm + SCONV_HALO, :]

    out = _dot(o_mix, wout_ref[0:V_WIDTH, :]) + _dot(y, wout_ref[V_WIDTH:V_WIDTH + CONV_W, :])
    x1_ref[...] = x + out

    @pl.when(t == last_t)
    def _():
        for hh in range(HEADS):
            sgla_ref[hh] = st_ref[hh].T
        sconv_ref[...] = ubuf_ref[pl.ds(SCONV_HALO + tm - (SCONV_K - 1), SCONV_K - 1), :]


def _odd_prompt_kernel(x_ref, ng_ref, win_ref, bin_ref, wdw_ref, bdw_ref, lng_ref, lnb_ref, wout_ref,
                       bout_ref, fng_ref,
                       y_ref, cconv_ref,
                       u3_ref, y3_ref, yc_ref, tail_ref):
    tm = ODD_TILE
    t = pl.program_id(1)
    last_t = pl.num_programs(1) - 1

    @pl.when(t == 0)
    def _():
        u3_ref[0:CCONV_HALO * LANE_TILES, :] = jnp.zeros((CCONV_HALO * LANE_TILES, LANES), F32)

    x = x_ref[...]
    h = _rmsnorm(x, ng_ref[...]).astype(BF16)
    for g0 in range(0, CONV_W, MXU_N):
        cols = slice(g0, g0 + MXU_N)
        gcols = slice(CONV_W + g0, CONV_W + g0 + MXU_N)
        u = _gate(_dot(h, win_ref[:, cols]) + bin_ref[:, cols], _dot(h, win_ref[:, gcols]) + bin_ref[:, gcols])
        tail_ref[:, cols] = u[tm - CCONV_HALO:, :]
        for r8 in range(tm // SUBLANES):
            for c in range(MXU_N // LANES):
                dst = pl.ds((CCONV_HALO + r8 * SUBLANES) * LANE_TILES + g0 // LANES + c, SUBLANES,
                            stride=LANE_TILES)
                u3_ref[dst, :] = u[r8 * SUBLANES:(r8 + 1) * SUBLANES, c * LANES:(c + 1) * LANES]

    @pl.when(t == last_t)
    def _():
        cconv_ref[...] = tail_ref[CCONV_HALO - (CCONV_K - 1):, :]

    base = CCONV_HALO - (CCONV_K - 1)
    tb = CCONV_TIME_BLOCK
    bdw = bdw_ref[...]
    for blk in range(tm // tb):
        acc = jnp.broadcast_to(bdw[None], (tb, LANE_TILES, LANES))
        for j in range(CCONV_K):
            rows = pl.ds((blk * tb + base + j) * LANE_TILES, tb * LANE_TILES)
            w_j = wdw_ref[j * LANE_TILES:(j + 1) * LANE_TILES, :]
            acc = acc + w_j[None] * u3_ref[rows, :].reshape(tb, LANE_TILES, LANES)
        y3_ref[blk * tb * LANE_TILES:(blk + 1) * tb * LANE_TILES, :] = acc.reshape(tb * LANE_TILES, LANES)

    u3_ref[0:CCONV_HALO * LANE_TILES, :] = u3_ref[tm * LANE_TILES:(tm + CCONV_HALO) * LANE_TILES, :]

    for r8 in range(tm // SUBLANES):
        for c in range(LANE_TILES):
            src = pl.ds(r8 * SUBLANES * LANE_TILES + c, SUBLANES, stride=LANE_TILES)
            yc_ref[r8 * SUBLANES:(r8 + 1) * SUBLANES, c * LANES:(c + 1) * LANES] = y3_ref[src, :]

    z = _dot(h, win_ref[:, 2 * CONV_W:3 * CONV_W]) + bin_ref[:, 2 * CONV_W:3 * CONV_W]
    act = _layernorm_act(yc_ref[...], z, lng_ref[...], lnb_ref[...])
    y_ref[...] = _rmsnorm(x + _dot(act, wout_ref[...]) + bout_ref[...], fng_ref[...])


def _even_decode_front_kernel(x_ref, ng_ref, wmain_ref, wup_ref, bup_ref, wsc_ref, sbuf_ref,
                              q_ref, k_ref, a_ref, v_ref, sg_ref, y_ref, snew_ref):
    h = _rmsnorm(x_ref[...], ng_ref[...]).astype(BF16)
    q_ref[...] = _proj(h, wmain_ref, COL_Q, COL_K) * Q_SCALE
    k_ref[...] = _proj(h, wmain_ref, COL_K, COL_V)
    v_ref[...] = _proj(h, wmain_ref, COL_V, COL_G)
    sg_ref[...] = _silu(_proj(h, wmain_ref, COL_G, COL_A_LOW))
    a_ref[...] = jnp.exp(_log_decay(h, wmain_ref, wup_ref, bup_ref))
    u = _proj(h, wmain_ref, COL_GATE_C, COL_ZB) * _proj(h, wmain_ref, COL_HB, COL_GATE_B)
    prev2 = sbuf_ref[:, 0:CONV_W]
    prev1 = sbuf_ref[:, CONV_W:2 * CONV_W]
    y_ref[...] = _short_conv_gate(u, prev1, prev2, _proj(h, wmain_ref, COL_GATE_B, COL_GATE_C),
                                  _proj(h, wmain_ref, COL_ZB, MAIN_W), wsc_ref)
    snew_ref[:, 0:CONV_W] = prev1
    snew_ref[:, CONV_W:2 * CONV_W] = u


def _lane_bcast_column(row):
    return jnp.broadcast_to(row, (DK, DK)).T


def _gla_decode_kernel(q_ref, k_ref, a_ref, v_ref, s_ref, snew_ref, o_ref):
    for b in range(DECODE_STATE_BLOCK):
        for hh in range(HEADS):
            kcols = slice(hh * DK, (hh + 1) * DK)
            vcols = slice(hh * DV, (hh + 1) * DV)
            a_col = _lane_bcast_column(a_ref[b:b + 1, kcols])
            k_col = _lane_bcast_column(k_ref[b:b + 1, kcols])
            q_col = _lane_bcast_column(q_ref[b:b + 1, kcols])
            v_row = v_ref[b:b + 1, vcols]
            halves = []
            for half in range(DV // DK):
                lanes = slice(half * DK, (half + 1) * DK)
                s_new = a_col * s_ref[b, hh, :, lanes] + k_col * v_row[:, lanes]
                snew_ref[b, hh, :, lanes] = s_new
                halves.append(jnp.sum(q_col * s_new, axis=0, keepdims=True))
            o_ref[b:b + 1, vcols] = jnp.concatenate(halves, axis=1)


def _even_decode_out_kernel(x_ref, o_ref, sg_ref, y_ref, gng_ref, wout_ref, x1_ref):
    gng = gng_ref[...]
    parts = [_head_rmsnorm(o_ref[:, hh * DV:(hh + 1) * DV], gng) for hh in range(HEADS)]
    o_mix = (jnp.concatenate(parts, axis=1) * sg_ref[...]).astype(BF16)
    out = _dot(o_mix, wout_ref[0:V_WIDTH, :]) + _dot(y_ref[...].astype(BF16), wout_ref[V_WIDTH:V_WIDTH + CONV_W, :])
    x1_ref[...] = x_ref[...] + out


def _odd_decode_kernel(x_ref, ng_ref, win_ref, bin_ref, wdw_ref, bdw_ref, lng_ref, lnb_ref, wout_ref,
                       bout_ref, fng_ref, cbuf_ref,
                       y_ref, cnew_ref):
    n_hist = CCONV_K - 1
    x = x_ref[...]
    h = _rmsnorm(x, ng_ref[...]).astype(BF16)
    a = _dot(h, win_ref[:, 0:CONV_W]) + bin_ref[:, 0:CONV_W]
    a_gate = _dot(h, win_ref[:, CONV_W:2 * CONV_W]) + bin_ref[:, CONV_W:2 * CONV_W]
    u = _gate(a, a_gate)
    yc = bdw_ref[...] + wdw_ref[n_hist:CCONV_K, :] * u
    for j in range(n_hist):
        tap = cbuf_ref[j]
        yc = yc + wdw_ref[j:j + 1, :] * tap
        if j >= 1:
            cnew_ref[j - 1] = tap
    cnew_ref[n_hist - 1] = u
    z = _dot(h, win_ref[:, 2 * CONV_W:3 * CONV_W]) + bin_ref[:, 2 * CONV_W:3 * CONV_W]
    act = _layernorm_act(yc, z, lng_ref[...], lnb_ref[...])
    y_ref[...] = _rmsnorm(x + _dot(act, wout_ref[...]) + bout_ref[...], fng_ref[...])


def _const_spec(shape):
    nd = len(shape)
    return pl.BlockSpec(shape, lambda *_: (0,) * nd, pipeline_mode=pl.Buffered(1))


def _row(v):
    return v.reshape(1, -1)


def kernel(x_prompt, x_sample, state_gla, state_sconv, state_cconv, norm_g, w_in_a, w_gate_up, b_gate_up, gla_norm_g, w_sconv, w_out_a, w_in_c, b_in_c, w_dwconv, b_dwconv, ln_g, ln_b, w_out_c, b_out_c, final_norm_g):
    bsz, seq, d = x_prompt.shape
    dec_b = x_sample.shape[0]
    assert d == D_MODEL and seq % EVEN_TILE == 0 and seq % ODD_TILE == 0 and x_sample.shape[1] == 1
    assert w_in_a.shape[0] == 1 and w_in_c.shape[0] == 1 and norm_g.shape[0] == 2

    assert w_in_a.shape[2] == MAIN_W
    wmain = w_in_a[0].T.astype(BF16)
    wup = jnp.pad(w_gate_up[0], ((0, GATE_RANK_PAD - GATE_RANK), (0, 0))).astype(BF16)
    bup = _row(b_gate_up[0])
    gng = _row(gla_norm_g[0])
    wsc = w_sconv[0]
    wout_a = w_out_a[0].astype(BF16)
    ng0 = _row(norm_g[0])
    ng1 = _row(norm_g[1])
    win_c = w_in_c[0].astype(BF16)
    bin_c = _row(b_in_c[0])
    wdw = w_dwconv[0]
    bdw = _row(b_dwconv[0])
    lng = _row(ln_g[0])
    lnb = _row(ln_b[0])
    wout_c = w_out_c[0].astype(BF16)
    bout = _row(b_out_c[0])
    fng = _row(final_norm_g)

    even_weights = (ng0, wmain, wup, bup)
    even_prompt_weights = even_weights + (gng, wsc, wout_a)
    odd_weights = (ng1, win_c, bin_c, wdw, bdw, lng, lnb, wout_c, bout, fng)
    odd_weight_specs = [_const_spec(w.shape) for w in odd_weights]
    wdw3 = wdw.reshape(CCONV_K * LANE_TILES, LANES)
    bdw3 = b_dwconv[0].reshape(LANE_TILES, LANES)
    odd_prompt_weights = (ng1, win_c, bin_c, wdw3, bdw3, lng, lnb, wout_c, bout, fng)

    prompt_params = pltpu.CompilerParams(dimension_semantics=("arbitrary", "arbitrary"),
                                         vmem_limit_bytes=VMEM_LIMIT)

    tm = EVEN_TILE
    tile_spec = pl.BlockSpec((None, tm, D_MODEL), lambda b, t: (b, t, 0))
    x1_p, gla_p, sconv_p = pl.pallas_call(
        _even_prompt_kernel,
        grid=(bsz, seq // tm),
        in_specs=[tile_spec] + [_const_spec(w.shape) for w in even_prompt_weights],
        out_specs=[tile_spec,
                   pl.BlockSpec((None, None, HEADS, DK, DV), lambda b, t: (0, b, 0, 0, 0)),
                   pl.BlockSpec((None, None, SCONV_K - 1, CONV_W), lambda b, t: (0, b, 0, 0))],
        out_shape=[jax.ShapeDtypeStruct((bsz, seq, D_MODEL), F32),
                   jax.ShapeDtypeStruct((1, bsz, HEADS, DK, DV), F32),
                   jax.ShapeDtypeStruct((1, bsz, SCONV_K - 1, CONV_W), F32)],
        scratch_shapes=[pltpu.VMEM((HEADS, DV, DK), F32),
                        pltpu.VMEM((tm + SCONV_HALO, CONV_W), F32),
                        pltpu.VMEM((tm, QK_WIDTH), BF16),
                        pltpu.VMEM((tm, QK_WIDTH), BF16),
                        pltpu.VMEM((tm, QK_WIDTH), BF16),
                        pltpu.VMEM((tm, V_WIDTH), BF16),
                        pltpu.VMEM((tm, QK_WIDTH), F32),
                        pltpu.VMEM((tm, V_WIDTH), F32)],
        compiler_params=prompt_params,
        name="even_prompt",
    )(x_prompt, *even_prompt_weights)

    tm = ODD_TILE
    tile_spec = pl.BlockSpec((None, tm, D_MODEL), lambda b, t: (b, t, 0))
    y_p, cconv_p = pl.pallas_call(
        _odd_prompt_kernel,
        grid=(bsz, seq // tm),
        in_specs=[tile_spec] + [_const_spec(w.shape) for w in odd_prompt_weights],
        out_specs=[tile_spec,
                   pl.BlockSpec((None, None, CCONV_K - 1, CONV_W), lambda b, t: (0, b, 0, 0))],
        out_shape=[jax.ShapeDtypeStruct((bsz, seq, D_MODEL), F32),
                   jax.ShapeDtypeStruct((1, bsz, CCONV_K - 1, CONV_W), F32)],
        scratch_shapes=[pltpu.VMEM(((tm + CCONV_HALO) * LANE_TILES, LANES), F32),
                        pltpu.VMEM((tm * LANE_TILES, LANES), F32),
                        pltpu.VMEM((tm, CONV_W), F32),
                        pltpu.VMEM((CCONV_HALO, CONV_W), F32)],
        compiler_params=prompt_params,
        name="odd_prompt",
    )(x1_p, *odd_prompt_weights)

    xs = x_sample.reshape(dec_b, D_MODEL)
    sbuf = state_sconv.reshape(dec_b, (SCONV_K - 1) * CONV_W)
    single = pltpu.CompilerParams(vmem_limit_bytes=VMEM_LIMIT)
    q_s, k_s, a_s, v_s, sg_s, ysc_s, sconv_s = pl.pallas_call(
        _even_decode_front_kernel,
        out_shape=[jax.ShapeDtypeStruct((dec_b, QK_WIDTH), F32),
                   jax.ShapeDtypeStruct((dec_b, QK_WIDTH), F32),
                   jax.ShapeDtypeStruct((dec_b, QK_WIDTH), F32),
                   jax.ShapeDtypeStruct((dec_b, V_WIDTH), F32),
                   jax.ShapeDtypeStruct((dec_b, V_WIDTH), F32),
                   jax.ShapeDtypeStruct((dec_b, CONV_W), F32),
                   jax.ShapeDtypeStruct((dec_b, (SCONV_K - 1) * CONV_W), F32)],
        compiler_params=single,
        name="even_decode_front",
    )(xs, *even_weights, wsc, sbuf)

    sb = DECODE_STATE_BLOCK
    assert dec_b % sb == 0
    vec_spec = lambda w: pl.BlockSpec((sb, w), lambda i: (i, 0))
    state_spec = pl.BlockSpec((sb, HEADS, DK, DV), lambda i: (i, 0, 0, 0))
    gla_s, o_s = pl.pallas_call(
        _gla_decode_kernel,
        grid=(dec_b // sb,),
        in_specs=[vec_spec(QK_WIDTH), vec_spec(QK_WIDTH), vec_spec(QK_WIDTH), vec_spec(V_WIDTH), state_spec],
        out_specs=[state_spec, vec_spec(V_WIDTH)],
        out_shape=[jax.ShapeDtypeStruct((dec_b, HEADS, DK, DV), F32),
                   jax.ShapeDtypeStruct((dec_b, V_WIDTH), F32)],
        compiler_params=pltpu.CompilerParams(dimension_semantics=("arbitrary",),
                                             vmem_limit_bytes=VMEM_LIMIT),
        name="gla_decode",
    )(q_s, k_s, a_s, v_s, state_gla[0])

    x1_s = pl.pallas_call(
        _even_decode_out_kernel,
        out_shape=jax.ShapeDtypeStruct((dec_b, D_MODEL), F32),
        compiler_params=single,
        name="even_decode_out",
    )(xs, o_s, sg_s, ysc_s, gng, wout_a)

    ob = DECODE_ODD_BLOCK
    assert dec_b % ob == 0
    rows_spec = pl.BlockSpec((ob, D_MODEL), lambda i: (i, 0))
    hist_spec = pl.BlockSpec((CCONV_K - 1, ob, CONV_W), lambda i: (0, i, 0))
    cbuf = jnp.transpose(state_cconv[0], (1, 0, 2))
    y_s, cconv_t = pl.pallas_call(
        _odd_decode_kernel,
        grid=(dec_b // ob,),
        in_specs=[rows_spec] + odd_weight_specs + [hist_spec],
        out_specs=[rows_spec, hist_spec],
        out_shape=[jax.ShapeDtypeStruct((dec_b, D_MODEL), F32),
                   jax.ShapeDtypeStruct((CCONV_K - 1, dec_b, CONV_W), F32)],
        compiler_params=pltpu.CompilerParams(dimension_semantics=("arbitrary",),
                                             vmem_limit_bytes=VMEM_LIMIT),
        name="odd_decode",
    )(x1_s, *odd_weights, cbuf)
    cconv_s = jnp.transpose(cconv_t, (1, 0, 2))[None]

    return (y_p,
            y_s.reshape(dec_b, 1, D_MODEL),
            gla_p,
            sconv_p,
            cconv_p,
            gla_s.reshape(1, dec_b, HEADS, DK, DV),
            sconv_s.reshape(1, dec_b, SCONV_K - 1, CONV_W),
            cconv_s)
```

```python
import jax
import jax.numpy as jnp
from jax import lax
from jax.experimental import pallas as pl
from jax.experimental.pallas import tpu as pltpu

F32 = jnp.float32
BF16 = jnp.bfloat16

D_MODEL = 1024
HEADS = 4
DK = 128
DV = 256
QK_WIDTH = HEADS * DK
V_WIDTH = HEADS * DV
GATE_RANK = 16
GATE_RANK_PAD = 128
GATE_TEMP_INV = 1.0 / 16.0
CHUNK = 64
CHUNK_SHIFT = 6
SCONV_K = 3
CCONV_K = 31
CONV_W = 1024
RMS_EPS = 1e-6
LN_EPS = 1e-5
Q_SCALE = DK ** -0.5

COL_Q = 0
COL_K = COL_Q + QK_WIDTH
COL_V = COL_K + QK_WIDTH
COL_G = COL_V + V_WIDTH
COL_A_LOW = COL_G + V_WIDTH
COL_HB = COL_A_LOW + GATE_RANK
COL_GATE_B = COL_HB + CONV_W
COL_GATE_C = COL_GATE_B + CONV_W
COL_ZB = COL_GATE_C + CONV_W
MAIN_W = COL_ZB + CONV_W

SUBLANES = 8
LANES = 128
MXU_K = 256
MXU_N = 256
LANE_TILES = CONV_W // LANES
EVEN_TILE = 512
ODD_TILE = 1024
CCONV_HALO = 32
SCONV_HALO = 8
CCONV_TIME_BLOCK = 16
DECODE_STATE_BLOCK = 16
DECODE_ODD_BLOCK = 64
VMEM_LIMIT = 60 * 1024 * 1024


def _dot(a, b):
    return jnp.dot(a, b, preferred_element_type=F32)


def _dot_nt(a, b):
    return lax.dot_general(a, b, (((1,), (1,)), ((), ())), preferred_element_type=F32)


def _dot_tn(a, b):
    return lax.dot_general(a, b, (((0,), (0,)), ((), ())), preferred_element_type=F32)


def _proj(h, wt_ref, lo, hi):
    return _dot_nt(h, wt_ref[lo:hi, :])


def _rmsnorm(x, g):
    ms = jnp.mean(x * x, axis=-1, keepdims=True)
    return x * lax.rsqrt(ms + RMS_EPS) * g


def _gate(x, y):
    return x / (1.0 + jnp.exp(-y))


def _silu(x):
    return _gate(x, x)


def _log_sigmoid(x):
    return -(jnp.maximum(-x, 0.0) + jnp.log(1.0 + jnp.exp(-jnp.abs(x))))


def _log_decay(h, wmain_ref, wup_ref, bup_ref):
    a_low = _proj(h, wmain_ref, COL_A_LOW, COL_A_LOW + GATE_RANK_PAD).astype(BF16)
    logit = _dot(a_low, wup_ref[...]) + bup_ref[...]
    return _log_sigmoid(logit) * GATE_TEMP_INV


def _head_rmsnorm(o, gng):
    ms = jnp.mean(o * o, axis=-1, keepdims=True)
    return o * lax.rsqrt(ms + RMS_EPS) * gng


def _layernorm_act(yc, z, lng, lnb):
    mu = jnp.mean(yc, axis=-1, keepdims=True)
    xc = yc - mu
    var = jnp.mean(xc * xc, axis=-1, keepdims=True)
    yn = xc * lax.rsqrt(var + LN_EPS) * lng + lnb
    return (_silu(yn) * _silu(z)).astype(BF16)


def _short_conv_gate(u, prev1, prev2, gate_b, z_b, wsc_ref):
    y = wsc_ref[2:3, :] * u + wsc_ref[1:2, :] * prev1 + wsc_ref[0:1, :] * prev2
    return gate_b * y * _silu(z_b)


def _even_prompt_kernel(x_ref, ng_ref, wmain_ref, wup_ref, bup_ref, gng_ref, wsc_ref, wout_ref,
                        x1_ref, sgla_ref, sconv_ref,
                        st_ref, ubuf_ref, qe_ref, ke_ref, kd_ref, v_ref, dec_ref, mix_ref, ysc_ref):
    tm = EVEN_TILE
    t = pl.program_id(1)
    last_t = pl.num_programs(1) - 1

    @pl.when(t == 0)
    def _():
        st_ref[...] = jnp.zeros_like(st_ref)
        ubuf_ref[0:SCONV_HALO, :] = jnp.zeros((SCONV_HALO, CONV_W), F32)

    x = x_ref[...]
    h = _rmsnorm(x, ng_ref[...]).astype(BF16)

    def short_conv_group(g0):
        cols = slice(g0, g0 + MXU_N)
        part = lambda c0: _proj(h, wmain_ref, c0 + g0, c0 + g0 + MXU_N)
        u = part(COL_GATE_C) * part(COL_HB)
        ubuf_ref[SCONV_HALO:SCONV_HALO + tm, cols] = u
        y = (wsc_ref[2:3, cols] * u + wsc_ref[1:2, cols] * ubuf_ref[pl.ds(SCONV_HALO - 1, tm), cols]
             + wsc_ref[0:1, cols] * ubuf_ref[pl.ds(SCONV_HALO - 2, tm), cols])
        ysc_ref[:, cols] = (part(COL_GATE_B) * y * _silu(part(COL_ZB))).astype(BF16)

    q = _proj(h, wmain_ref, COL_Q, COL_K) * Q_SCALE
    k = _proj(h, wmain_ref, COL_K, COL_V)
    v_ref[...] = _proj(h, wmain_ref, COL_V, COL_G).astype(BF16)
    a_low = _proj(h, wmain_ref, COL_A_LOW, COL_A_LOW + GATE_RANK_PAD).astype(BF16)
    short_conv_group(0)
    log_a = _log_sigmoid(_dot(a_low, wup_ref[...]) + bup_ref[...]) * GATE_TEMP_INV
    short_conv_group(MXU_N)

    row = lax.broadcasted_iota(jnp.int32, (MXU_K, MXU_K), 0)
    col = lax.broadcasted_iota(jnp.int32, (MXU_K, MXU_K), 1)
    in_chunk_causal = ((row >> CHUNK_SHIFT) == (col >> CHUNK_SHIFT)) & (col <= row)
    tri = jnp.where(in_chunk_causal, 1.0, 0.0).astype(BF16)
    la_hi = log_a.astype(BF16)
    la_lo = (log_a - la_hi.astype(F32)).astype(BF16)
    for sb in range(tm // MXU_K):
        rows = slice(sb * MXU_K, (sb + 1) * MXU_K)
        b_cum = _dot(tri, la_hi[rows, :]) + _dot(tri, la_lo[rows, :])
        b_tot = jnp.concatenate(
            [jnp.broadcast_to(b_cum[(c + 1) * CHUNK - 1:(c + 1) * CHUNK, :], (CHUNK, QK_WIDTH))
             for c in range(MXU_K // CHUNK)], axis=0)
        qe_ref[rows, :] = (q[rows, :] * jnp.exp(b_cum)).astype(BF16)
        ke_ref[rows, :] = (k[rows, :] * jnp.exp(-b_cum)).astype(BF16)
        kd_ref[rows, :] = (k[rows, :] * jnp.exp(b_tot - b_cum)).astype(BF16)
        dec_ref[rows, :] = jnp.exp(b_tot)
        if sb < 2:
            short_conv_group((2 + sb) * MXU_N)
    assert tm // MXU_K >= 2 and CONV_W == 4 * MXU_N

    gng = gng_ref[...]
    for hh in range(HEADS):
        kcols = slice(hh * DK, (hh + 1) * DK)
        vcols = slice(hh * DV, (hh + 1) * DV)
        st = st_ref[hh]
        for sb in range(tm // MXU_K):
            rows = slice(sb * MXU_K, (sb + 1) * MXU_K)
            sc = jnp.where(in_chunk_causal, _dot_nt(qe_ref[rows, kcols], ke_ref[rows, kcols]), 0.0)
            o_intra = _dot(sc.astype(BF16), v_ref[rows, vcols])
            for c in range(MXU_K // CHUNK):
                r0 = sb * MXU_K + c * CHUNK
                crow = slice(r0, r0 + CHUNK)
                o = o_intra[c * CHUNK:(c + 1) * CHUNK, :] + _dot_nt(qe_ref[crow, kcols], st.astype(BF16))
                mix_ref[crow, vcols] = _head_rmsnorm(o, gng)
                dec = dec_ref[r0:r0 + 1, kcols]
                st = st * dec + _dot_tn(v_ref[crow, vcols], kd_ref[crow, kcols])
        st_ref[hh] = st

    o_mix = (mix_ref[...] * _silu(_proj(h, wmain_ref, COL_G, COL_A_LOW))).astype(BF16)

    ubuf_ref[0:SCONV_HALO, :] = ubuf_ref[tm:tm + SCONV_HALO, :]

    out = _dot(o_mix, wout_ref[0:V_WIDTH, :]) + _dot(ysc_ref[...], wout_ref[V_WIDTH:V_WIDTH + CONV_W, :])
    x1_ref[...] = x + out

    @pl.when(t == last_t)
    def _():
        for hh in range(HEADS):
            sgla_ref[hh] = st_ref[hh].T
        sconv_ref[...] = ubuf_ref[pl.ds(SCONV_HALO + tm - (SCONV_K - 1), SCONV_K - 1), :]


def _odd_prompt_kernel(x_ref, ng_ref, win_ref, bin_ref, wdw_ref, bdw_ref, lng_ref, lnb_ref, wout_ref,
                       bout_ref, fng_ref,
                       y_ref, cconv_ref,
                       u3_ref, y3_ref, yc_ref, tail_ref):
    tm = ODD_TILE
    t = pl.program_id(1)
    last_t = pl.num_programs(1) - 1

    @pl.when(t == 0)
    def _():
        u3_ref[0:CCONV_HALO * LANE_TILES, :] = jnp.zeros((CCONV_HALO * LANE_TILES, LANES), F32)

    x = x_ref[...]
    h = _rmsnorm(x, ng_ref[...]).astype(BF16)
    for g0 in range(0, CONV_W, MXU_N):
        cols = slice(g0, g0 + MXU_N)
        gcols = slice(CONV_W + g0, CONV_W + g0 + MXU_N)
        u = _gate(_dot(h, win_ref[:, cols]) + bin_ref[:, cols], _dot(h, win_ref[:, gcols]) + bin_ref[:, gcols])
        tail_ref[:, cols] = u[tm - CCONV_HALO:, :]
        for r8 in range(tm // SUBLANES):
            for c in range(MXU_N // LANES):
                dst = pl.ds((CCONV_HALO + r8 * SUBLANES) * LANE_TILES + g0 // LANES + c, SUBLANES,
                            stride=LANE_TILES)
                u3_ref[dst, :] = u[r8 * SUBLANES:(r8 + 1) * SUBLANES, c * LANES:(c + 1) * LANES]

    @pl.when(t == last_t)
    def _():
        cconv_ref[...] = tail_ref[CCONV_HALO - (CCONV_K - 1):, :]

    base = CCONV_HALO - (CCONV_K - 1)
    tb = CCONV_TIME_BLOCK
    bdw = bdw_ref[...]
    for blk in range(tm // tb):
        acc = jnp.broadcast_to(bdw[None], (tb, LANE_TILES, LANES))
        for j in range(CCONV_K):
            rows = pl.ds((blk * tb + base + j) * LANE_TILES, tb * LANE_TILES)
            w_j = wdw_ref[j * LANE_TILES:(j + 1) * LANE_TILES, :]
            acc = acc + w_j[None] * u3_ref[rows, :].reshape(tb, LANE_TILES, LANES)
        y3_ref[blk * tb * LANE_TILES:(blk + 1) * tb * LANE_TILES, :] = acc.reshape(tb * LANE_TILES, LANES)

    u3_ref[0:CCONV_HALO * LANE_TILES, :] = u3_ref[tm * LANE_TILES:(tm + CCONV_HALO) * LANE_TILES, :]

    for r8 in range(tm // SUBLANES):
        for c in range(LANE_TILES):
            src = pl.ds(r8 * SUBLANES * LANE_TILES + c, SUBLANES, stride=LANE_TILES)
            yc_ref[r8 * SUBLANES:(r8 + 1) * SUBLANES, c * LANES:(c + 1) * LANES] = y3_ref[src, :]

    z = _dot(h, win_ref[:, 2 * CONV_W:3 * CONV_W]) + bin_ref[:, 2 * CONV_W:3 * CONV_W]
    act = _layernorm_act(yc_ref[...], z, lng_ref[...], lnb_ref[...])
    y_ref[...] = _rmsnorm(x + _dot(act, wout_ref[...]) + bout_ref[...], fng_ref[...])


def _even_decode_front_kernel(x_ref, ng_ref, wmain_ref, wup_ref, bup_ref, wsc_ref, sbuf_ref,
                              q_ref, k_ref, a_ref, v_ref, sg_ref, y_ref, snew_ref):
    h = _rmsnorm(x_ref[...], ng_ref[...]).astype(BF16)
    q_ref[...] = _proj(h, wmain_ref, COL_Q, COL_K) * Q_SCALE
    k_ref[...] = _proj(h, wmain_ref, COL_K, COL_V)
    v_ref[...] = _proj(h, wmain_ref, COL_V, COL_G)
    sg_ref[...] = _silu(_proj(h, wmain_ref, COL_G, COL_A_LOW))
    a_ref[...] = jnp.exp(_log_decay(h, wmain_ref, wup_ref, bup_ref))
    u = _proj(h, wmain_ref, COL_GATE_C, COL_ZB) * _proj(h, wmain_ref, COL_HB, COL_GATE_B)
    prev2 = sbuf_ref[:, 0:CONV_W]
    prev1 = sbuf_ref[:, CONV_W:2 * CONV_W]
    y_ref[...] = _short_conv_gate(u, prev1, prev2, _proj(h, wmain_ref, COL_GATE_B, COL_GATE_C),
                                  _proj(h, wmain_ref, COL_ZB, MAIN_W), wsc_ref)
    snew_ref[:, 0:CONV_W] = prev1
    snew_ref[:, CONV_W:2 * CONV_W] = u


def _lane_bcast_column(row):
    return jnp.broadcast_to(row, (DK, DK)).T


def _gla_decode_kernel(q_ref, k_ref, a_ref, v_ref, s_ref, snew_ref, o_ref):
    for b in range(DECODE_STATE_BLOCK):
        for hh in range(HEADS):
            kcols = slice(hh * DK, (hh + 1) * DK)
            vcols = slice(hh * DV, (hh + 1) * DV)
            a_col = _lane_bcast_column(a_ref[b:b + 1, kcols])
            k_col = _lane_bcast_column(k_ref[b:b + 1, kcols])
            q_col = _lane_bcast_column(q_ref[b:b + 1, kcols])
            v_row = v_ref[b:b + 1, vcols]
            halves = []
            for half in range(DV // DK):
                lanes = slice(half * DK, (half + 1) * DK)
                s_new = a_col * s_ref[b, hh, :, lanes] + k_col * v_row[:, lanes]
                snew_ref[b, hh, :, lanes] = s_new
                halves.append(jnp.sum(q_col * s_new, axis=0, keepdims=True))
            o_ref[b:b + 1, vcols] = jnp.concatenate(halves, axis=1)


def _even_decode_out_kernel(x_ref, o_ref, sg_ref, y_ref, gng_ref, wout_ref, x1_ref):
    gng = gng_ref[...]
    parts = [_head_rmsnorm(o_ref[:, hh * DV:(hh + 1) * DV], gng) for hh in range(HEADS)]
    o_mix = (jnp.concatenate(parts, axis=1) * sg_ref[...]).astype(BF16)
    out = _dot(o_mix, wout_ref[0:V_WIDTH, :]) + _dot(y_ref[...].astype(BF16), wout_ref[V_WIDTH:V_WIDTH + CONV_W, :])
    x1_ref[...] = x_ref[...] + out


def _odd_decode_kernel(x_ref, ng_ref, win_ref, bin_ref, wdw_ref, bdw_ref, lng_ref, lnb_ref, wout_ref,
                       bout_ref, fng_ref, cbuf_ref,
                       y_ref, cnew_ref):
    n_hist = CCONV_K - 1
    x = x_ref[...]
    h = _rmsnorm(x, ng_ref[...]).astype(BF16)
    a = _dot(h, win_ref[:, 0:CONV_W]) + bin_ref[:, 0:CONV_W]
    a_gate = _dot(h, win_ref[:, CONV_W:2 * CONV_W]) + bin_ref[:, CONV_W:2 * CONV_W]
    u = _gate(a, a_gate)
    yc = bdw_ref[...] + wdw_ref[n_hist:CCONV_K, :] * u
    for j in range(n_hist):
        tap = cbuf_ref[j]
        yc = yc + wdw_ref[j:j + 1, :] * tap
        if j >= 1:
            cnew_ref[j - 1] = tap
    cnew_ref[n_hist - 1] = u
    z = _dot(h, win_ref[:, 2 * CONV_W:3 * CONV_W]) + bin_ref[:, 2 * CONV_W:3 * CONV_W]
    act = _layernorm_act(yc, z, lng_ref[...], lnb_ref[...])
    y_ref[...] = _rmsnorm(x + _dot(act, wout_ref[...]) + bout_ref[...], fng_ref[...])


def _const_spec(shape):
    nd = len(shape)
    return pl.BlockSpec(shape, lambda *_: (0,) * nd, pipeline_mode=pl.Buffered(1))


def _row(v):
    return v.reshape(1, -1)


def kernel(x_prompt, x_sample, state_gla, state_sconv, state_cconv, norm_g, w_in_a, w_gate_up, b_gate_up, gla_norm_g, w_sconv, w_out_a, w_in_c, b_in_c, w_dwconv, b_dwconv, ln_g, ln_b, w_out_c, b_out_c, final_norm_g):
    bsz, seq, d = x_prompt.shape
    dec_b = x_sample.shape[0]
    assert d == D_MODEL and seq % EVEN_TILE == 0 and seq % ODD_TILE == 0 and x_sample.shape[1] == 1
    assert w_in_a.shape[0] == 1 and w_in_c.shape[0] == 1 and norm_g.shape[0] == 2

    assert w_in_a.shape[2] == MAIN_W
    wmain = w_in_a[0].T.astype(BF16)
    wup = jnp.pad(w_gate_up[0], ((0, GATE_RANK_PAD - GATE_RANK), (0, 0))).astype(BF16)
    bup = _row(b_gate_up[0])
    gng = _row(gla_norm_g[0])
    wsc = w_sconv[0]
    wout_a = w_out_a[0].astype(BF16)
    ng0 = _row(norm_g[0])
    ng1 = _row(norm_g[1])
    win_c = w_in_c[0].astype(BF16)
    bin_c = _row(b_in_c[0])
    wdw = w_dwconv[0]
    bdw = _row(b_dwconv[0])
    lng = _row(ln_g[0])
    lnb = _row(ln_b[0])
    wout_c = w_out_c[0].astype(BF16)
    bout = _row(b_out_c[0])
    fng = _row(final_norm_g)

    even_weights = (ng0, wmain, wup, bup)
    even_prompt_weights = even_weights + (gng, wsc, wout_a)
    odd_weights = (ng1, win_c, bin_c, wdw, bdw, lng, lnb, wout_c, bout, fng)
    odd_weight_specs = [_const_spec(w.shape) for w in odd_weights]
    wdw3 = wdw.reshape(CCONV_K * LANE_TILES, LANES)
    bdw3 = b_dwconv[0].reshape(LANE_TILES, LANES)
    odd_prompt_weights = (ng1, win_c, bin_c, wdw3, bdw3, lng, lnb, wout_c, bout, fng)

    prompt_params = pltpu.CompilerParams(dimension_semantics=("arbitrary", "arbitrary"),
                                         vmem_limit_bytes=VMEM_LIMIT)

    tm = EVEN_TILE
    tile_spec = pl.BlockSpec((None, tm, D_MODEL), lambda b, t: (b, t, 0))
    x1_p, gla_p, sconv_p = pl.pallas_call(
        _even_prompt_kernel,
        grid=(bsz, seq // tm),
        in_specs=[tile_spec] + [_const_spec(w.shape) for w in even_prompt_weights],
        out_specs=[tile_spec,
                   pl.BlockSpec((None, None, HEADS, DK, DV), lambda b, t: (0, b, 0, 0, 0)),
                   pl.BlockSpec((None, None, SCONV_K - 1, CONV_W), lambda b, t: (0, b, 0, 0))],
        out_shape=[jax.ShapeDtypeStruct((bsz, seq, D_MODEL), F32),
                   jax.ShapeDtypeStruct((1, bsz, HEADS, DK, DV), F32),
                   jax.ShapeDtypeStruct((1, bsz, SCONV_K - 1, CONV_W), F32)],
        scratch_shapes=[pltpu.VMEM((HEADS, DV, DK), F32),
                        pltpu.VMEM((tm + SCONV_HALO, CONV_W), F32),
                        pltpu.VMEM((tm, QK_WIDTH), BF16),
                        pltpu.VMEM((tm, QK_WIDTH), BF16),
                        pltpu.VMEM((tm, QK_WIDTH), BF16),
                        pltpu.VMEM((tm, V_WIDTH), BF16),
                        pltpu.VMEM((tm, QK_WIDTH), F32),
                        pltpu.VMEM((tm, V_WIDTH), F32),
                        pltpu.VMEM((tm, CONV_W), BF16)],
        compiler_params=prompt_params,
        name="even_prompt",
    )(x_prompt, *even_prompt_weights)

    tm = ODD_TILE
    tile_spec = pl.BlockSpec((None, tm, D_MODEL), lambda b, t: (b, t, 0))
    y_p, cconv_p = pl.pallas_call(
        _odd_prompt_kernel,
        grid=(bsz, seq // tm),
        in_specs=[tile_spec] + [_const_spec(w.shape) for w in odd_prompt_weights],
        out_specs=[tile_spec,
                   pl.BlockSpec((None, None, CCONV_K - 1, CONV_W), lambda b, t: (0, b, 0, 0))],
        out_shape=[jax.ShapeDtypeStruct((bsz, seq, D_MODEL), F32),
                   jax.ShapeDtypeStruct((1, bsz, CCONV_K - 1, CONV_W), F32)],
        scratch_shapes=[pltpu.VMEM(((tm + CCONV_HALO) * LANE_TILES, LANES), F32),
                        pltpu.VMEM((tm * LANE_TILES, LANES), F32),
                        pltpu.VMEM((tm, CONV_W), F32),
                        pltpu.VMEM((CCONV_HALO, CONV_W), F32)],
        compiler_params=prompt_params,
        name="odd_prompt",
    )(x1_p, *odd_prompt_weights)

    xs = x_sample.reshape(dec_b, D_MODEL)
    sbuf = state_sconv.reshape(dec_b, (SCONV_K - 1) * CONV_W)
    single = pltpu.CompilerParams(vmem_limit_bytes=VMEM_LIMIT)
    q_s, k_s, a_s, v_s, sg_s, ysc_s, sconv_s = pl.pallas_call(
        _even_decode_front_kernel,
        out_shape=[jax.ShapeDtypeStruct((dec_b, QK_WIDTH), F32),
                   jax.ShapeDtypeStruct((dec_b, QK_WIDTH), F32),
                   jax.ShapeDtypeStruct((dec_b, QK_WIDTH), F32),
                   jax.ShapeDtypeStruct((dec_b, V_WIDTH), F32),
                   jax.ShapeDtypeStruct((dec_b, V_WIDTH), F32),
                   jax.ShapeDtypeStruct((dec_b, CONV_W), F32),
                   jax.ShapeDtypeStruct((dec_b, (SCONV_K - 1) * CONV_W), F32)],
        compiler_params=single,
        name="even_decode_front",
    )(xs, *even_weights, wsc, sbuf)

    sb = DECODE_STATE_BLOCK
    assert dec_b % sb == 0
    vec_spec = lambda w: pl.BlockSpec((sb, w), lambda i: (i, 0))
    state_spec = pl.BlockSpec((sb, HEADS, DK, DV), lambda i: (i, 0, 0, 0))
    gla_s, o_s = pl.pallas_call(
        _gla_decode_kernel,
        grid=(dec_b // sb,),
        in_specs=[vec_spec(QK_WIDTH), vec_spec(QK_WIDTH), vec_spec(QK_WIDTH), vec_spec(V_WIDTH), state_spec],
        out_specs=[state_spec, vec_spec(V_WIDTH)],
        out_shape=[jax.ShapeDtypeStruct((dec_b, HEADS, DK, DV), F32),
                   jax.ShapeDtypeStruct((dec_b, V_WIDTH), F32)],
        compiler_params=pltpu.CompilerParams(dimension_semantics=("arbitrary",),
                                             vmem_limit_bytes=VMEM_LIMIT),
        name="gla_decode",
    )(q_s, k_s, a_s, v_s, state_gla[0])

    x1_s = pl.pallas_call(
        _even_decode_out_kernel,
        out_shape=jax.ShapeDtypeStruct((dec_b, D_MODEL), F32),
        compiler_params=single,
        name="even_decode_out",
    )(xs, o_s, sg_s, ysc_s, gng, wout_a)

    ob = DECODE_ODD_BLOCK
    assert dec_b % ob == 0
    rows_spec = pl.BlockSpec((ob, D_MODEL), lambda i: (i, 0))
    hist_spec = pl.BlockSpec((CCONV_K - 1, ob, CONV_W), lambda i: (0, i, 0))
    cbuf = jnp.transpose(state_cconv[0], (1, 0, 2))
    y_s, cconv_t = pl.pallas_call(
        _odd_decode_kernel,
        grid=(dec_b // ob,),
        in_specs=[rows_spec] + odd_weight_specs + [hist_spec],
        out_specs=[rows_spec, hist_spec],
        out_shape=[jax.ShapeDtypeStruct((dec_b, D_MODEL), F32),
                   jax.ShapeDtypeStruct((CCONV_K - 1, dec_b, CONV_W), F32)],
        compiler_params=pltpu.CompilerParams(dimension_semantics=("arbitrary",),
                                             vmem_limit_bytes=VMEM_LIMIT),
        name="odd_decode",
    )(x1_s, *odd_weights, cbuf)
    cconv_s = jnp.transpose(cconv_t, (1, 0, 2))[None]

    return (y_p,
            y_s.reshape(dec_b, 1, D_MODEL),
            gla_p,
            sconv_p,
            cconv_p,
            gla_s.reshape(1, dec_b, HEADS, DK, DV),
            sconv_s.reshape(1, dec_b, SCONV_K - 1, CONV_W),
            cconv_s)
```

```python
import jax
import jax.numpy as jnp
import numpy as np
from jax import lax
from jax.experimental import pallas as pl
from jax.experimental.pallas import tpu as pltpu

F32 = jnp.float32
BF16 = jnp.bfloat16

D_MODEL = 1024
HEADS = 4
DK = 128
DV = 256
QK_WIDTH = HEADS * DK
V_WIDTH = HEADS * DV
GATE_RANK = 16
GATE_RANK_PAD = 128
GATE_TEMP_INV = 1.0 / 16.0
CHUNK = 64
CHUNK_SHIFT = 6
SCONV_K = 3
CCONV_K = 31
CONV_W = 1024
RMS_EPS = 1e-6
LN_EPS = 1e-5
Q_SCALE = DK ** -0.5

COL_Q = 0
COL_K = COL_Q + QK_WIDTH
COL_V = COL_K + QK_WIDTH
COL_G = COL_V + V_WIDTH
COL_A_LOW = COL_G + V_WIDTH
COL_HB = COL_A_LOW + GATE_RANK
COL_GATE_B = COL_HB + CONV_W
COL_GATE_C = COL_GATE_B + CONV_W
COL_ZB = COL_GATE_C + CONV_W
MAIN_W = COL_ZB + CONV_W

MXU_K = 256
MXU_N = 256
EVEN_TILE = 512
ODD_TILE = 1024
CCONV_HALO = 32
SCONV_HALO = 8
DFT_HOP = 256
DFT_N = DFT_HOP + CCONV_HALO
DFT_BINS = 152
DECODE_STATE_BLOCK = 16
DECODE_ODD_BLOCK = 64
VMEM_LIMIT = 60 * 1024 * 1024


def _dot(a, b):
    return jnp.dot(a, b, preferred_element_type=F32)


def _dot_nt(a, b):
    return lax.dot_general(a, b, (((1,), (1,)), ((), ())), preferred_element_type=F32)


def _dot_tn(a, b):
    return lax.dot_general(a, b, (((0,), (0,)), ((), ())), preferred_element_type=F32)


def _proj(h, wt_ref, lo, hi):
    return _dot_nt(h, wt_ref[lo:hi, :])


def _rmsnorm(x, g):
    ms = jnp.mean(x * x, axis=-1, keepdims=True)
    return x * lax.rsqrt(ms + RMS_EPS) * g


def _gate(x, y):
    return x / (1.0 + jnp.exp(-y))


def _silu(x):
    return _gate(x, x)


def _log_sigmoid(x):
    return -(jnp.maximum(-x, 0.0) + jnp.log(1.0 + jnp.exp(-jnp.abs(x))))


def _log_decay(h, wmain_ref, wup_ref, bup_ref):
    a_low = _proj(h, wmain_ref, COL_A_LOW, COL_A_LOW + GATE_RANK_PAD).astype(BF16)
    logit = _dot(a_low, wup_ref[...]) + bup_ref[...]
    return _log_sigmoid(logit) * GATE_TEMP_INV


def _head_rmsnorm(o, gng):
    ms = jnp.mean(o * o, axis=-1, keepdims=True)
    return o * lax.rsqrt(ms + RMS_EPS) * gng


def _layernorm_act(yc, z, lng, lnb):
    mu = jnp.mean(yc, axis=-1, keepdims=True)
    xc = yc - mu
    var = jnp.mean(xc * xc, axis=-1, keepdims=True)
    yn = xc * lax.rsqrt(var + LN_EPS) * lng + lnb
    return (_silu(yn) * _silu(z)).astype(BF16)


def _short_conv_gate(u, prev1, prev2, gate_b, z_b, wsc_ref):
    y = wsc_ref[2:3, :] * u + wsc_ref[1:2, :] * prev1 + wsc_ref[0:1, :] * prev2
    return gate_b * y * _silu(z_b)


def _even_prompt_kernel(x_ref, ng_ref, wmain_ref, wup_ref, bup_ref, gng_ref, wsc_ref, wout_ref,
                        x1_ref, sgla_ref, sconv_ref,
                        st_ref, ubuf_ref, qe_ref, ke_ref, kd_ref, v_ref, dec_ref, mix_ref, ysc_ref):
    tm = EVEN_TILE
    t = pl.program_id(1)
    last_t = pl.num_programs(1) - 1

    @pl.when(t == 0)
    def _():
        st_ref[...] = jnp.zeros_like(st_ref)
        ubuf_ref[0:SCONV_HALO, :] = jnp.zeros((SCONV_HALO, CONV_W), F32)

    x = x_ref[...]
    h = _rmsnorm(x, ng_ref[...]).astype(BF16)

    def short_conv_group(g0):
        cols = slice(g0, g0 + MXU_N)
        part = lambda c0: _proj(h, wmain_ref, c0 + g0, c0 + g0 + MXU_N)
        u = part(COL_GATE_C) * part(COL_HB)
        ubuf_ref[SCONV_HALO:SCONV_HALO + tm, cols] = u
        y = (wsc_ref[2:3, cols] * u + wsc_ref[1:2, cols] * ubuf_ref[pl.ds(SCONV_HALO - 1, tm), cols]
             + wsc_ref[0:1, cols] * ubuf_ref[pl.ds(SCONV_HALO - 2, tm), cols])
        ysc_ref[:, cols] = (part(COL_GATE_B) * y * _silu(part(COL_ZB))).astype(BF16)

    q = _proj(h, wmain_ref, COL_Q, COL_K) * Q_SCALE
    k = _proj(h, wmain_ref, COL_K, COL_V)
    v_ref[...] = _proj(h, wmain_ref, COL_V, COL_G).astype(BF16)
    a_low = _proj(h, wmain_ref, COL_A_LOW, COL_A_LOW + GATE_RANK_PAD).astype(BF16)
    short_conv_group(0)
    log_a = _log_sigmoid(_dot(a_low, wup_ref[...]) + bup_ref[...]) * GATE_TEMP_INV
    short_conv_group(MXU_N)

    row = lax.broadcasted_iota(jnp.int32, (MXU_K, MXU_K), 0)
    col = lax.broadcasted_iota(jnp.int32, (MXU_K, MXU_K), 1)
    in_chunk_causal = ((row >> CHUNK_SHIFT) == (col >> CHUNK_SHIFT)) & (col <= row)
    tri = jnp.where(in_chunk_causal, 1.0, 0.0).astype(BF16)
    la_hi = log_a.astype(BF16)
    la_lo = (log_a - la_hi.astype(F32)).astype(BF16)
    for sb in range(tm // MXU_K):
        rows = slice(sb * MXU_K, (sb + 1) * MXU_K)
        b_cum = _dot(tri, la_hi[rows, :]) + _dot(tri, la_lo[rows, :])
        b_tot = jnp.concatenate(
            [jnp.broadcast_to(b_cum[(c + 1) * CHUNK - 1:(c + 1) * CHUNK, :], (CHUNK, QK_WIDTH))
             for c in range(MXU_K // CHUNK)], axis=0)
        qe_ref[rows, :] = (q[rows, :] * jnp.exp(b_cum)).astype(BF16)
        ke_ref[rows, :] = (k[rows, :] * jnp.exp(-b_cum)).astype(BF16)
        kd_ref[rows, :] = (k[rows, :] * jnp.exp(b_tot - b_cum)).astype(BF16)
        dec_ref[rows, :] = jnp.exp(b_tot)
        if sb < 2:
            short_conv_group((2 + sb) * MXU_N)

    gng = gng_ref[...]
    for hh in range(HEADS):
        kcols = slice(hh * DK, (hh + 1) * DK)
        vcols = slice(hh * DV, (hh + 1) * DV)
        st = st_ref[hh]
        for sb in range(tm // MXU_K):
            rows = slice(sb * MXU_K, (sb + 1) * MXU_K)
            sc = jnp.where(in_chunk_causal, _dot_nt(qe_ref[rows, kcols], ke_ref[rows, kcols]), 0.0)
            o_intra = _dot(sc.astype(BF16), v_ref[rows, vcols])
            for c in range(MXU_K // CHUNK):
                r0 = sb * MXU_K + c * CHUNK
                crow = slice(r0, r0 + CHUNK)
                o = o_intra[c * CHUNK:(c + 1) * CHUNK, :] + _dot_nt(qe_ref[crow, kcols], st.astype(BF16))
                mix_ref[crow, vcols] = _head_rmsnorm(o, gng)
                dec = dec_ref[r0:r0 + 1, kcols]
                st = st * dec + _dot_tn(v_ref[crow, vcols], kd_ref[crow, kcols])
        st_ref[hh] = st

    o_mix = (mix_ref[...] * _silu(_proj(h, wmain_ref, COL_G, COL_A_LOW))).astype(BF16)

    ubuf_ref[0:SCONV_HALO, :] = ubuf_ref[tm:tm + SCONV_HALO, :]

    out = _dot(o_mix, wout_ref[0:V_WIDTH, :]) + _dot(ysc_ref[...], wout_ref[V_WIDTH:V_WIDTH + CONV_W, :])
    x1_ref[...] = x + out

    @pl.when(t == last_t)
    def _():
        for hh in range(HEADS):
            sgla_ref[hh] = st_ref[hh].T
        sconv_ref[...] = ubuf_ref[pl.ds(SCONV_HALO + tm - (SCONV_K - 1), SCONV_K - 1), :]


def _odd_prompt_kernel(x_ref, ng_ref, win_ref, bin_ref, bdw_ref, lng_ref, lnb_ref, wout_ref,
                       bout_ref, fng_ref, fwd_ref, inv_ref, hr_ref, hi_ref,
                       y_ref, cconv_ref,
                       ub_ref, yc_ref, tail_ref):
    tm = ODD_TILE
    t = pl.program_id(1)
    last_t = pl.num_programs(1) - 1

    @pl.when(t == 0)
    def _():
        ub_ref[0:CCONV_HALO, :] = jnp.zeros((CCONV_HALO, CONV_W), BF16)

    x = x_ref[...]
    h = _rmsnorm(x, ng_ref[...]).astype(BF16)
    for g0 in range(0, CONV_W, MXU_N):
        cols = slice(g0, g0 + MXU_N)
        gcols = slice(CONV_W + g0, CONV_W + g0 + MXU_N)
        u = _gate(_dot(h, win_ref[:, cols]) + bin_ref[:, cols], _dot(h, win_ref[:, gcols]) + bin_ref[:, gcols])
        tail_ref[:, cols] = u[tm - CCONV_HALO:, :]
        ub_ref[CCONV_HALO:CCONV_HALO + tm, cols] = u.astype(BF16)

    @pl.when(t == last_t)
    def _():
        cconv_ref[...] = tail_ref[CCONV_HALO - (CCONV_K - 1):, :]

    bdw = bdw_ref[...]
    hr = hr_ref[...]
    hi = hi_ref[...]
    fwd = fwd_ref[...].astype(BF16)
    inv = inv_ref[...].astype(BF16)
    for blk in range(tm // DFT_HOP):
        xr_xi = _dot(fwd, ub_ref[blk * DFT_HOP:blk * DFT_HOP + DFT_N, :])
        xr = xr_xi[0:DFT_BINS, :]
        xi = xr_xi[DFT_BINS:2 * DFT_BINS, :]
        yr_yi = jnp.concatenate([xr * hr - xi * hi, xr * hi + xi * hr], axis=0).astype(BF16)
        yc_ref[blk * DFT_HOP:(blk + 1) * DFT_HOP, :] = _dot(inv, yr_yi) + bdw

    ub_ref[0:CCONV_HALO, :] = ub_ref[tm:tm + CCONV_HALO, :]

    z = _dot(h, win_ref[:, 2 * CONV_W:3 * CONV_W]) + bin_ref[:, 2 * CONV_W:3 * CONV_W]
    act = _layernorm_act(yc_ref[...], z, lng_ref[...], lnb_ref[...])
    y_ref[...] = _rmsnorm(x + _dot(act, wout_ref[...]) + bout_ref[...], fng_ref[...])


def _even_decode_front_kernel(x_ref, ng_ref, wmain_ref, wup_ref, bup_ref, wsc_ref, sbuf_ref,
                              q_ref, k_ref, a_ref, v_ref, sg_ref, y_ref, snew_ref):
    h = _rmsnorm(x_ref[...], ng_ref[...]).astype(BF16)
    q_ref[...] = _proj(h, wmain_ref, COL_Q, COL_K) * Q_SCALE
    k_ref[...] = _proj(h, wmain_ref, COL_K, COL_V)
    v_ref[...] = _proj(h, wmain_ref, COL_V, COL_G)
    sg_ref[...] = _silu(_proj(h, wmain_ref, COL_G, COL_A_LOW))
    a_ref[...] = jnp.exp(_log_decay(h, wmain_ref, wup_ref, bup_ref))
    u = _proj(h, wmain_ref, COL_GATE_C, COL_ZB) * _proj(h, wmain_ref, COL_HB, COL_GATE_B)
    prev2 = sbuf_ref[:, 0:CONV_W]
    prev1 = sbuf_ref[:, CONV_W:2 * CONV_W]
    y_ref[...] = _short_conv_gate(u, prev1, prev2, _proj(h, wmain_ref, COL_GATE_B, COL_GATE_C),
                                  _proj(h, wmain_ref, COL_ZB, MAIN_W), wsc_ref)
    snew_ref[:, 0:CONV_W] = prev1
    snew_ref[:, CONV_W:2 * CONV_W] = u


def _lane_bcast_column(row):
    return jnp.broadcast_to(row, (DK, DK)).T


def _gla_decode_kernel(q_ref, k_ref, a_ref, v_ref, s_ref, snew_ref, o_ref):
    for b in range(DECODE_STATE_BLOCK):
        for hh in range(HEADS):
            kcols = slice(hh * DK, (hh + 1) * DK)
            vcols = slice(hh * DV, (hh + 1) * DV)
            a_col = _lane_bcast_column(a_ref[b:b + 1, kcols])
            k_col = _lane_bcast_column(k_ref[b:b + 1, kcols])
            q_col = _lane_bcast_column(q_ref[b:b + 1, kcols])
            v_row = v_ref[b:b + 1, vcols]
            halves = []
            for half in range(DV // DK):
                lanes = slice(half * DK, (half + 1) * DK)
                s_new = a_col * s_ref[b, hh, :, lanes] + k_col * v_row[:, lanes]
                snew_ref[b, hh, :, lanes] = s_new
                halves.append(jnp.sum(q_col * s_new, axis=0, keepdims=True))
            o_ref[b:b + 1, vcols] = jnp.concatenate(halves, axis=1)


def _even_decode_out_kernel(x_ref, o_ref, sg_ref, y_ref, gng_ref, wout_ref, x1_ref):
    gng = gng_ref[...]
    parts = [_head_rmsnorm(o_ref[:, hh * DV:(hh + 1) * DV], gng) for hh in range(HEADS)]
    o_mix = (jnp.concatenate(parts, axis=1) * sg_ref[...]).astype(BF16)
    out = _dot(o_mix, wout_ref[0:V_WIDTH, :]) + _dot(y_ref[...].astype(BF16), wout_ref[V_WIDTH:V_WIDTH + CONV_W, :])
    x1_ref[...] = x_ref[...] + out


def _odd_decode_kernel(x_ref, ng_ref, win_ref, bin_ref, wdw_ref, bdw_ref, lng_ref, lnb_ref, wout_ref,
                       bout_ref, fng_ref, cbuf_ref,
                       y_ref, cnew_ref):
    n_hist = CCONV_K - 1
    x = x_ref[...]
    h = _rmsnorm(x, ng_ref[...]).astype(BF16)
    a = _dot(h, win_ref[:, 0:CONV_W]) + bin_ref[:, 0:CONV_W]
    a_gate = _dot(h, win_ref[:, CONV_W:2 * CONV_W]) + bin_ref[:, CONV_W:2 * CONV_W]
    u = _gate(a, a_gate)
    yc = bdw_ref[...] + wdw_ref[n_hist:CCONV_K, :] * u
    for j in range(n_hist):
        tap = cbuf_ref[j]
        yc = yc + wdw_ref[j:j + 1, :] * tap
        if j >= 1:
            cnew_ref[j - 1] = tap
    cnew_ref[n_hist - 1] = u
    z = _dot(h, win_ref[:, 2 * CONV_W:3 * CONV_W]) + bin_ref[:, 2 * CONV_W:3 * CONV_W]
    act = _layernorm_act(yc, z, lng_ref[...], lnb_ref[...])
    y_ref[...] = _rmsnorm(x + _dot(act, wout_ref[...]) + bout_ref[...], fng_ref[...])


def _const_spec(shape):
    nd = len(shape)
    return pl.BlockSpec(shape, lambda *_: (0,) * nd, pipeline_mode=pl.Buffered(1))


def _row(v):
    return v.reshape(1, -1)


def kernel(x_prompt, x_sample, state_gla, state_sconv, state_cconv, norm_g, w_in_a, w_gate_up, b_gate_up, gla_norm_g, w_sconv, w_out_a, w_in_c, b_in_c, w_dwconv, b_dwconv, ln_g, ln_b, w_out_c, b_out_c, final_norm_g):
    bsz, seq, d = x_prompt.shape
    dec_b = x_sample.shape[0]
    assert d == D_MODEL and seq % EVEN_TILE == 0 and seq % ODD_TILE == 0 and x_sample.shape[1] == 1
    assert EVEN_TILE // MXU_K >= 2 and CONV_W == 4 * MXU_N and ODD_TILE % DFT_HOP == 0
    assert w_in_a.shape[0] == 1 and w_in_c.shape[0] == 1 and norm_g.shape[0] == 2

    assert w_in_a.shape[2] == MAIN_W
    wmain = w_in_a[0].T.astype(BF16)
    wup = jnp.pad(w_gate_up[0], ((0, GATE_RANK_PAD - GATE_RANK), (0, 0))).astype(BF16)
    bup = _row(b_gate_up[0])
    gng = _row(gla_norm_g[0])
    wsc = w_sconv[0]
    wout_a = w_out_a[0].astype(BF16)
    ng0 = _row(norm_g[0])
    ng1 = _row(norm_g[1])
    win_c = w_in_c[0].astype(BF16)
    bin_c = _row(b_in_c[0])
    wdw = w_dwconv[0]
    bdw = _row(b_dwconv[0])
    lng = _row(ln_g[0])
    lnb = _row(ln_b[0])
    wout_c = w_out_c[0].astype(BF16)
    bout = _row(b_out_c[0])
    fng = _row(final_norm_g)

    even_weights = (ng0, wmain, wup, bup)
    even_prompt_weights = even_weights + (gng, wsc, wout_a)
    odd_weights = (ng1, win_c, bin_c, wdw, bdw, lng, lnb, wout_c, bout, fng)
    odd_weight_specs = [_const_spec(w.shape) for w in odd_weights]
    n_bins = DFT_N // 2 + 1
    assert DFT_N % 2 == 0 and n_bins <= DFT_BINS and CCONV_HALO >= CCONV_K - 1
    kk = np.arange(n_bins)[:, None]
    ang_f = 2.0 * np.pi * kk * np.arange(DFT_N)[None, :] / DFT_N
    fwd = np.zeros((2 * DFT_BINS, DFT_N), np.float32)
    fwd[0:n_bins] = np.cos(ang_f)
    fwd[DFT_BINS:DFT_BINS + n_bins] = -np.sin(ang_f)
    alpha = np.where((kk == 0) | (kk == DFT_N // 2), 1.0, 2.0) / DFT_N
    ang_i = 2.0 * np.pi * kk * np.arange(CCONV_HALO, DFT_N)[None, :] / DFT_N
    inv = np.zeros((DFT_HOP, 2 * DFT_BINS), np.float32)
    inv[:, 0:n_bins] = (alpha * np.cos(ang_i)).T
    inv[:, DFT_BINS:DFT_BINS + n_bins] = (-alpha * np.sin(ang_i)).T
    ang_h = 2.0 * np.pi * kk * np.arange(CCONV_K)[None, :] / DFT_N
    resp = np.zeros((2 * DFT_BINS, CCONV_K), np.float32)
    resp[0:n_bins] = np.cos(ang_h)
    resp[DFT_BINS:DFT_BINS + n_bins] = -np.sin(ang_h)
    h_resp = jnp.dot(jnp.asarray(resp), wdw[::-1], precision=lax.Precision.HIGHEST)
    odd_prompt_weights = (ng1, win_c, bin_c, bdw, lng, lnb, wout_c, bout, fng,
                          jnp.asarray(fwd), jnp.asarray(inv), h_resp[:DFT_BINS], h_resp[DFT_BINS:])

    prompt_params = pltpu.CompilerParams(dimension_semantics=("arbitrary", "arbitrary"),
                                         vmem_limit_bytes=VMEM_LIMIT)

    tm = EVEN_TILE
    tile_spec = pl.BlockSpec((None, tm, D_MODEL), lambda b, t: (b, t, 0))
    x1_p, gla_p, sconv_p = pl.pallas_call(
        _even_prompt_kernel,
        grid=(bsz, seq // tm),
        in_specs=[tile_spec] + [_const_spec(w.shape) for w in even_prompt_weights],
        out_specs=[tile_spec,
                   pl.BlockSpec((None, None, HEADS, DK, DV), lambda b, t: (0, b, 0, 0, 0)),
                   pl.BlockSpec((None, None, SCONV_K - 1, CONV_W), lambda b, t: (0, b, 0, 0))],
        out_shape=[jax.ShapeDtypeStruct((bsz, seq, D_MODEL), F32),
                   jax.ShapeDtypeStruct((1, bsz, HEADS, DK, DV), F32),
                   jax.ShapeDtypeStruct((1, bsz, SCONV_K - 1, CONV_W), F32)],
        scratch_shapes=[pltpu.VMEM((HEADS, DV, DK), F32),
                        pltpu.VMEM((tm + SCONV_HALO, CONV_W), F32),
                        pltpu.VMEM((tm, QK_WIDTH), BF16),
                        pltpu.VMEM((tm, QK_WIDTH), BF16),
                        pltpu.VMEM((tm, QK_WIDTH), BF16),
                        pltpu.VMEM((tm, V_WIDTH), BF16),
                        pltpu.VMEM((tm, QK_WIDTH), F32),
                        pltpu.VMEM((tm, V_WIDTH), F32),
                        pltpu.VMEM((tm, CONV_W), BF16)],
        compiler_params=prompt_params,
        name="even_prompt",
    )(x_prompt, *even_prompt_weights)

    tm = ODD_TILE
    tile_spec = pl.BlockSpec((None, tm, D_MODEL), lambda b, t: (b, t, 0))
    y_p, cconv_p = pl.pallas_call(
        _odd_prompt_kernel,
        grid=(bsz, seq // tm),
        in_specs=[tile_spec] + [_const_spec(w.shape) for w in odd_prompt_weights],
        out_specs=[tile_spec,
                   pl.BlockSpec((None, None, CCONV_K - 1, CONV_W), lambda b, t: (0, b, 0, 0))],
        out_shape=[jax.ShapeDtypeStruct((bsz, seq, D_MODEL), F32),
                   jax.ShapeDtypeStruct((1, bsz, CCONV_K - 1, CONV_W), F32)],
        scratch_shapes=[pltpu.VMEM((tm + CCONV_HALO, CONV_W), BF16),
                        pltpu.VMEM((tm, CONV_W), F32),
                        pltpu.VMEM((CCONV_HALO, CONV_W), F32)],
        compiler_params=prompt_params,
        name="odd_prompt",
    )(x1_p, *odd_prompt_weights)

    xs = x_sample.reshape(dec_b, D_MODEL)
    sbuf = state_sconv.reshape(dec_b, (SCONV_K - 1) * CONV_W)
    single = pltpu.CompilerParams(vmem_limit_bytes=VMEM_LIMIT)
    q_s, k_s, a_s, v_s, sg_s, ysc_s, sconv_s = pl.pallas_call(
        _even_decode_front_kernel,
        out_shape=[jax.ShapeDtypeStruct((dec_b, QK_WIDTH), F32),
                   jax.ShapeDtypeStruct((dec_b, QK_WIDTH), F32),
                   jax.ShapeDtypeStruct((dec_b, QK_WIDTH), F32),
                   jax.ShapeDtypeStruct((dec_b, V_WIDTH), F32),
                   jax.ShapeDtypeStruct((dec_b, V_WIDTH), F32),
                   jax.ShapeDtypeStruct((dec_b, CONV_W), F32),
                   jax.ShapeDtypeStruct((dec_b, (SCONV_K - 1) * CONV_W), F32)],
        compiler_params=single,
        name="even_decode_front",
    )(xs, *even_weights, wsc, sbuf)

    sb = DECODE_STATE_BLOCK
    assert dec_b % sb == 0
    vec_spec = lambda w: pl.BlockSpec((sb, w), lambda i: (i, 0))
    state_spec = pl.BlockSpec((sb, HEADS, DK, DV), lambda i: (i, 0, 0, 0))
    gla_s, o_s = pl.pallas_call(
        _gla_decode_kernel,
        grid=(dec_b // sb,),
        in_specs=[vec_spec(QK_WIDTH), vec_spec(QK_WIDTH), vec_spec(QK_WIDTH), vec_spec(V_WIDTH), state_spec],
        out_specs=[state_spec, vec_spec(V_WIDTH)],
        out_shape=[jax.ShapeDtypeStruct((dec_b, HEADS, DK, DV), F32),
                   jax.ShapeDtypeStruct((dec_b, V_WIDTH), F32)],
        compiler_params=pltpu.CompilerParams(dimension_semantics=("arbitrary",),
                                             vmem_limit_bytes=VMEM_LIMIT),
        name="gla_decode",
    )(q_s, k_s, a_s, v_s, state_gla[0])

    x1_s = pl.pallas_call(
        _even_decode_out_kernel,
        out_shape=jax.ShapeDtypeStruct((dec_b, D_MODEL), F32),
        compiler_params=single,
        name="even_decode_out",
    )(xs, o_s, sg_s, ysc_s, gng, wout_a)

    ob = DECODE_ODD_BLOCK
    assert dec_b % ob == 0
    rows_spec = pl.BlockSpec((ob, D_MODEL), lambda i: (i, 0))
    hist_spec = pl.BlockSpec((CCONV_K - 1, ob, CONV_W), lambda i: (0, i, 0))
    cbuf = jnp.transpose(state_cconv[0], (1, 0, 2))
    y_s, cconv_t = pl.pallas_call(
        _odd_decode_kernel,
        grid=(dec_b // ob,),
        in_specs=[rows_spec] + odd_weight_specs + [hist_spec],
        out_specs=[rows_spec, hist_spec],
        out_shape=[jax.ShapeDtypeStruct((dec_b, D_MODEL), F32),
                   jax.ShapeDtypeStruct((CCONV_K - 1, dec_b, CONV_W), F32)],
        compiler_params=pltpu.CompilerParams(dimension_semantics=("arbitrary",),
                                             vmem_limit_bytes=VMEM_LIMIT),
        name="odd_decode",
    )(x1_s, *odd_weights, cbuf)
    cconv_s = jnp.transpose(cconv_t, (1, 0, 2))[None]

    return (y_p,
            y_s.reshape(dec_b, 1, D_MODEL),
            gla_p,
            sconv_p,
            cconv_p,
            gla_s.reshape(1, dec_b, HEADS, DK, DV),
            sconv_s.reshape(1, dec_b, SCONV_K - 1, CONV_W),
            cconv_s)
```

```python
import jax
import jax.numpy as jnp
import numpy as np
from jax import lax
from jax.experimental import pallas as pl
from jax.experimental.pallas import tpu as pltpu

F32 = jnp.float32
BF16 = jnp.bfloat16

D_MODEL = 1024
HEADS = 4
DK = 128
DV = 256
QK_WIDTH = HEADS * DK
V_WIDTH = HEADS * DV
GATE_RANK = 16
GATE_RANK_PAD = 128
GATE_TEMP_INV = 1.0 / 16.0
CHUNK = 64
CHUNK_SHIFT = 6
SCONV_K = 3
CCONV_K = 31
CONV_W = 1024
RMS_EPS = 1e-6
LN_EPS = 1e-5
Q_SCALE = DK ** -0.5

COL_Q = 0
COL_K = COL_Q + QK_WIDTH
COL_V = COL_K + QK_WIDTH
COL_G = COL_V + V_WIDTH
COL_A_LOW = COL_G + V_WIDTH
COL_HB = COL_A_LOW + GATE_RANK
COL_GATE_B = COL_HB + CONV_W
COL_GATE_C = COL_GATE_B + CONV_W
COL_ZB = COL_GATE_C + CONV_W
MAIN_W = COL_ZB + CONV_W

MXU_K = 256
MXU_N = 256
EVEN_TILE = 512
ODD_TILE = 1024
CCONV_HALO = 32
SCONV_HALO = 8
DFT_N = MXU_K
DFT_HALF = DFT_N // 2
DFT_HOP = DFT_N - CCONV_HALO
DECODE_STATE_BLOCK = 16
DECODE_ODD_BLOCK = 64
VMEM_LIMIT = 60 * 1024 * 1024


def _dot(a, b):
    return jnp.dot(a, b, preferred_element_type=F32)


def _dot_nt(a, b):
    return lax.dot_general(a, b, (((1,), (1,)), ((), ())), preferred_element_type=F32)


def _dot_tn(a, b):
    return lax.dot_general(a, b, (((0,), (0,)), ((), ())), preferred_element_type=F32)


def _proj(h, wt_ref, lo, hi):
    return _dot_nt(h, wt_ref[lo:hi, :])


def _rmsnorm(x, g):
    ms = jnp.mean(x * x, axis=-1, keepdims=True)
    return x * lax.rsqrt(ms + RMS_EPS) * g


def _gate(x, y):
    return x / (1.0 + jnp.exp(-y))


def _silu(x):
    return _gate(x, x)


def _log_sigmoid(x):
    return -(jnp.maximum(-x, 0.0) + jnp.log(1.0 + jnp.exp(-jnp.abs(x))))


def _log_decay(h, wmain_ref, wup_ref, bup_ref):
    a_low = _proj(h, wmain_ref, COL_A_LOW, COL_A_LOW + GATE_RANK_PAD).astype(BF16)
    logit = _dot(a_low, wup_ref[...]) + bup_ref[...]
    return _log_sigmoid(logit) * GATE_TEMP_INV


def _head_rmsnorm(o, gng):
    ms = jnp.mean(o * o, axis=-1, keepdims=True)
    return o * lax.rsqrt(ms + RMS_EPS) * gng


def _layernorm_act(yc, z, lng, lnb):
    mu = jnp.mean(yc, axis=-1, keepdims=True)
    xc = yc - mu
    var = jnp.mean(xc * xc, axis=-1, keepdims=True)
    yn = xc * lax.rsqrt(var + LN_EPS) * lng + lnb
    return (_silu(yn) * _silu(z)).astype(BF16)


def _short_conv_gate(u, prev1, prev2, gate_b, z_b, wsc_ref):
    y = wsc_ref[2:3, :] * u + wsc_ref[1:2, :] * prev1 + wsc_ref[0:1, :] * prev2
    return gate_b * y * _silu(z_b)


def _even_prompt_kernel(x_ref, ng_ref, wmain_ref, wup_ref, bup_ref, gng_ref, wsc_ref, wout_ref,
                        x1_ref, sgla_ref, sconv_ref,
                        st_ref, ubuf_ref, qe_ref, ke_ref, kd_ref, v_ref, dec_ref, mix_ref, ysc_ref):
    tm = EVEN_TILE
    t = pl.program_id(1)
    last_t = pl.num_programs(1) - 1

    @pl.when(t == 0)
    def _():
        st_ref[...] = jnp.zeros_like(st_ref)
        ubuf_ref[0:SCONV_HALO, :] = jnp.zeros((SCONV_HALO, CONV_W), F32)

    x = x_ref[...]
    h = _rmsnorm(x, ng_ref[...]).astype(BF16)

    def short_conv_group(g0):
        cols = slice(g0, g0 + MXU_N)
        part = lambda c0: _proj(h, wmain_ref, c0 + g0, c0 + g0 + MXU_N)
        u = part(COL_GATE_C) * part(COL_HB)
        ubuf_ref[SCONV_HALO:SCONV_HALO + tm, cols] = u
        y = (wsc_ref[2:3, cols] * u + wsc_ref[1:2, cols] * ubuf_ref[pl.ds(SCONV_HALO - 1, tm), cols]
             + wsc_ref[0:1, cols] * ubuf_ref[pl.ds(SCONV_HALO - 2, tm), cols])
        ysc_ref[:, cols] = (part(COL_GATE_B) * y * _silu(part(COL_ZB))).astype(BF16)

    q = _proj(h, wmain_ref, COL_Q, COL_K) * Q_SCALE
    k = _proj(h, wmain_ref, COL_K, COL_V)
    v_ref[...] = _proj(h, wmain_ref, COL_V, COL_G).astype(BF16)
    a_low = _proj(h, wmain_ref, COL_A_LOW, COL_A_LOW + GATE_RANK_PAD).astype(BF16)
    short_conv_group(0)
    log_a = _log_sigmoid(_dot(a_low, wup_ref[...]) + bup_ref[...]) * GATE_TEMP_INV
    short_conv_group(MXU_N)

    row = lax.broadcasted_iota(jnp.int32, (MXU_K, MXU_K), 0)
    col = lax.broadcasted_iota(jnp.int32, (MXU_K, MXU_K), 1)
    in_chunk_causal = ((row >> CHUNK_SHIFT) == (col >> CHUNK_SHIFT)) & (col <= row)
    tri = jnp.where(in_chunk_causal, 1.0, 0.0).astype(BF16)
    la_hi = log_a.astype(BF16)
    la_lo = (log_a - la_hi.astype(F32)).astype(BF16)
    for sb in range(tm // MXU_K):
        rows = slice(sb * MXU_K, (sb + 1) * MXU_K)
        b_cum = _dot(tri, la_hi[rows, :]) + _dot(tri, la_lo[rows, :])
        b_tot = jnp.concatenate(
            [jnp.broadcast_to(b_cum[(c + 1) * CHUNK - 1:(c + 1) * CHUNK, :], (CHUNK, QK_WIDTH))
             for c in range(MXU_K // CHUNK)], axis=0)
        qe_ref[rows, :] = (q[rows, :] * jnp.exp(b_cum)).astype(BF16)
        ke_ref[rows, :] = (k[rows, :] * jnp.exp(-b_cum)).astype(BF16)
        kd_ref[rows, :] = (k[rows, :] * jnp.exp(b_tot - b_cum)).astype(BF16)
        dec_ref[rows, :] = jnp.exp(b_tot)
        if sb < 2:
            short_conv_group((2 + sb) * MXU_N)

    gng = gng_ref[...]
    for hh in range(HEADS):
        kcols = slice(hh * DK, (hh + 1) * DK)
        vcols = slice(hh * DV, (hh + 1) * DV)
        st = st_ref[hh]
        for sb in range(tm // MXU_K):
            rows = slice(sb * MXU_K, (sb + 1) * MXU_K)
            sc = jnp.where(in_chunk_causal, _dot_nt(qe_ref[rows, kcols], ke_ref[rows, kcols]), 0.0)
            o_intra = _dot(sc.astype(BF16), v_ref[rows, vcols])
            for c in range(MXU_K // CHUNK):
                r0 = sb * MXU_K + c * CHUNK
                crow = slice(r0, r0 + CHUNK)
                o = o_intra[c * CHUNK:(c + 1) * CHUNK, :] + _dot_nt(qe_ref[crow, kcols], st.astype(BF16))
                mix_ref[crow, vcols] = _head_rmsnorm(o, gng)
                dec = dec_ref[r0:r0 + 1, kcols]
                st = st * dec + _dot_tn(v_ref[crow, vcols], kd_ref[crow, kcols])
        st_ref[hh] = st

    o_mix = (mix_ref[...] * _silu(_proj(h, wmain_ref, COL_G, COL_A_LOW))).astype(BF16)

    ubuf_ref[0:SCONV_HALO, :] = ubuf_ref[tm:tm + SCONV_HALO, :]

    out = _dot(o_mix, wout_ref[0:V_WIDTH, :]) + _dot(ysc_ref[...], wout_ref[V_WIDTH:V_WIDTH + CONV_W, :])
    x1_ref[...] = x + out

    @pl.when(t == last_t)
    def _():
        for hh in range(HEADS):
            sgla_ref[hh] = st_ref[hh].T
        sconv_ref[...] = ubuf_ref[pl.ds(SCONV_HALO + tm - (SCONV_K - 1), SCONV_K - 1), :]


def _odd_prompt_kernel(x_ref, ng_ref, win_ref, bin_ref, bdw_ref, lng_ref, lnb_ref, wout_ref,
                       bout_ref, fng_ref, fwd_ref, inv_ref, ha_ref, hb_ref, ha2_ref,
                       y_ref, cconv_ref,
                       ub_ref, yc_ref, tail_ref):
    tm = ODD_TILE
    t = pl.program_id(1)
    last_t = pl.num_programs(1) - 1

    @pl.when(t == 0)
    def _():
        ub_ref[0:CCONV_HALO, :] = jnp.zeros((CCONV_HALO, CONV_W), BF16)

    x = x_ref[...]
    h = _rmsnorm(x, ng_ref[...]).astype(BF16)
    for g0 in range(0, CONV_W, MXU_N):
        cols = slice(g0, g0 + MXU_N)
        gcols = slice(CONV_W + g0, CONV_W + g0 + MXU_N)
        u = _gate(_dot(h, win_ref[:, cols]) + bin_ref[:, cols], _dot(h, win_ref[:, gcols]) + bin_ref[:, gcols])
        tail_ref[:, cols] = u[tm - CCONV_HALO:, :]
        ub_ref[CCONV_HALO:CCONV_HALO + tm, cols] = u.astype(BF16)

    @pl.when(t == last_t)
    def _():
        cconv_ref[...] = tail_ref[CCONV_HALO - (CCONV_K - 1):, :]

    bdw = bdw_ref[...]
    ha = ha_ref[...]
    hb = hb_ref[...]
    ha2 = ha2_ref[...]
    fwd = fwd_ref[...].astype(BF16)
    inv = inv_ref[...].astype(BF16)
    for start in sorted({min(s0, tm - DFT_HOP) for s0 in range(0, tm, DFT_HOP)}):
        spec = _dot(fwd, ub_ref[start:start + DFT_N, :])
        p = spec[0:DFT_HALF, :]
        q = spec[DFT_HALF:DFT_N, :]
        prod = jnp.concatenate([p * ha - q * hb, p * hb + q * ha2], axis=0).astype(BF16)
        yc_ref[start:start + DFT_HOP, :] = _dot(inv, prod) + bdw

    ub_ref[0:CCONV_HALO, :] = ub_ref[tm:tm + CCONV_HALO, :]

    z = _dot(h, win_ref[:, 2 * CONV_W:3 * CONV_W]) + bin_ref[:, 2 * CONV_W:3 * CONV_W]
    act = _layernorm_act(yc_ref[...], z, lng_ref[...], lnb_ref[...])
    y_ref[...] = _rmsnorm(x + _dot(act, wout_ref[...]) + bout_ref[...], fng_ref[...])


def _even_decode_front_kernel(x_ref, ng_ref, wmain_ref, wup_ref, bup_ref, wsc_ref, sbuf_ref,
                              q_ref, k_ref, a_ref, v_ref, sg_ref, y_ref, snew_ref):
    h = _rmsnorm(x_ref[...], ng_ref[...]).astype(BF16)
    q_ref[...] = _proj(h, wmain_ref, COL_Q, COL_K) * Q_SCALE
    k_ref[...] = _proj(h, wmain_ref, COL_K, COL_V)
    v_ref[...] = _proj(h, wmain_ref, COL_V, COL_G)
    sg_ref[...] = _silu(_proj(h, wmain_ref, COL_G, COL_A_LOW))
    a_ref[...] = jnp.exp(_log_decay(h, wmain_ref, wup_ref, bup_ref))
    u = _proj(h, wmain_ref, COL_GATE_C, COL_ZB) * _proj(h, wmain_ref, COL_HB, COL_GATE_B)
    prev2 = sbuf_ref[:, 0:CONV_W]
    prev1 = sbuf_ref[:, CONV_W:2 * CONV_W]
    y_ref[...] = _short_conv_gate(u, prev1, prev2, _proj(h, wmain_ref, COL_GATE_B, COL_GATE_C),
                                  _proj(h, wmain_ref, COL_ZB, MAIN_W), wsc_ref)
    snew_ref[:, 0:CONV_W] = prev1
    snew_ref[:, CONV_W:2 * CONV_W] = u


def _lane_bcast_column(row):
    return jnp.broadcast_to(row, (DK, DK)).T


def _gla_decode_kernel(q_ref, k_ref, a_ref, v_ref, s_ref, snew_ref, o_ref):
    for b in range(DECODE_STATE_BLOCK):
        for hh in range(HEADS):
            kcols = slice(hh * DK, (hh + 1) * DK)
            vcols = slice(hh * DV, (hh + 1) * DV)
            a_col = _lane_bcast_column(a_ref[b:b + 1, kcols])
            k_col = _lane_bcast_column(k_ref[b:b + 1, kcols])
            q_col = _lane_bcast_column(q_ref[b:b + 1, kcols])
            v_row = v_ref[b:b + 1, vcols]
            halves = []
            for half in range(DV // DK):
                lanes = slice(half * DK, (half + 1) * DK)
                s_new = a_col * s_ref[b, hh, :, lanes] + k_col * v_row[:, lanes]
                snew_ref[b, hh, :, lanes] = s_new
                halves.append(jnp.sum(q_col * s_new, axis=0, keepdims=True))
            o_ref[b:b + 1, vcols] = jnp.concatenate(halves, axis=1)


def _even_decode_out_kernel(x_ref, o_ref, sg_ref, y_ref, gng_ref, wout_ref, x1_ref):
    gng = gng_ref[...]
    parts = [_head_rmsnorm(o_ref[:, hh * DV:(hh + 1) * DV], gng) for hh in range(HEADS)]
    o_mix = (jnp.concatenate(parts, axis=1) * sg_ref[...]).astype(BF16)
    out = _dot(o_mix, wout_ref[0:V_WIDTH, :]) + _dot(y_ref[...].astype(BF16), wout_ref[V_WIDTH:V_WIDTH + CONV_W, :])
    x1_ref[...] = x_ref[...] + out


def _odd_decode_kernel(x_ref, ng_ref, win_ref, bin_ref, wdw_ref, bdw_ref, lng_ref, lnb_ref, wout_ref,
                       bout_ref, fng_ref, cbuf_ref,
                       y_ref, cnew_ref):
    n_hist = CCONV_K - 1
    x = x_ref[...]
    h = _rmsnorm(x, ng_ref[...]).astype(BF16)
    a = _dot(h, win_ref[:, 0:CONV_W]) + bin_ref[:, 0:CONV_W]
    a_gate = _dot(h, win_ref[:, CONV_W:2 * CONV_W]) + bin_ref[:, CONV_W:2 * CONV_W]
    u = _gate(a, a_gate)
    yc = bdw_ref[...] + wdw_ref[n_hist:CCONV_K, :] * u
    for j in range(n_hist):
        tap = cbuf_ref[j]
        yc = yc + wdw_ref[j:j + 1, :] * tap
        if j >= 1:
            cnew_ref[j - 1] = tap
    cnew_ref[n_hist - 1] = u
    z = _dot(h, win_ref[:, 2 * CONV_W:3 * CONV_W]) + bin_ref[:, 2 * CONV_W:3 * CONV_W]
    act = _layernorm_act(yc, z, lng_ref[...], lnb_ref[...])
    y_ref[...] = _rmsnorm(x + _dot(act, wout_ref[...]) + bout_ref[...], fng_ref[...])


def _const_spec(shape):
    nd = len(shape)
    return pl.BlockSpec(shape, lambda *_: (0,) * nd, pipeline_mode=pl.Buffered(1))


def _row(v):
    return v.reshape(1, -1)


def kernel(x_prompt, x_sample, state_gla, state_sconv, state_cconv, norm_g, w_in_a, w_gate_up, b_gate_up, gla_norm_g, w_sconv, w_out_a, w_in_c, b_in_c, w_dwconv, b_dwconv, ln_g, ln_b, w_out_c, b_out_c, final_norm_g):
    bsz, seq, d = x_prompt.shape
    dec_b = x_sample.shape[0]
    assert d == D_MODEL and seq % EVEN_TILE == 0 and seq % ODD_TILE == 0 and x_sample.shape[1] == 1
    assert EVEN_TILE // MXU_K >= 2 and CONV_W == 4 * MXU_N and ODD_TILE >= DFT_HOP
    assert w_in_a.shape[0] == 1 and w_in_c.shape[0] == 1 and norm_g.shape[0] == 2

    assert w_in_a.shape[2] == MAIN_W
    wmain = w_in_a[0].T.astype(BF16)
    wup = jnp.pad(w_gate_up[0], ((0, GATE_RANK_PAD - GATE_RANK), (0, 0))).astype(BF16)
    bup = _row(b_gate_up[0])
    gng = _row(gla_norm_g[0])
    wsc = w_sconv[0]
    wout_a = w_out_a[0].astype(BF16)
    ng0 = _row(norm_g[0])
    ng1 = _row(norm_g[1])
    win_c = w_in_c[0].astype(BF16)
    bin_c = _row(b_in_c[0])
    wdw = w_dwconv[0]
    bdw = _row(b_dwconv[0])
    lng = _row(ln_g[0])
    lnb = _row(ln_b[0])
    wout_c = w_out_c[0].astype(BF16)
    bout = _row(b_out_c[0])
    fng = _row(final_norm_g)

    even_weights = (ng0, wmain, wup, bup)
    even_prompt_weights = even_weights + (gng, wsc, wout_a)
    odd_weights = (ng1, win_c, bin_c, wdw, bdw, lng, lnb, wout_c, bout, fng)
    odd_weight_specs = [_const_spec(w.shape) for w in odd_weights]
    hf = DFT_HALF
    assert CCONV_HALO >= CCONV_K - 1 and DFT_HOP % 16 == 0
    kk = np.arange(hf)[:, None]

    def packed_basis(pos):
        ang = 2.0 * np.pi * kk * pos[None, :] / DFT_N
        lower = -np.sin(ang)
        lower[0] = np.cos(np.pi * pos)
        return np.cos(ang), lower

    fc, fs = packed_basis(np.arange(DFT_N, dtype=np.float64))
    fwd = np.concatenate([fc, fs], axis=0).astype(np.float32)
    ic, isn = packed_basis(np.arange(CCONV_HALO, DFT_N, dtype=np.float64))
    weight = np.full((hf, 1), 2.0 / DFT_N)
    weight[0] = 1.0 / DFT_N
    inv = np.concatenate([weight * ic, weight * isn], axis=0).T.astype(np.float32)
    hc, hs = packed_basis(np.arange(CCONV_K, dtype=np.float64))
    hs_imag = hs.copy()
    hs_imag[0] = 0.0
    hc_alt = hc.copy()
    hc_alt[0] = hs[0]
    resp = np.concatenate([hc, hs_imag, hc_alt], axis=0).astype(np.float32)
    h_resp = jnp.dot(jnp.asarray(resp), wdw[::-1], precision=lax.Precision.HIGHEST)
    odd_prompt_weights = (ng1, win_c, bin_c, bdw, lng, lnb, wout_c, bout, fng,
                          jnp.asarray(fwd), jnp.asarray(inv), h_resp[:hf], h_resp[hf:2 * hf], h_resp[2 * hf:])

    prompt_params = pltpu.CompilerParams(dimension_semantics=("arbitrary", "arbitrary"),
                                         vmem_limit_bytes=VMEM_LIMIT)

    tm = EVEN_TILE
    tile_spec = pl.BlockSpec((None, tm, D_MODEL), lambda b, t: (b, t, 0))
    x1_p, gla_p, sconv_p = pl.pallas_call(
        _even_prompt_kernel,
        grid=(bsz, seq // tm),
        in_specs=[tile_spec] + [_const_spec(w.shape) for w in even_prompt_weights],
        out_specs=[tile_spec,
                   pl.BlockSpec((None, None, HEADS, DK, DV), lambda b, t: (0, b, 0, 0, 0)),
                   pl.BlockSpec((None, None, SCONV_K - 1, CONV_W), lambda b, t: (0, b, 0, 0))],
        out_shape=[jax.ShapeDtypeStruct((bsz, seq, D_MODEL), F32),
                   jax.ShapeDtypeStruct((1, bsz, HEADS, DK, DV), F32),
                   jax.ShapeDtypeStruct((1, bsz, SCONV_K - 1, CONV_W), F32)],
        scratch_shapes=[pltpu.VMEM((HEADS, DV, DK), F32),
                        pltpu.VMEM((tm + SCONV_HALO, CONV_W), F32),
                        pltpu.VMEM((tm, QK_WIDTH), BF16),
                        pltpu.VMEM((tm, QK_WIDTH), BF16),
                        pltpu.VMEM((tm, QK_WIDTH), BF16),
                        pltpu.VMEM((tm, V_WIDTH), BF16),
                        pltpu.VMEM((tm, QK_WIDTH), F32),
                        pltpu.VMEM((tm, V_WIDTH), F32),
                        pltpu.VMEM((tm, CONV_W), BF16)],
        compiler_params=prompt_params,
        name="even_prompt",
    )(x_prompt, *even_prompt_weights)

    tm = ODD_TILE
    tile_spec = pl.BlockSpec((None, tm, D_MODEL), lambda b, t: (b, t, 0))
    y_p, cconv_p = pl.pallas_call(
        _odd_prompt_kernel,
        grid=(bsz, seq // tm),
        in_specs=[tile_spec] + [_const_spec(w.shape) for w in odd_prompt_weights],
        out_specs=[tile_spec,
                   pl.BlockSpec((None, None, CCONV_K - 1, CONV_W), lambda b, t: (0, b, 0, 0))],
        out_shape=[jax.ShapeDtypeStruct((bsz, seq, D_MODEL), F32),
                   jax.ShapeDtypeStruct((1, bsz, CCONV_K - 1, CONV_W), F32)],
        scratch_shapes=[pltpu.VMEM((tm + CCONV_HALO, CONV_W), BF16),
                        pltpu.VMEM((tm, CONV_W), F32),
                        pltpu.VMEM((CCONV_HALO, CONV_W), F32)],
        compiler_params=prompt_params,
        name="odd_prompt",
    )(x1_p, *odd_prompt_weights)

    xs = x_sample.reshape(dec_b, D_MODEL)
    sbuf = state_sconv.reshape(dec_b, (SCONV_K - 1) * CONV_W)
    single = pltpu.CompilerParams(vmem_limit_bytes=VMEM_LIMIT)
    q_s, k_s, a_s, v_s, sg_s, ysc_s, sconv_s = pl.pallas_call(
        _even_decode_front_kernel,
        out_shape=[jax.ShapeDtypeStruct((dec_b, QK_WIDTH), F32),
                   jax.ShapeDtypeStruct((dec_b, QK_WIDTH), F32),
                   jax.ShapeDtypeStruct((dec_b, QK_WIDTH), F32),
                   jax.ShapeDtypeStruct((dec_b, V_WIDTH), F32),
                   jax.ShapeDtypeStruct((dec_b, V_WIDTH), F32),
                   jax.ShapeDtypeStruct((dec_b, CONV_W), F32),
                   jax.ShapeDtypeStruct((dec_b, (SCONV_K - 1) * CONV_W), F32)],
        compiler_params=single,
        name="even_decode_front",
    )(xs, *even_weights, wsc, sbuf)

    sb = DECODE_STATE_BLOCK
    assert dec_b % sb == 0
    vec_spec = lambda w: pl.BlockSpec((sb, w), lambda i: (i, 0))
    state_spec = pl.BlockSpec((sb, HEADS, DK, DV), lambda i: (i, 0, 0, 0))
    gla_s, o_s = pl.pallas_call(
        _gla_decode_kernel,
        grid=(dec_b // sb,),
        in_specs=[vec_spec(QK_WIDTH), vec_spec(QK_WIDTH), vec_spec(QK_WIDTH), vec_spec(V_WIDTH), state_spec],
        out_specs=[state_spec, vec_spec(V_WIDTH)],
        out_shape=[jax.ShapeDtypeStruct((dec_b, HEADS, DK, DV), F32),
                   jax.ShapeDtypeStruct((dec_b, V_WIDTH), F32)],
        compiler_params=pltpu.CompilerParams(dimension_semantics=("arbitrary",),
                                             vmem_limit_bytes=VMEM_LIMIT),
        name="gla_decode",
    )(q_s, k_s, a_s, v_s, state_gla[0])

    x1_s = pl.pallas_call(
        _even_decode_out_kernel,
        out_shape=jax.ShapeDtypeStruct((dec_b, D_MODEL), F32),
        compiler_params=single,
        name="even_decode_out",
    )(xs, o_s, sg_s, ysc_s, gng, wout_a)

    ob = DECODE_ODD_BLOCK
    assert dec_b % ob == 0
    rows_spec = pl.BlockSpec((ob, D_MODEL), lambda i: (i, 0))
    hist_spec = pl.BlockSpec((CCONV_K - 1, ob, CONV_W), lambda i: (0, i, 0))
    cbuf = jnp.transpose(state_cconv[0], (1, 0, 2))
    y_s, cconv_t = pl.pallas_call(
        _odd_decode_kernel,
        grid=(dec_b // ob,),
        in_specs=[rows_spec] + odd_weight_specs + [hist_spec],
        out_specs=[rows_spec, hist_spec],
        out_shape=[jax.ShapeDtypeStruct((dec_b, D_MODEL), F32),
                   jax.ShapeDtypeStruct((CCONV_K - 1, dec_b, CONV_W), F32)],
        compiler_params=pltpu.CompilerParams(dimension_semantics=("arbitrary",),
                                             vmem_limit_bytes=VMEM_LIMIT),
        name="odd_decode",
    )(x1_s, *odd_weights, cbuf)
    cconv_s = jnp.transpose(cconv_t, (1, 0, 2))[None]

    return (y_p,
            y_s.reshape(dec_b, 1, D_MODEL),
            gla_p,
            sconv_p,
            cconv_p,
            gla_s.reshape(1, dec_b, HEADS, DK, DV),
            sconv_s.reshape(1, dec_b, SCONV_K - 1, CONV_W),
            cconv_s)
```

```python
import jax
import jax.numpy as jnp
import numpy as np
from jax import lax
from jax.experimental import pallas as pl
from jax.experimental.pallas import tpu as pltpu

F32 = jnp.float32
BF16 = jnp.bfloat16

D_MODEL = 1024
HEADS = 4
DK = 128
DV = 256
QK_WIDTH = HEADS * DK
V_WIDTH = HEADS * DV
GATE_RANK = 16
GATE_RANK_PAD = 128
GATE_TEMP_INV = 1.0 / 16.0
CHUNK = 64
CHUNK_SHIFT = 6
SCONV_K = 3
CCONV_K = 31
CONV_W = 1024
RMS_EPS = 1e-6
LN_EPS = 1e-5
Q_SCALE = DK ** -0.5

COL_Q = 0
COL_K = COL_Q + QK_WIDTH
COL_V = COL_K + QK_WIDTH
COL_G = COL_V + V_WIDTH
COL_A_LOW = COL_G + V_WIDTH
COL_HB = COL_A_LOW + GATE_RANK
COL_GATE_B = COL_HB + CONV_W
COL_GATE_C = COL_GATE_B + CONV_W
COL_ZB = COL_GATE_C + CONV_W
MAIN_W = COL_ZB + CONV_W

MXU_K = 256
MXU_N = 256
EVEN_TILE = 512
ODD_TILE = 1024
CCONV_HALO = 32
SCONV_HALO = 8
DFT_N = MXU_K
DFT_HALF = DFT_N // 2
DFT_HOP = DFT_N - CCONV_HALO
DECODE_STATE_BLOCK = 16
DECODE_ODD_BLOCK = 64
VMEM_LIMIT = 60 * 1024 * 1024


def _dot(a, b):
    return jnp.dot(a, b, preferred_element_type=F32)


def _dot_nt(a, b):
    return lax.dot_general(a, b, (((1,), (1,)), ((), ())), preferred_element_type=F32)


def _dot_tn(a, b):
    return lax.dot_general(a, b, (((0,), (0,)), ((), ())), preferred_element_type=F32)


def _proj(h, wt_ref, lo, hi):
    return _dot_nt(h, wt_ref[lo:hi, :])


def _rmsnorm(x, g):
    ms = jnp.mean(x * x, axis=-1, keepdims=True)
    return x * lax.rsqrt(ms + RMS_EPS) * g


def _gate(x, y):
    return x / (1.0 + jnp.exp(-y))


def _silu(x):
    return _gate(x, x)


def _log_sigmoid(x):
    return -(jnp.maximum(-x, 0.0) + jnp.log(1.0 + jnp.exp(-jnp.abs(x))))


def _log_decay(h, wmain_ref, wup_ref, bup_ref):
    a_low = _proj(h, wmain_ref, COL_A_LOW, COL_A_LOW + GATE_RANK_PAD).astype(BF16)
    logit = _dot(a_low, wup_ref[...]) + bup_ref[...]
    return _log_sigmoid(logit) * GATE_TEMP_INV


def _head_rmsnorm(o, gng):
    ms = jnp.mean(o * o, axis=-1, keepdims=True)
    return o * lax.rsqrt(ms + RMS_EPS) * gng


def _layernorm_act(yc, z, lng, lnb):
    mu = jnp.mean(yc, axis=-1, keepdims=True)
    xc = yc - mu
    var = jnp.mean(xc * xc, axis=-1, keepdims=True)
    yn = xc * lax.rsqrt(var + LN_EPS) * lng + lnb
    return (_silu(yn) * _silu(z)).astype(BF16)


def _short_conv_gate(u, prev1, prev2, gate_b, z_b, wsc_ref):
    y = wsc_ref[2:3, :] * u + wsc_ref[1:2, :] * prev1 + wsc_ref[0:1, :] * prev2
    return gate_b * y * _silu(z_b)


def _even_prompt_kernel(x_ref, ng_ref, wmain_ref, wup_ref, bup_ref, gng_ref, wsc_ref, wout_ref,
                        x1_ref, sgla_ref, sconv_ref,
                        st_ref, ubuf_ref, qe_ref, ke_ref, kd_ref, v_ref, dec_ref, mix_ref, ysc_ref):
    tm = EVEN_TILE
    t = pl.program_id(1)
    last_t = pl.num_programs(1) - 1

    @pl.when(t == 0)
    def _():
        st_ref[...] = jnp.zeros_like(st_ref)
        ubuf_ref[0:SCONV_HALO, :] = jnp.zeros((SCONV_HALO, CONV_W), F32)

    x = x_ref[...]
    h = _rmsnorm(x, ng_ref[...]).astype(BF16)

    def short_conv_group(g0):
        cols = slice(g0, g0 + MXU_N)
        part = lambda c0: _proj(h, wmain_ref, c0 + g0, c0 + g0 + MXU_N)
        u = part(COL_GATE_C) * part(COL_HB)
        ubuf_ref[SCONV_HALO:SCONV_HALO + tm, cols] = u
        y = (wsc_ref[2:3, cols] * u + wsc_ref[1:2, cols] * ubuf_ref[pl.ds(SCONV_HALO - 1, tm), cols]
             + wsc_ref[0:1, cols] * ubuf_ref[pl.ds(SCONV_HALO - 2, tm), cols])
        ysc_ref[:, cols] = (part(COL_GATE_B) * y * _silu(part(COL_ZB))).astype(BF16)

    q = _proj(h, wmain_ref, COL_Q, COL_K) * Q_SCALE
    k = _proj(h, wmain_ref, COL_K, COL_V)
    v_ref[...] = _proj(h, wmain_ref, COL_V, COL_G).astype(BF16)
    a_low = _proj(h, wmain_ref, COL_A_LOW, COL_A_LOW + GATE_RANK_PAD).astype(BF16)
    short_conv_group(0)
    log_a = _log_sigmoid(_dot(a_low, wup_ref[...]) + bup_ref[...]) * GATE_TEMP_INV
    short_conv_group(MXU_N)

    row = lax.broadcasted_iota(jnp.int32, (MXU_K, MXU_K), 0)
    col = lax.broadcasted_iota(jnp.int32, (MXU_K, MXU_K), 1)
    in_chunk_causal = ((row >> CHUNK_SHIFT) == (col >> CHUNK_SHIFT)) & (col <= row)
    tri = jnp.where(in_chunk_causal, 1.0, 0.0).astype(BF16)
    la_hi = log_a.astype(BF16)
    la_lo = (log_a - la_hi.astype(F32)).astype(BF16)
    for sb in range(tm // MXU_K):
        rows = slice(sb * MXU_K, (sb + 1) * MXU_K)
        b_cum = _dot(tri, la_hi[rows, :]) + _dot(tri, la_lo[rows, :])
        b_tot = jnp.concatenate(
            [jnp.broadcast_to(b_cum[(c + 1) * CHUNK - 1:(c + 1) * CHUNK, :], (CHUNK, QK_WIDTH))
             for c in range(MXU_K // CHUNK)], axis=0)
        qe_ref[rows, :] = (q[rows, :] * jnp.exp(b_cum)).astype(BF16)
        ke_ref[rows, :] = (k[rows, :] * jnp.exp(-b_cum)).astype(BF16)
        kd_ref[rows, :] = (k[rows, :] * jnp.exp(b_tot - b_cum)).astype(BF16)
        dec_ref[rows, :] = jnp.exp(b_tot)
        if sb < 2:
            short_conv_group((2 + sb) * MXU_N)

    gng = gng_ref[...]
    for hh in range(HEADS):
        kcols = slice(hh * DK, (hh + 1) * DK)
        vcols = slice(hh * DV, (hh + 1) * DV)
        st = st_ref[hh]
        for sb in range(tm // MXU_K):
            rows = slice(sb * MXU_K, (sb + 1) * MXU_K)
            sc = jnp.where(in_chunk_causal, _dot_nt(qe_ref[rows, kcols], ke_ref[rows, kcols]), 0.0)
            o_intra = _dot(sc.astype(BF16), v_ref[rows, vcols])
            for c in range(MXU_K // CHUNK):
                r0 = sb * MXU_K + c * CHUNK
                crow = slice(r0, r0 + CHUNK)
                o = o_intra[c * CHUNK:(c + 1) * CHUNK, :] + _dot_nt(qe_ref[crow, kcols], st.astype(BF16))
                mix_ref[crow, vcols] = _head_rmsnorm(o, gng)
                dec = dec_ref[r0:r0 + 1, kcols]
                st = st * dec + _dot_tn(v_ref[crow, vcols], kd_ref[crow, kcols])
        st_ref[hh] = st

    o_mix = (mix_ref[...] * _silu(_proj(h, wmain_ref, COL_G, COL_A_LOW))).astype(BF16)

    ubuf_ref[0:SCONV_HALO, :] = ubuf_ref[tm:tm + SCONV_HALO, :]

    out = _dot(o_mix, wout_ref[0:V_WIDTH, :]) + _dot(ysc_ref[...], wout_ref[V_WIDTH:V_WIDTH + CONV_W, :])
    x1_ref[...] = x + out

    @pl.when(t == last_t)
    def _():
        for hh in range(HEADS):
            sgla_ref[hh] = st_ref[hh].T
        sconv_ref[...] = ubuf_ref[pl.ds(SCONV_HALO + tm - (SCONV_K - 1), SCONV_K - 1), :]


def _odd_prompt_kernel(x_ref, ng_ref, win_ref, bin_ref, bdw_ref, lng_ref, lnb_ref, wout_ref,
                       bout_ref, fng_ref, fwd_ref, inv_ref, ha_ref, hb_ref, ha2_ref,
                       y_ref, cconv_ref,
                       ub_ref, yc_ref, tail_ref):
    tm = ODD_TILE
    t = pl.program_id(1)
    last_t = pl.num_programs(1) - 1

    @pl.when(t == 0)
    def _():
        ub_ref[0:CCONV_HALO, :] = jnp.zeros((CCONV_HALO, CONV_W), BF16)

    x = x_ref[...]
    h = _rmsnorm(x, ng_ref[...]).astype(BF16)
    for g0 in range(0, CONV_W, MXU_N):
        cols = slice(g0, g0 + MXU_N)
        gcols = slice(CONV_W + g0, CONV_W + g0 + MXU_N)
        u = _gate(_dot(h, win_ref[:, cols]) + bin_ref[:, cols], _dot(h, win_ref[:, gcols]) + bin_ref[:, gcols])
        tail_ref[:, cols] = u[tm - CCONV_HALO:, :]
        ub_ref[CCONV_HALO:CCONV_HALO + tm, cols] = u.astype(BF16)

    @pl.when(t == last_t)
    def _():
        cconv_ref[...] = tail_ref[CCONV_HALO - (CCONV_K - 1):, :]

    bdw = bdw_ref[...]
    ha = ha_ref[...]
    hb = hb_ref[...]
    ha2 = ha2_ref[...]
    fwd = fwd_ref[...].astype(BF16)
    inv = inv_ref[...].astype(BF16)
    for start in sorted({min(s0, tm - DFT_HOP) for s0 in range(0, tm, DFT_HOP)}):
        spec = _dot(fwd, ub_ref[start:start + DFT_N, :])
        p = spec[0:DFT_HALF, :]
        q = spec[DFT_HALF:DFT_N, :]
        prod = jnp.concatenate([p * ha - q * hb, p * hb + q * ha2], axis=0).astype(BF16)
        yc_ref[start:start + DFT_HOP, :] = _dot(inv, prod) + bdw

    ub_ref[0:CCONV_HALO, :] = ub_ref[tm:tm + CCONV_HALO, :]

    z = _dot(h, win_ref[:, 2 * CONV_W:3 * CONV_W]) + bin_ref[:, 2 * CONV_W:3 * CONV_W]
    act = _layernorm_act(yc_ref[...], z, lng_ref[...], lnb_ref[...])
    y_ref[...] = _rmsnorm(x + _dot(act, wout_ref[...]) + bout_ref[...], fng_ref[...])


def _even_decode_front_kernel(x_ref, ng_ref, wmain_ref, wup_ref, bup_ref, wsc_ref, sbuf_ref,
                              q_ref, k_ref, a_ref, v_ref, sg_ref, y_ref, snew_ref):
    h = _rmsnorm(x_ref[...], ng_ref[...]).astype(BF16)
    q_ref[...] = _proj(h, wmain_ref, COL_Q, COL_K) * Q_SCALE
    k_ref[...] = _proj(h, wmain_ref, COL_K, COL_V)
    v_ref[...] = _proj(h, wmain_ref, COL_V, COL_G)
    sg_ref[...] = _silu(_proj(h, wmain_ref, COL_G, COL_A_LOW))
    a_ref[...] = jnp.exp(_log_decay(h, wmain_ref, wup_ref, bup_ref))
    u = _proj(h, wmain_ref, COL_GATE_C, COL_ZB) * _proj(h, wmain_ref, COL_HB, COL_GATE_B)
    prev2 = sbuf_ref[:, 0:CONV_W]
    prev1 = sbuf_ref[:, CONV_W:2 * CONV_W]
    y_ref[...] = _short_conv_gate(u, prev1, prev2, _proj(h, wmain_ref, COL_GATE_B, COL_GATE_C),
                                  _proj(h, wmain_ref, COL_ZB, MAIN_W), wsc_ref)
    snew_ref[:, 0:CONV_W] = prev1
    snew_ref[:, CONV_W:2 * CONV_W] = u


def _bf16_terms(x):
    hi = x.astype(BF16).astype(F32)
    mid = (x - hi).astype(BF16).astype(F32)
    lo = (x - hi - mid).astype(BF16).astype(F32)
    return [hi, mid, lo]


def _gla_decode_kernel(q_ref, k_ref, a_ref, v_ref, s_ref, snew_ref, o_ref):
    n_vec = 3
    row_id = lax.broadcasted_iota(jnp.int32, (16, n_vec * DK), 0)
    col_id = lax.broadcasted_iota(jnp.int32, (16, n_vec * DK), 1)
    picks = [(row_id >= 3 * v) & (row_id < 3 * v + 3) & (col_id >= v * DK) & (col_id < (v + 1) * DK)
             for v in range(n_vec)]
    sel = jnp.where(picks[0] | picks[1] | picks[2], 1.0, 0.0).astype(BF16)
    pad = jnp.zeros((16 - 3 * n_vec, QK_WIDTH), F32)
    for b in range(DECODE_STATE_BLOCK):
        rows = (_bf16_terms(a_ref[b:b + 1, :]) + _bf16_terms(k_ref[b:b + 1, :])
                + _bf16_terms(q_ref[b:b + 1, :]) + [pad])
        stack = jnp.concatenate(rows, axis=0).astype(BF16)
        for hh in range(HEADS):
            kcols = slice(hh * DK, (hh + 1) * DK)
            vcols = slice(hh * DV, (hh + 1) * DV)
            cols3 = _dot_tn(stack[:, kcols], sel)
            a_col = cols3[:, 0:DK]
            k_col = cols3[:, DK:2 * DK]
            q_col = cols3[:, 2 * DK:3 * DK]
            v_row = v_ref[b:b + 1, vcols]
            halves = []
            for half in range(DV // DK):
                lanes = slice(half * DK, (half + 1) * DK)
                s_new = a_col * s_ref[b, hh, :, lanes] + k_col * v_row[:, lanes]
                snew_ref[b, hh, :, lanes] = s_new
                halves.append(jnp.sum(q_col * s_new, axis=0, keepdims=True))
            o_ref[b:b + 1, vcols] = jnp.concatenate(halves, axis=1)


def _even_decode_out_kernel(x_ref, o_ref, sg_ref, y_ref, gng_ref, wout_ref, x1_ref):
    gng = gng_ref[...]
    parts = [_head_rmsnorm(o_ref[:, hh * DV:(hh + 1) * DV], gng) for hh in range(HEADS)]
    o_mix = (jnp.concatenate(parts, axis=1) * sg_ref[...]).astype(BF16)
    out = _dot(o_mix, wout_ref[0:V_WIDTH, :]) + _dot(y_ref[...].astype(BF16), wout_ref[V_WIDTH:V_WIDTH + CONV_W, :])
    x1_ref[...] = x_ref[...] + out


def _odd_decode_kernel(x_ref, ng_ref, win_ref, bin_ref, wdw_ref, bdw_ref, lng_ref, lnb_ref, wout_ref,
                       bout_ref, fng_ref, cbuf_ref,
                       y_ref, cnew_ref):
    n_hist = CCONV_K - 1
    x = x_ref[...]
    h = _rmsnorm(x, ng_ref[...]).astype(BF16)
    a = _dot(h, win_ref[:, 0:CONV_W]) + bin_ref[:, 0:CONV_W]
    a_gate = _dot(h, win_ref[:, CONV_W:2 * CONV_W]) + bin_ref[:, CONV_W:2 * CONV_W]
    u = _gate(a, a_gate)
    yc = bdw_ref[...] + wdw_ref[n_hist:CCONV_K, :] * u
    for j in range(n_hist):
        tap = cbuf_ref[j]
        yc = yc + wdw_ref[j:j + 1, :] * tap
        if j >= 1:
            cnew_ref[j - 1] = tap
    cnew_ref[n_hist - 1] = u
    z = _dot(h, win_ref[:, 2 * CONV_W:3 * CONV_W]) + bin_ref[:, 2 * CONV_W:3 * CONV_W]
    act = _layernorm_act(yc, z, lng_ref[...], lnb_ref[...])
    y_ref[...] = _rmsnorm(x + _dot(act, wout_ref[...]) + bout_ref[...], fng_ref[...])


def _const_spec(shape):
    nd = len(shape)
    return pl.BlockSpec(shape, lambda *_: (0,) * nd, pipeline_mode=pl.Buffered(1))


def _row(v):
    return v.reshape(1, -1)


def kernel(x_prompt, x_sample, state_gla, state_sconv, state_cconv, norm_g, w_in_a, w_gate_up, b_gate_up, gla_norm_g, w_sconv, w_out_a, w_in_c, b_in_c, w_dwconv, b_dwconv, ln_g, ln_b, w_out_c, b_out_c, final_norm_g):
    bsz, seq, d = x_prompt.shape
    dec_b = x_sample.shape[0]
    assert d == D_MODEL and seq % EVEN_TILE == 0 and seq % ODD_TILE == 0 and x_sample.shape[1] == 1
    assert EVEN_TILE // MXU_K >= 2 and CONV_W == 4 * MXU_N and ODD_TILE >= DFT_HOP
    assert w_in_a.shape[0] == 1 and w_in_c.shape[0] == 1 and norm_g.shape[0] == 2

    assert w_in_a.shape[2] == MAIN_W
    wmain = w_in_a[0].T.astype(BF16)
    wup = jnp.pad(w_gate_up[0], ((0, GATE_RANK_PAD - GATE_RANK), (0, 0))).astype(BF16)
    bup = _row(b_gate_up[0])
    gng = _row(gla_norm_g[0])
    wsc = w_sconv[0]
    wout_a = w_out_a[0].astype(BF16)
    ng0 = _row(norm_g[0])
    ng1 = _row(norm_g[1])
    win_c = w_in_c[0].astype(BF16)
    bin_c = _row(b_in_c[0])
    wdw = w_dwconv[0]
    bdw = _row(b_dwconv[0])
    lng = _row(ln_g[0])
    lnb = _row(ln_b[0])
    wout_c = w_out_c[0].astype(BF16)
    bout = _row(b_out_c[0])
    fng = _row(final_norm_g)

    even_weights = (ng0, wmain, wup, bup)
    even_prompt_weights = even_weights + (gng, wsc, wout_a)
    odd_weights = (ng1, win_c, bin_c, wdw, bdw, lng, lnb, wout_c, bout, fng)
    odd_weight_specs = [_const_spec(w.shape) for w in odd_weights]
    hf = DFT_HALF
    assert CCONV_HALO >= CCONV_K - 1 and DFT_HOP % 16 == 0
    kk = np.arange(hf)[:, None]

    def packed_basis(pos):
        ang = 2.0 * np.pi * kk * pos[None, :] / DFT_N
        lower = -np.sin(ang)
        lower[0] = np.cos(np.pi * pos)
        return np.cos(ang), lower

    fc, fs = packed_basis(np.arange(DFT_N, dtype=np.float64))
    fwd = np.concatenate([fc, fs], axis=0).astype(np.float32)
    ic, isn = packed_basis(np.arange(CCONV_HALO, DFT_N, dtype=np.float64))
    weight = np.full((hf, 1), 2.0 / DFT_N)
    weight[0] = 1.0 / DFT_N
    inv = np.concatenate([weight * ic, weight * isn], axis=0).T.astype(np.float32)
    hc, hs = packed_basis(np.arange(CCONV_K, dtype=np.float64))
    hs_imag = hs.copy()
    hs_imag[0] = 0.0
    hc_alt = hc.copy()
    hc_alt[0] = hs[0]
    resp = np.concatenate([hc, hs_imag, hc_alt], axis=0).astype(np.float32)
    h_resp = jnp.dot(jnp.asarray(resp), wdw[::-1], precision=lax.Precision.HIGHEST)
    odd_prompt_weights = (ng1, win_c, bin_c, bdw, lng, lnb, wout_c, bout, fng,
                          jnp.asarray(fwd), jnp.asarray(inv), h_resp[:hf], h_resp[hf:2 * hf], h_resp[2 * hf:])

    prompt_params = pltpu.CompilerParams(dimension_semantics=("arbitrary", "arbitrary"),
                                         vmem_limit_bytes=VMEM_LIMIT)

    tm = EVEN_TILE
    tile_spec = pl.BlockSpec((None, tm, D_MODEL), lambda b, t: (b, t, 0))
    x1_p, gla_p, sconv_p = pl.pallas_call(
        _even_prompt_kernel,
        grid=(bsz, seq // tm),
        in_specs=[tile_spec] + [_const_spec(w.shape) for w in even_prompt_weights],
        out_specs=[tile_spec,
                   pl.BlockSpec((None, None, HEADS, DK, DV), lambda b, t: (0, b, 0, 0, 0)),
                   pl.BlockSpec((None, None, SCONV_K - 1, CONV_W), lambda b, t: (0, b, 0, 0))],
        out_shape=[jax.ShapeDtypeStruct((bsz, seq, D_MODEL), F32),
                   jax.ShapeDtypeStruct((1, bsz, HEADS, DK, DV), F32),
                   jax.ShapeDtypeStruct((1, bsz, SCONV_K - 1, CONV_W), F32)],
        scratch_shapes=[pltpu.VMEM((HEADS, DV, DK), F32),
                        pltpu.VMEM((tm + SCONV_HALO, CONV_W), F32),
                        pltpu.VMEM((tm, QK_WIDTH), BF16),
                        pltpu.VMEM((tm, QK_WIDTH), BF16),
                        pltpu.VMEM((tm, QK_WIDTH), BF16),
                        pltpu.VMEM((tm, V_WIDTH), BF16),
                        pltpu.VMEM((tm, QK_WIDTH), F32),
                        pltpu.VMEM((tm, V_WIDTH), F32),
                        pltpu.VMEM((tm, CONV_W), BF16)],
        compiler_params=prompt_params,
        name="even_prompt",
    )(x_prompt, *even_prompt_weights)

    tm = ODD_TILE
    tile_spec = pl.BlockSpec((None, tm, D_MODEL), lambda b, t: (b, t, 0))
    y_p, cconv_p = pl.pallas_call(
        _odd_prompt_kernel,
        grid=(bsz, seq // tm),
        in_specs=[tile_spec] + [_const_spec(w.shape) for w in odd_prompt_weights],
        out_specs=[tile_spec,
                   pl.BlockSpec((None, None, CCONV_K - 1, CONV_W), lambda b, t: (0, b, 0, 0))],
        out_shape=[jax.ShapeDtypeStruct((bsz, seq, D_MODEL), F32),
                   jax.ShapeDtypeStruct((1, bsz, CCONV_K - 1, CONV_W), F32)],
        scratch_shapes=[pltpu.VMEM((tm + CCONV_HALO, CONV_W), BF16),
                        pltpu.VMEM((tm, CONV_W), F32),
                        pltpu.VMEM((CCONV_HALO, CONV_W), F32)],
        compiler_params=prompt_params,
        name="odd_prompt",
    )(x1_p, *odd_prompt_weights)

    xs = x_sample.reshape(dec_b, D_MODEL)
    sbuf = state_sconv.reshape(dec_b, (SCONV_K - 1) * CONV_W)
    single = pltpu.CompilerParams(vmem_limit_bytes=VMEM_LIMIT)
    q_s, k_s, a_s, v_s, sg_s, ysc_s, sconv_s = pl.pallas_call(
        _even_decode_front_kernel,
        out_shape=[jax.ShapeDtypeStruct((dec_b, QK_WIDTH), F32),
                   jax.ShapeDtypeStruct((dec_b, QK_WIDTH), F32),
                   jax.ShapeDtypeStruct((dec_b, QK_WIDTH), F32),
                   jax.ShapeDtypeStruct((dec_b, V_WIDTH), F32),
                   jax.ShapeDtypeStruct((dec_b, V_WIDTH), F32),
                   jax.ShapeDtypeStruct((dec_b, CONV_W), F32),
                   jax.ShapeDtypeStruct((dec_b, (SCONV_K - 1) * CONV_W), F32)],
        compiler_params=single,
        name="even_decode_front",
    )(xs, *even_weights, wsc, sbuf)

    sb = DECODE_STATE_BLOCK
    assert dec_b % sb == 0
    vec_spec = lambda w: pl.BlockSpec((sb, w), lambda i: (i, 0))
    state_spec = pl.BlockSpec((sb, HEADS, DK, DV), lambda i: (i, 0, 0, 0))
    gla_s, o_s = pl.pallas_call(
        _gla_decode_kernel,
        grid=(dec_b // sb,),
        in_specs=[vec_spec(QK_WIDTH), vec_spec(QK_WIDTH), vec_spec(QK_WIDTH), vec_spec(V_WIDTH), state_spec],
        out_specs=[state_spec, vec_spec(V_WIDTH)],
        out_shape=[jax.ShapeDtypeStruct((dec_b, HEADS, DK, DV), F32),
                   jax.ShapeDtypeStruct((dec_b, V_WIDTH), F32)],
        compiler_params=pltpu.CompilerParams(dimension_semantics=("arbitrary",),
                                             vmem_limit_bytes=VMEM_LIMIT),
        name="gla_decode",
    )(q_s, k_s, a_s, v_s, state_gla[0])

    x1_s = pl.pallas_call(
        _even_decode_out_kernel,
        out_shape=jax.ShapeDtypeStruct((dec_b, D_MODEL), F32),
        compiler_params=single,
        name="even_decode_out",
    )(xs, o_s, sg_s, ysc_s, gng, wout_a)

    ob = DECODE_ODD_BLOCK
    assert dec_b % ob == 0
    rows_spec = pl.BlockSpec((ob, D_MODEL), lambda i: (i, 0))
    hist_spec = pl.BlockSpec((CCONV_K - 1, ob, CONV_W), lambda i: (0, i, 0))
    cbuf = jnp.transpose(state_cconv[0], (1, 0, 2))
    y_s, cconv_t = pl.pallas_call(
        _odd_decode_kernel,
        grid=(dec_b // ob,),
        in_specs=[rows_spec] + odd_weight_specs + [hist_spec],
        out_specs=[rows_spec, hist_spec],
        out_shape=[jax.ShapeDtypeStruct((dec_b, D_MODEL), F32),
                   jax.ShapeDtypeStruct((CCONV_K - 1, dec_b, CONV_W), F32)],
        compiler_params=pltpu.CompilerParams(dimension_semantics=("arbitrary",),
                                             vmem_limit_bytes=VMEM_LIMIT),
        name="odd_decode",
    )(x1_s, *odd_weights, cbuf)
    cconv_s = jnp.transpose(cconv_t, (1, 0, 2))[None]

    return (y_p,
            y_s.reshape(dec_b, 1, D_MODEL),
            gla_p,
            sconv_p,
            cconv_p,
            gla_s.reshape(1, dec_b, HEADS, DK, DV),
            sconv_s.reshape(1, dec_b, SCONV_K - 1, CONV_W),
            cconv_s)
```

```python
import jax
import jax.numpy as jnp
import numpy as np
from jax import lax
from jax.experimental import pallas as pl
from jax.experimental.pallas import tpu as pltpu

F32 = jnp.float32
BF16 = jnp.bfloat16

D_MODEL = 1024
HEADS = 4
DK = 128
DV = 256
QK_WIDTH = HEADS * DK
V_WIDTH = HEADS * DV
GATE_RANK = 16
GATE_RANK_PAD = 128
GATE_TEMP_INV = 1.0 / 16.0
CHUNK = 64
CHUNK_SHIFT = 6
SCONV_K = 3
CCONV_K = 31
CONV_W = 1024
RMS_EPS = 1e-6
LN_EPS = 1e-5
Q_SCALE = DK ** -0.5

COL_Q = 0
COL_K = COL_Q + QK_WIDTH
COL_V = COL_K + QK_WIDTH
COL_G = COL_V + V_WIDTH
COL_A_LOW = COL_G + V_WIDTH
COL_HB = COL_A_LOW + GATE_RANK
COL_GATE_B = COL_HB + CONV_W
COL_GATE_C = COL_GATE_B + CONV_W
COL_ZB = COL_GATE_C + CONV_W
MAIN_W = COL_ZB + CONV_W

MXU_K = 256
MXU_N = 256
EVEN_TILE = 512
ODD_TILE = 1024
CCONV_HALO = 32
SCONV_HALO = 8
DFT_N = MXU_K
DFT_HALF = DFT_N // 2
DFT_HOP = DFT_N - CCONV_HALO
DECODE_STATE_BLOCK = 16
DECODE_ODD_BLOCK = 64
VMEM_LIMIT = 60 * 1024 * 1024


def _dot(a, b):
    return jnp.dot(a, b, preferred_element_type=F32)


def _dot_nt(a, b):
    return lax.dot_general(a, b, (((1,), (1,)), ((), ())), preferred_element_type=F32)


def _dot_tn(a, b):
    return lax.dot_general(a, b, (((0,), (0,)), ((), ())), preferred_element_type=F32)


def _proj(h, wt_ref, lo, hi):
    return _dot_nt(h, wt_ref[lo:hi, :])


def _rmsnorm(x, g):
    ms = jnp.mean(x * x, axis=-1, keepdims=True)
    return x * lax.rsqrt(ms + RMS_EPS) * g


def _gate(x, y):
    return x / (1.0 + jnp.exp(-y))


def _silu(x):
    return _gate(x, x)


def _log_sigmoid(x):
    return -(jnp.maximum(-x, 0.0) + jnp.log(1.0 + jnp.exp(-jnp.abs(x))))


def _log_decay(h, wmain_ref, wup_ref, bup_ref):
    a_low = _proj(h, wmain_ref, COL_A_LOW, COL_A_LOW + GATE_RANK_PAD).astype(BF16)
    logit = _dot(a_low, wup_ref[...]) + bup_ref[...]
    return _log_sigmoid(logit) * GATE_TEMP_INV


def _head_rmsnorm(o, gng):
    ms = jnp.mean(o * o, axis=-1, keepdims=True)
    return o * lax.rsqrt(ms + RMS_EPS) * gng


def _layernorm_act(yc, z, lng, lnb):
    mu = jnp.mean(yc, axis=-1, keepdims=True)
    xc = yc - mu
    var = jnp.mean(xc * xc, axis=-1, keepdims=True)
    yn = xc * lax.rsqrt(var + LN_EPS) * lng + lnb
    return (_silu(yn) * _silu(z)).astype(BF16)


def _short_conv_gate(u, prev1, prev2, gate_b, z_b, wsc_ref):
    y = wsc_ref[2:3, :] * u + wsc_ref[1:2, :] * prev1 + wsc_ref[0:1, :] * prev2
    return gate_b * y * _silu(z_b)


def _even_prompt_kernel(x_ref, ng_ref, wmain_ref, wup_ref, bup_ref, gng_ref, wsc_ref, wout_ref,
                        x1_ref, sgla_ref, sconv_ref,
                        st_ref, ubuf_ref, qe_ref, ke_ref, kd_ref, v_ref, dec_ref, mix_ref, ysc_ref):
    tm = EVEN_TILE
    t = pl.program_id(1)
    last_t = pl.num_programs(1) - 1

    @pl.when(t == 0)
    def _():
        st_ref[...] = jnp.zeros_like(st_ref)
        ubuf_ref[0:SCONV_HALO, :] = jnp.zeros((SCONV_HALO, CONV_W), F32)

    x = x_ref[...]
    h = _rmsnorm(x, ng_ref[...]).astype(BF16)

    def short_conv_group(g0):
        cols = slice(g0, g0 + MXU_N)
        part = lambda c0: _proj(h, wmain_ref, c0 + g0, c0 + g0 + MXU_N)
        u = part(COL_GATE_C) * part(COL_HB)
        ubuf_ref[SCONV_HALO:SCONV_HALO + tm, cols] = u
        y = (wsc_ref[2:3, cols] * u + wsc_ref[1:2, cols] * ubuf_ref[pl.ds(SCONV_HALO - 1, tm), cols]
             + wsc_ref[0:1, cols] * ubuf_ref[pl.ds(SCONV_HALO - 2, tm), cols])
        ysc_ref[:, cols] = (part(COL_GATE_B) * y * _silu(part(COL_ZB))).astype(BF16)

    q = _proj(h, wmain_ref, COL_Q, COL_K) * Q_SCALE
    k = _proj(h, wmain_ref, COL_K, COL_V)
    v_ref[...] = _proj(h, wmain_ref, COL_V, COL_G).astype(BF16)
    a_low = _proj(h, wmain_ref, COL_A_LOW, COL_A_LOW + GATE_RANK_PAD).astype(BF16)
    short_conv_group(0)
    log_a = _log_sigmoid(_dot(a_low, wup_ref[...]) + bup_ref[...]) * GATE_TEMP_INV
    short_conv_group(MXU_N)

    row = lax.broadcasted_iota(jnp.int32, (MXU_K, MXU_K), 0)
    col = lax.broadcasted_iota(jnp.int32, (MXU_K, MXU_K), 1)
    in_chunk_causal = ((row >> CHUNK_SHIFT) == (col >> CHUNK_SHIFT)) & (col <= row)
    tri = jnp.where(in_chunk_causal, 1.0, 0.0).astype(BF16)
    la_hi = log_a.astype(BF16)
    la_lo = (log_a - la_hi.astype(F32)).astype(BF16)
    for sb in range(tm // MXU_K):
        rows = slice(sb * MXU_K, (sb + 1) * MXU_K)
        b_cum = _dot(tri, la_hi[rows, :]) + _dot(tri, la_lo[rows, :])
        b_tot = jnp.concatenate(
            [jnp.broadcast_to(b_cum[(c + 1) * CHUNK - 1:(c + 1) * CHUNK, :], (CHUNK, QK_WIDTH))
             for c in range(MXU_K // CHUNK)], axis=0)
        qe_ref[rows, :] = (q[rows, :] * jnp.exp(b_cum)).astype(BF16)
        ke_ref[rows, :] = (k[rows, :] * jnp.exp(-b_cum)).astype(BF16)
        kd_ref[rows, :] = (k[rows, :] * jnp.exp(b_tot - b_cum)).astype(BF16)
        dec_ref[rows, :] = jnp.exp(b_tot)
        if sb < 2:
            short_conv_group((2 + sb) * MXU_N)

    gng = gng_ref[...]
    for hh in range(HEADS):
        kcols = slice(hh * DK, (hh + 1) * DK)
        vcols = slice(hh * DV, (hh + 1) * DV)
        st = st_ref[hh]
        for sb in range(tm // MXU_K):
            rows = slice(sb * MXU_K, (sb + 1) * MXU_K)
            sc = jnp.where(in_chunk_causal, _dot_nt(qe_ref[rows, kcols], ke_ref[rows, kcols]), 0.0)
            o_intra = _dot(sc.astype(BF16), v_ref[rows, vcols])
            for c in range(MXU_K // CHUNK):
                r0 = sb * MXU_K + c * CHUNK
                crow = slice(r0, r0 + CHUNK)
                o = o_intra[c * CHUNK:(c + 1) * CHUNK, :] + _dot_nt(qe_ref[crow, kcols], st.astype(BF16))
                mix_ref[crow, vcols] = _head_rmsnorm(o, gng)
                dec = dec_ref[r0:r0 + 1, kcols]
                st = st * dec + _dot_tn(v_ref[crow, vcols], kd_ref[crow, kcols])
        st_ref[hh] = st

    o_mix = (mix_ref[...] * _silu(_proj(h, wmain_ref, COL_G, COL_A_LOW))).astype(BF16)

    ubuf_ref[0:SCONV_HALO, :] = ubuf_ref[tm:tm + SCONV_HALO, :]

    out = _dot(o_mix, wout_ref[0:V_WIDTH, :]) + _dot(ysc_ref[...], wout_ref[V_WIDTH:V_WIDTH + CONV_W, :])
    x1_ref[...] = x + out

    @pl.when(t == last_t)
    def _():
        for hh in range(HEADS):
            sgla_ref[hh] = st_ref[hh].T
        sconv_ref[...] = ubuf_ref[pl.ds(SCONV_HALO + tm - (SCONV_K - 1), SCONV_K - 1), :]


def _odd_prompt_kernel(x_ref, ng_ref, win_ref, bin_ref, bdw_ref, lng_ref, lnb_ref, wout_ref,
                       bout_ref, fng_ref, fwd_ref, inv_ref, ha_ref, hb_ref, ha2_ref,
                       y_ref, cconv_ref,
                       ub_ref, yc_ref, tail_ref, z_ref):
    tm = ODD_TILE
    t = pl.program_id(1)
    last_t = pl.num_programs(1) - 1

    @pl.when(t == 0)
    def _():
        ub_ref[0:CCONV_HALO, :] = jnp.zeros((CCONV_HALO, CONV_W), BF16)

    x = x_ref[...]
    h = _rmsnorm(x, ng_ref[...]).astype(BF16)
    for g0 in range(0, CONV_W, MXU_N):
        cols = slice(g0, g0 + MXU_N)
        gcols = slice(CONV_W + g0, CONV_W + g0 + MXU_N)
        u = _gate(_dot(h, win_ref[:, cols]) + bin_ref[:, cols], _dot(h, win_ref[:, gcols]) + bin_ref[:, gcols])
        tail_ref[:, cols] = u[tm - CCONV_HALO:, :]
        ub_ref[CCONV_HALO:CCONV_HALO + tm, cols] = u.astype(BF16)

    @pl.when(t == last_t)
    def _():
        cconv_ref[...] = tail_ref[CCONV_HALO - (CCONV_K - 1):, :]

    bdw = bdw_ref[...]
    ha = ha_ref[...]
    hb = hb_ref[...]
    ha2 = ha2_ref[...]
    fwd = fwd_ref[...].astype(BF16)
    inv = inv_ref[...].astype(BF16)
    starts = sorted({min(s0, tm - DFT_HOP) for s0 in range(0, tm, DFT_HOP)})
    z_groups = list(range(0, CONV_W, MXU_N))
    for i, start in enumerate(starts):
        spec = _dot(fwd, ub_ref[start:start + DFT_N, :])
        p = spec[0:DFT_HALF, :]
        q = spec[DFT_HALF:DFT_N, :]
        prod = jnp.concatenate([p * ha - q * hb, p * hb + q * ha2], axis=0).astype(BF16)
        yc_ref[start:start + DFT_HOP, :] = _dot(inv, prod) + bdw
        if i < len(z_groups):
            zc = slice(2 * CONV_W + z_groups[i], 2 * CONV_W + z_groups[i] + MXU_N)
            z_ref[:, z_groups[i]:z_groups[i] + MXU_N] = _dot(h, win_ref[:, zc]) + bin_ref[:, zc]
    assert len(starts) >= len(z_groups)

    ub_ref[0:CCONV_HALO, :] = ub_ref[tm:tm + CCONV_HALO, :]

    act = _layernorm_act(yc_ref[...], z_ref[...], lng_ref[...], lnb_ref[...])
    y_ref[...] = _rmsnorm(x + _dot(act, wout_ref[...]) + bout_ref[...], fng_ref[...])


def _even_decode_front_kernel(x_ref, ng_ref, wmain_ref, wup_ref, bup_ref, wsc_ref, sbuf_ref,
                              q_ref, k_ref, a_ref, v_ref, sg_ref, y_ref, snew_ref):
    h = _rmsnorm(x_ref[...], ng_ref[...]).astype(BF16)
    q_ref[...] = _proj(h, wmain_ref, COL_Q, COL_K) * Q_SCALE
    k_ref[...] = _proj(h, wmain_ref, COL_K, COL_V)
    v_ref[...] = _proj(h, wmain_ref, COL_V, COL_G)
    sg_ref[...] = _silu(_proj(h, wmain_ref, COL_G, COL_A_LOW))
    a_ref[...] = jnp.exp(_log_decay(h, wmain_ref, wup_ref, bup_ref))
    u = _proj(h, wmain_ref, COL_GATE_C, COL_ZB) * _proj(h, wmain_ref, COL_HB, COL_GATE_B)
    prev2 = sbuf_ref[:, 0:CONV_W]
    prev1 = sbuf_ref[:, CONV_W:2 * CONV_W]
    y_ref[...] = _short_conv_gate(u, prev1, prev2, _proj(h, wmain_ref, COL_GATE_B, COL_GATE_C),
                                  _proj(h, wmain_ref, COL_ZB, MAIN_W), wsc_ref)
    snew_ref[:, 0:CONV_W] = prev1
    snew_ref[:, CONV_W:2 * CONV_W] = u


def _lane_bcast_column(row):
    return jnp.broadcast_to(row, (DK, DK)).T


def _gla_decode_kernel(q_ref, k_ref, a_ref, v_ref, s_ref, snew_ref, o_ref):
    for b in range(DECODE_STATE_BLOCK):
        for hh in range(HEADS):
            kcols = slice(hh * DK, (hh + 1) * DK)
            vcols = slice(hh * DV, (hh + 1) * DV)
            a_col = _lane_bcast_column(a_ref[b:b + 1, kcols])
            k_col = _lane_bcast_column(k_ref[b:b + 1, kcols])
            q_col = _lane_bcast_column(q_ref[b:b + 1, kcols])
            v_row = v_ref[b:b + 1, vcols]
            halves = []
            for half in range(DV // DK):
                lanes = slice(half * DK, (half + 1) * DK)
                s_new = a_col * s_ref[b, hh, :, lanes] + k_col * v_row[:, lanes]
                snew_ref[b, hh, :, lanes] = s_new
                halves.append(jnp.sum(q_col * s_new, axis=0, keepdims=True))
            o_ref[b:b + 1, vcols] = jnp.concatenate(halves, axis=1)


def _even_decode_out_kernel(x_ref, o_ref, sg_ref, y_ref, gng_ref, wout_ref, x1_ref):
    gng = gng_ref[...]
    parts = [_head_rmsnorm(o_ref[:, hh * DV:(hh + 1) * DV], gng) for hh in range(HEADS)]
    o_mix = (jnp.concatenate(parts, axis=1) * sg_ref[...]).astype(BF16)
    out = _dot(o_mix, wout_ref[0:V_WIDTH, :]) + _dot(y_ref[...].astype(BF16), wout_ref[V_WIDTH:V_WIDTH + CONV_W, :])
    x1_ref[...] = x_ref[...] + out


def _odd_decode_kernel(x_ref, ng_ref, win_ref, bin_ref, wdw_ref, bdw_ref, lng_ref, lnb_ref, wout_ref,
                       bout_ref, fng_ref, cbuf_ref,
                       y_ref, cnew_ref):
    n_hist = CCONV_K - 1
    x = x_ref[...]
    h = _rmsnorm(x, ng_ref[...]).astype(BF16)
    a = _dot(h, win_ref[:, 0:CONV_W]) + bin_ref[:, 0:CONV_W]
    a_gate = _dot(h, win_ref[:, CONV_W:2 * CONV_W]) + bin_ref[:, CONV_W:2 * CONV_W]
    u = _gate(a, a_gate)
    yc = bdw_ref[...] + wdw_ref[n_hist:CCONV_K, :] * u
    for j in range(n_hist):
        tap = cbuf_ref[j]
        yc = yc + wdw_ref[j:j + 1, :] * tap
        if j >= 1:
            cnew_ref[j - 1] = tap
    cnew_ref[n_hist - 1] = u
    z = _dot(h, win_ref[:, 2 * CONV_W:3 * CONV_W]) + bin_ref[:, 2 * CONV_W:3 * CONV_W]
    act = _layernorm_act(yc, z, lng_ref[...], lnb_ref[...])
    y_ref[...] = _rmsnorm(x + _dot(act, wout_ref[...]) + bout_ref[...], fng_ref[...])


def _const_spec(shape):
    nd = len(shape)
    return pl.BlockSpec(shape, lambda *_: (0,) * nd, pipeline_mode=pl.Buffered(1))


def _row(v):
    return v.reshape(1, -1)


def kernel(x_prompt, x_sample, state_gla, state_sconv, state_cconv, norm_g, w_in_a, w_gate_up, b_gate_up, gla_norm_g, w_sconv, w_out_a, w_in_c, b_in_c, w_dwconv, b_dwconv, ln_g, ln_b, w_out_c, b_out_c, final_norm_g):
    bsz, seq, d = x_prompt.shape
    dec_b = x_sample.shape[0]
    assert d == D_MODEL and seq % EVEN_TILE == 0 and seq % ODD_TILE == 0 and x_sample.shape[1] == 1
    assert EVEN_TILE // MXU_K >= 2 and CONV_W == 4 * MXU_N and ODD_TILE >= DFT_HOP
    assert w_in_a.shape[0] == 1 and w_in_c.shape[0] == 1 and norm_g.shape[0] == 2

    assert w_in_a.shape[2] == MAIN_W
    wmain = w_in_a[0].T.astype(BF16)
    wup = jnp.pad(w_gate_up[0], ((0, GATE_RANK_PAD - GATE_RANK), (0, 0))).astype(BF16)
    bup = _row(b_gate_up[0])
    gng = _row(gla_norm_g[0])
    wsc = w_sconv[0]
    wout_a = w_out_a[0].astype(BF16)
    ng0 = _row(norm_g[0])
    ng1 = _row(norm_g[1])
    win_c = w_in_c[0].astype(BF16)
    bin_c = _row(b_in_c[0])
    wdw = w_dwconv[0]
    bdw = _row(b_dwconv[0])
    lng = _row(ln_g[0])
    lnb = _row(ln_b[0])
    wout_c = w_out_c[0].astype(BF16)
    bout = _row(b_out_c[0])
    fng = _row(final_norm_g)

    even_weights = (ng0, wmain, wup, bup)
    even_prompt_weights = even_weights + (gng, wsc, wout_a)
    odd_weights = (ng1, win_c, bin_c, wdw, bdw, lng, lnb, wout_c, bout, fng)
    odd_weight_specs = [_const_spec(w.shape) for w in odd_weights]
    hf = DFT_HALF
    assert CCONV_HALO >= CCONV_K - 1 and DFT_HOP % 16 == 0
    kk = np.arange(hf)[:, None]

    def packed_basis(pos):
        ang = 2.0 * np.pi * kk * pos[None, :] / DFT_N
        lower = -np.sin(ang)
        lower[0] = np.cos(np.pi * pos)
        return np.cos(ang), lower

    fc, fs = packed_basis(np.arange(DFT_N, dtype=np.float64))
    fwd = np.concatenate([fc, fs], axis=0).astype(np.float32)
    ic, isn = packed_basis(np.arange(CCONV_HALO, DFT_N, dtype=np.float64))
    weight = np.full((hf, 1), 2.0 / DFT_N)
    weight[0] = 1.0 / DFT_N
    inv = np.concatenate([weight * ic, weight * isn], axis=0).T.astype(np.float32)
    hc, hs = packed_basis(np.arange(CCONV_K, dtype=np.float64))
    hs_imag = hs.copy()
    hs_imag[0] = 0.0
    hc_alt = hc.copy()
    hc_alt[0] = hs[0]
    resp = np.concatenate([hc, hs_imag, hc_alt], axis=0).astype(np.float32)
    h_resp = jnp.dot(jnp.asarray(resp), wdw[::-1], precision=lax.Precision.HIGHEST)
    odd_prompt_weights = (ng1, win_c, bin_c, bdw, lng, lnb, wout_c, bout, fng,
                          jnp.asarray(fwd), jnp.asarray(inv), h_resp[:hf], h_resp[hf:2 * hf], h_resp[2 * hf:])

    prompt_params = pltpu.CompilerParams(dimension_semantics=("arbitrary", "arbitrary"),
                                         vmem_limit_bytes=VMEM_LIMIT)

    tm = EVEN_TILE
    tile_spec = pl.BlockSpec((None, tm, D_MODEL), lambda b, t: (b, t, 0))
    x1_p, gla_p, sconv_p = pl.pallas_call(
        _even_prompt_kernel,
        grid=(bsz, seq // tm),
        in_specs=[tile_spec] + [_const_spec(w.shape) for w in even_prompt_weights],
        out_specs=[tile_spec,
                   pl.BlockSpec((None, None, HEADS, DK, DV), lambda b, t: (0, b, 0, 0, 0)),
                   pl.BlockSpec((None, None, SCONV_K - 1, CONV_W), lambda b, t: (0, b, 0, 0))],
        out_shape=[jax.ShapeDtypeStruct((bsz, seq, D_MODEL), F32),
                   jax.ShapeDtypeStruct((1, bsz, HEADS, DK, DV), F32),
                   jax.ShapeDtypeStruct((1, bsz, SCONV_K - 1, CONV_W), F32)],
        scratch_shapes=[pltpu.VMEM((HEADS, DV, DK), F32),
                        pltpu.VMEM((tm + SCONV_HALO, CONV_W), F32),
                        pltpu.VMEM((tm, QK_WIDTH), BF16),
                        pltpu.VMEM((tm, QK_WIDTH), BF16),
                        pltpu.VMEM((tm, QK_WIDTH), BF16),
                        pltpu.VMEM((tm, V_WIDTH), BF16),
                        pltpu.VMEM((tm, QK_WIDTH), F32),
                        pltpu.VMEM((tm, V_WIDTH), F32),
                        pltpu.VMEM((tm, CONV_W), BF16)],
        compiler_params=prompt_params,
        name="even_prompt",
    )(x_prompt, *even_prompt_weights)

    tm = ODD_TILE
    tile_spec = pl.BlockSpec((None, tm, D_MODEL), lambda b, t: (b, t, 0))
    y_p, cconv_p = pl.pallas_call(
        _odd_prompt_kernel,
        grid=(bsz, seq // tm),
        in_specs=[tile_spec] + [_const_spec(w.shape) for w in odd_prompt_weights],
        out_specs=[tile_spec,
                   pl.BlockSpec((None, None, CCONV_K - 1, CONV_W), lambda b, t: (0, b, 0, 0))],
        out_shape=[jax.ShapeDtypeStruct((bsz, seq, D_MODEL), F32),
                   jax.ShapeDtypeStruct((1, bsz, CCONV_K - 1, CONV_W), F32)],
        scratch_shapes=[pltpu.VMEM((tm + CCONV_HALO, CONV_W), BF16),
                        pltpu.VMEM((tm, CONV_W), F32),
                        pltpu.VMEM((CCONV_HALO, CONV_W), F32),
                        pltpu.VMEM((tm, CONV_W), F32)],
        compiler_params=prompt_params,
        name="odd_prompt",
    )(x1_p, *odd_prompt_weights)

    xs = x_sample.reshape(dec_b, D_MODEL)
    sbuf = state_sconv.reshape(dec_b, (SCONV_K - 1) * CONV_W)
    single = pltpu.CompilerParams(vmem_limit_bytes=VMEM_LIMIT)
    q_s, k_s, a_s, v_s, sg_s, ysc_s, sconv_s = pl.pallas_call(
        _even_decode_front_kernel,
        out_shape=[jax.ShapeDtypeStruct((dec_b, QK_WIDTH), F32),
                   jax.ShapeDtypeStruct((dec_b, QK_WIDTH), F32),
                   jax.ShapeDtypeStruct((dec_b, QK_WIDTH), F32),
                   jax.ShapeDtypeStruct((dec_b, V_WIDTH), F32),
                   jax.ShapeDtypeStruct((dec_b, V_WIDTH), F32),
                   jax.ShapeDtypeStruct((dec_b, CONV_W), F32),
                   jax.ShapeDtypeStruct((dec_b, (SCONV_K - 1) * CONV_W), F32)],
        compiler_params=single,
        name="even_decode_front",
    )(xs, *even_weights, wsc, sbuf)

    sb = DECODE_STATE_BLOCK
    assert dec_b % sb == 0
    vec_spec = lambda w: pl.BlockSpec((sb, w), lambda i: (i, 0))
    state_spec = pl.BlockSpec((sb, HEADS, DK, DV), lambda i: (i, 0, 0, 0))
    gla_s, o_s = pl.pallas_call(
        _gla_decode_kernel,
        grid=(dec_b // sb,),
        in_specs=[vec_spec(QK_WIDTH), vec_spec(QK_WIDTH), vec_spec(QK_WIDTH), vec_spec(V_WIDTH), state_spec],
        out_specs=[state_spec, vec_spec(V_WIDTH)],
        out_shape=[jax.ShapeDtypeStruct((dec_b, HEADS, DK, DV), F32),
                   jax.ShapeDtypeStruct((dec_b, V_WIDTH), F32)],
        compiler_params=pltpu.CompilerParams(dimension_semantics=("arbitrary",),
                                             vmem_limit_bytes=VMEM_LIMIT),
        name="gla_decode",
    )(q_s, k_s, a_s, v_s, state_gla[0])

    x1_s = pl.pallas_call(
        _even_decode_out_kernel,
        out_shape=jax.ShapeDtypeStruct((dec_b, D_MODEL), F32),
        compiler_params=single,
        name="even_decode_out",
    )(xs, o_s, sg_s, ysc_s, gng, wout_a)

    ob = DECODE_ODD_BLOCK
    assert dec_b % ob == 0
    rows_spec = pl.BlockSpec((ob, D_MODEL), lambda i: (i, 0))
    hist_spec = pl.BlockSpec((CCONV_K - 1, ob, CONV_W), lambda i: (0, i, 0))
    cbuf = jnp.transpose(state_cconv[0], (1, 0, 2))
    y_s, cconv_t = pl.pallas_call(
        _odd_decode_kernel,
        grid=(dec_b // ob,),
        in_specs=[rows_spec] + odd_weight_specs + [hist_spec],
        out_specs=[rows_spec, hist_spec],
        out_shape=[jax.ShapeDtypeStruct((dec_b, D_MODEL), F32),
                   jax.ShapeDtypeStruct((CCONV_K - 1, dec_b, CONV_W), F32)],
        compiler_params=pltpu.CompilerParams(dimension_semantics=("arbitrary",),
                                             vmem_limit_bytes=VMEM_LIMIT),
        name="odd_decode",
    )(x1_s, *odd_weights, cbuf)
    cconv_s = jnp.transpose(cconv_t, (1, 0, 2))[None]

    return (y_p,
            y_s.reshape(dec_b, 1, D_MODEL),
            gla_p,
            sconv_p,
            cconv_p,
            gla_s.reshape(1, dec_b, HEADS, DK, DV),
            sconv_s.reshape(1, dec_b, SCONV_K - 1, CONV_W),
            cconv_s)
```

```python
import jax
import jax.numpy as jnp
import numpy as np
from jax import lax
from jax.experimental import pallas as pl
from jax.experimental.pallas import tpu as pltpu

F32 = jnp.float32
BF16 = jnp.bfloat16

D_MODEL = 1024
HEADS = 4
DK = 128
DV = 256
QK_WIDTH = HEADS * DK
V_WIDTH = HEADS * DV
GATE_RANK = 16
GATE_RANK_PAD = 128
GATE_TEMP_INV = 1.0 / 16.0
CHUNK = 64
CHUNK_SHIFT = 6
SCONV_K = 3
CCONV_K = 31
CONV_W = 1024
RMS_EPS = 1e-6
LN_EPS = 1e-5
Q_SCALE = DK ** -0.5

COL_Q = 0
COL_K = COL_Q + QK_WIDTH
COL_V = COL_K + QK_WIDTH
COL_G = COL_V + V_WIDTH
COL_A_LOW = COL_G + V_WIDTH
COL_HB = COL_A_LOW + GATE_RANK
COL_GATE_B = COL_HB + CONV_W
COL_GATE_C = COL_GATE_B + CONV_W
COL_ZB = COL_GATE_C + CONV_W
MAIN_W = COL_ZB + CONV_W

MXU_K = 256
MXU_N = 256
EVEN_TILE = 512
ODD_TILE = 1024
ODD_FINISH_ROWS = 256
CCONV_HALO = 32
SCONV_HALO = 8
DFT_N = MXU_K
DFT_HALF = DFT_N // 2
DFT_HOP = DFT_N - CCONV_HALO
DECODE_STATE_BLOCK = 16
DECODE_ODD_BLOCK = 64
VMEM_LIMIT = 60 * 1024 * 1024


def _dot(a, b):
    return jnp.dot(a, b, preferred_element_type=F32)


def _dot_nt(a, b):
    return lax.dot_general(a, b, (((1,), (1,)), ((), ())), preferred_element_type=F32)


def _dot_tn(a, b):
    return lax.dot_general(a, b, (((0,), (0,)), ((), ())), preferred_element_type=F32)


def _proj(h, wt_ref, lo, hi):
    return _dot_nt(h, wt_ref[lo:hi, :])


def _rmsnorm(x, g):
    ms = jnp.mean(x * x, axis=-1, keepdims=True)
    return x * lax.rsqrt(ms + RMS_EPS) * g


def _gate(x, y):
    return x / (1.0 + jnp.exp(-y))


def _silu(x):
    return _gate(x, x)


def _log_sigmoid(x):
    return -(jnp.maximum(-x, 0.0) + jnp.log(1.0 + jnp.exp(-jnp.abs(x))))


def _log_decay(h, wmain_ref, wup_ref, bup_ref):
    a_low = _proj(h, wmain_ref, COL_A_LOW, COL_A_LOW + GATE_RANK_PAD).astype(BF16)
    logit = _dot(a_low, wup_ref[...]) + bup_ref[...]
    return _log_sigmoid(logit) * GATE_TEMP_INV


def _head_rmsnorm(o, gng):
    ms = jnp.mean(o * o, axis=-1, keepdims=True)
    return o * lax.rsqrt(ms + RMS_EPS) * gng


def _layernorm_act(yc, z, lng, lnb):
    mu = jnp.mean(yc, axis=-1, keepdims=True)
    xc = yc - mu
    var = jnp.mean(xc * xc, axis=-1, keepdims=True)
    yn = xc * lax.rsqrt(var + LN_EPS) * lng + lnb
    return (_silu(yn) * _silu(z)).astype(BF16)


def _short_conv_gate(u, prev1, prev2, gate_b, z_b, wsc_ref):
    y = wsc_ref[2:3, :] * u + wsc_ref[1:2, :] * prev1 + wsc_ref[0:1, :] * prev2
    return gate_b * y * _silu(z_b)


def _even_prompt_kernel(x_ref, ng_ref, wmain_ref, wup_ref, bup_ref, gng_ref, wsc_ref, wout_ref,
                        x1_ref, sgla_ref, sconv_ref,
                        st_ref, ubuf_ref, qe_ref, ke_ref, kd_ref, v_ref, dec_ref, mix_ref, ysc_ref):
    tm = EVEN_TILE
    t = pl.program_id(1)
    last_t = pl.num_programs(1) - 1

    @pl.when(t == 0)
    def _():
        st_ref[...] = jnp.zeros_like(st_ref)
        ubuf_ref[0:SCONV_HALO, :] = jnp.zeros((SCONV_HALO, CONV_W), F32)

    x = x_ref[...]
    h = _rmsnorm(x, ng_ref[...]).astype(BF16)

    def short_conv_group(g0):
        cols = slice(g0, g0 + MXU_N)
        part = lambda c0: _proj(h, wmain_ref, c0 + g0, c0 + g0 + MXU_N)
        u = part(COL_GATE_C) * part(COL_HB)
        ubuf_ref[SCONV_HALO:SCONV_HALO + tm, cols] = u
        y = (wsc_ref[2:3, cols] * u + wsc_ref[1:2, cols] * ubuf_ref[pl.ds(SCONV_HALO - 1, tm), cols]
             + wsc_ref[0:1, cols] * ubuf_ref[pl.ds(SCONV_HALO - 2, tm), cols])
        ysc_ref[:, cols] = (part(COL_GATE_B) * y * _silu(part(COL_ZB))).astype(BF16)

    q = _proj(h, wmain_ref, COL_Q, COL_K) * Q_SCALE
    k = _proj(h, wmain_ref, COL_K, COL_V)
    v_ref[...] = _proj(h, wmain_ref, COL_V, COL_G).astype(BF16)
    a_low = _proj(h, wmain_ref, COL_A_LOW, COL_A_LOW + GATE_RANK_PAD).astype(BF16)
    short_conv_group(0)
    log_a = _log_sigmoid(_dot(a_low, wup_ref[...]) + bup_ref[...]) * GATE_TEMP_INV
    short_conv_group(MXU_N)

    row = lax.broadcasted_iota(jnp.int32, (MXU_K, MXU_K), 0)
    col = lax.broadcasted_iota(jnp.int32, (MXU_K, MXU_K), 1)
    in_chunk_causal = ((row >> CHUNK_SHIFT) == (col >> CHUNK_SHIFT)) & (col <= row)
    tri = jnp.where(in_chunk_causal, 1.0, 0.0).astype(BF16)
    la_hi = log_a.astype(BF16)
    la_lo = (log_a - la_hi.astype(F32)).astype(BF16)
    for sb in range(tm // MXU_K):
        rows = slice(sb * MXU_K, (sb + 1) * MXU_K)
        b_cum = _dot(tri, la_hi[rows, :]) + _dot(tri, la_lo[rows, :])
        b_tot = jnp.concatenate(
            [jnp.broadcast_to(b_cum[(c + 1) * CHUNK - 1:(c + 1) * CHUNK, :], (CHUNK, QK_WIDTH))
             for c in range(MXU_K // CHUNK)], axis=0)
        qe_ref[rows, :] = (q[rows, :] * jnp.exp(b_cum)).astype(BF16)
        ke_ref[rows, :] = (k[rows, :] * jnp.exp(-b_cum)).astype(BF16)
        kd_ref[rows, :] = (k[rows, :] * jnp.exp(b_tot - b_cum)).astype(BF16)
        dec_ref[rows, :] = jnp.exp(b_tot)
        if sb < 2:
            short_conv_group((2 + sb) * MXU_N)

    gng = gng_ref[...]
    for hh in range(HEADS):
        kcols = slice(hh * DK, (hh + 1) * DK)
        vcols = slice(hh * DV, (hh + 1) * DV)
        st = st_ref[hh]
        for sb in range(tm // MXU_K):
            rows = slice(sb * MXU_K, (sb + 1) * MXU_K)
            sc = jnp.where(in_chunk_causal, _dot_nt(qe_ref[rows, kcols], ke_ref[rows, kcols]), 0.0)
            o_intra = _dot(sc.astype(BF16), v_ref[rows, vcols])
            for c in range(MXU_K // CHUNK):
                r0 = sb * MXU_K + c * CHUNK
                crow = slice(r0, r0 + CHUNK)
                o = o_intra[c * CHUNK:(c + 1) * CHUNK, :] + _dot_nt(qe_ref[crow, kcols], st.astype(BF16))
                mix_ref[crow, vcols] = _head_rmsnorm(o, gng)
                dec = dec_ref[r0:r0 + 1, kcols]
                st = st * dec + _dot_tn(v_ref[crow, vcols], kd_ref[crow, kcols])
        st_ref[hh] = st

    o_mix = (mix_ref[...] * _silu(_proj(h, wmain_ref, COL_G, COL_A_LOW))).astype(BF16)

    ubuf_ref[0:SCONV_HALO, :] = ubuf_ref[tm:tm + SCONV_HALO, :]

    out = _dot(o_mix, wout_ref[0:V_WIDTH, :]) + _dot(ysc_ref[...], wout_ref[V_WIDTH:V_WIDTH + CONV_W, :])
    x1_ref[...] = x + out

    @pl.when(t == last_t)
    def _():
        for hh in range(HEADS):
            sgla_ref[hh] = st_ref[hh].T
        sconv_ref[...] = ubuf_ref[pl.ds(SCONV_HALO + tm - (SCONV_K - 1), SCONV_K - 1), :]


def _odd_prompt_kernel(x_ref, ng_ref, win_ref, bin_ref, bdw_ref, lng_ref, lnb_ref, wout_ref,
                       bout_ref, fng_ref, fwd_ref, inv_ref, ha_ref, hb_ref, ha2_ref,
                       y_ref, cconv_ref,
                       ub_ref, yc_ref, tail_ref, z_ref):
    tm = ODD_TILE
    t = pl.program_id(1)
    last_t = pl.num_programs(1) - 1

    @pl.when(t == 0)
    def _():
        ub_ref[0:CCONV_HALO, :] = jnp.zeros((CCONV_HALO, CONV_W), BF16)

    x = x_ref[...]
    h = _rmsnorm(x, ng_ref[...]).astype(BF16)
    for g0 in range(0, CONV_W, MXU_N):
        cols = slice(g0, g0 + MXU_N)
        gcols = slice(CONV_W + g0, CONV_W + g0 + MXU_N)
        u = _gate(_dot(h, win_ref[:, cols]) + bin_ref[:, cols], _dot(h, win_ref[:, gcols]) + bin_ref[:, gcols])
        tail_ref[:, cols] = u[tm - CCONV_HALO:, :]
        ub_ref[CCONV_HALO:CCONV_HALO + tm, cols] = u.astype(BF16)

    @pl.when(t == last_t)
    def _():
        cconv_ref[...] = tail_ref[CCONV_HALO - (CCONV_K - 1):, :]

    bdw = bdw_ref[...]
    ha = ha_ref[...]
    hb = hb_ref[...]
    ha2 = ha2_ref[...]
    fwd = fwd_ref[...].astype(BF16)
    inv = inv_ref[...].astype(BF16)
    starts = sorted({min(s0, tm - DFT_HOP) for s0 in range(0, tm, DFT_HOP)})
    z_groups = list(range(0, CONV_W, MXU_N))
    for i, start in enumerate(starts):
        spec = _dot(fwd, ub_ref[start:start + DFT_N, :])
        p = spec[0:DFT_HALF, :]
        q = spec[DFT_HALF:DFT_N, :]
        prod = jnp.concatenate([p * ha - q * hb, p * hb + q * ha2], axis=0).astype(BF16)
        yc_ref[start:start + DFT_HOP, :] = _dot(inv, prod) + bdw
        if i < len(z_groups):
            zc = slice(2 * CONV_W + z_groups[i], 2 * CONV_W + z_groups[i] + MXU_N)
            z_ref[:, z_groups[i]:z_groups[i] + MXU_N] = _dot(h, win_ref[:, zc]) + bin_ref[:, zc]
    assert len(starts) >= len(z_groups)

    ub_ref[0:CCONV_HALO, :] = ub_ref[tm:tm + CCONV_HALO, :]

    for r0 in range(0, tm, ODD_FINISH_ROWS):
        rows = slice(r0, r0 + ODD_FINISH_ROWS)
        act = _layernorm_act(yc_ref[rows, :], z_ref[rows, :], lng_ref[...], lnb_ref[...])
        y_ref[rows, :] = _rmsnorm(x_ref[rows, :] + _dot(act, wout_ref[...]) + bout_ref[...], fng_ref[...])


def _even_decode_front_kernel(x_ref, ng_ref, wmain_ref, wup_ref, bup_ref, wsc_ref, sbuf_ref,
                              q_ref, k_ref, a_ref, v_ref, sg_ref, y_ref, snew_ref):
    h = _rmsnorm(x_ref[...], ng_ref[...]).astype(BF16)
    q_ref[...] = _proj(h, wmain_ref, COL_Q, COL_K) * Q_SCALE
    k_ref[...] = _proj(h, wmain_ref, COL_K, COL_V)
    v_ref[...] = _proj(h, wmain_ref, COL_V, COL_G)
    sg_ref[...] = _silu(_proj(h, wmain_ref, COL_G, COL_A_LOW))
    a_ref[...] = jnp.exp(_log_decay(h, wmain_ref, wup_ref, bup_ref))
    u = _proj(h, wmain_ref, COL_GATE_C, COL_ZB) * _proj(h, wmain_ref, COL_HB, COL_GATE_B)
    prev2 = sbuf_ref[:, 0:CONV_W]
    prev1 = sbuf_ref[:, CONV_W:2 * CONV_W]
    y_ref[...] = _short_conv_gate(u, prev1, prev2, _proj(h, wmain_ref, COL_GATE_B, COL_GATE_C),
                                  _proj(h, wmain_ref, COL_ZB, MAIN_W), wsc_ref)
    snew_ref[:, 0:CONV_W] = prev1
    snew_ref[:, CONV_W:2 * CONV_W] = u


def _lane_bcast_column(row):
    return jnp.broadcast_to(row, (DK, DK)).T


def _gla_decode_kernel(q_ref, k_ref, a_ref, v_ref, s_ref, snew_ref, o_ref):
    for b in range(DECODE_STATE_BLOCK):
        for hh in range(HEADS):
            kcols = slice(hh * DK, (hh + 1) * DK)
            vcols = slice(hh * DV, (hh + 1) * DV)
            a_col = _lane_bcast_column(a_ref[b:b + 1, kcols])
            k_col = _lane_bcast_column(k_ref[b:b + 1, kcols])
            q_col = _lane_bcast_column(q_ref[b:b + 1, kcols])
            v_row = v_ref[b:b + 1, vcols]
            halves = []
            for half in range(DV // DK):
                lanes = slice(half * DK, (half + 1) * DK)
                s_new = a_col * s_ref[b, hh, :, lanes] + k_col * v_row[:, lanes]
                snew_ref[b, hh, :, lanes] = s_new
                halves.append(jnp.sum(q_col * s_new, axis=0, keepdims=True))
            o_ref[b:b + 1, vcols] = jnp.concatenate(halves, axis=1)


def _even_decode_out_kernel(x_ref, o_ref, sg_ref, y_ref, gng_ref, wout_ref, x1_ref):
    gng = gng_ref[...]
    parts = [_head_rmsnorm(o_ref[:, hh * DV:(hh + 1) * DV], gng) for hh in range(HEADS)]
    o_mix = (jnp.concatenate(parts, axis=1) * sg_ref[...]).astype(BF16)
    out = _dot(o_mix, wout_ref[0:V_WIDTH, :]) + _dot(y_ref[...].astype(BF16), wout_ref[V_WIDTH:V_WIDTH + CONV_W, :])
    x1_ref[...] = x_ref[...] + out


def _odd_decode_kernel(x_ref, ng_ref, win_ref, bin_ref, wdw_ref, bdw_ref, lng_ref, lnb_ref, wout_ref,
                       bout_ref, fng_ref, cbuf_ref,
                       y_ref, cnew_ref):
    n_hist = CCONV_K - 1
    x = x_ref[...]
    h = _rmsnorm(x, ng_ref[...]).astype(BF16)
    a = _dot(h, win_ref[:, 0:CONV_W]) + bin_ref[:, 0:CONV_W]
    a_gate = _dot(h, win_ref[:, CONV_W:2 * CONV_W]) + bin_ref[:, CONV_W:2 * CONV_W]
    u = _gate(a, a_gate)
    yc = bdw_ref[...] + wdw_ref[n_hist:CCONV_K, :] * u
    for j in range(n_hist):
        tap = cbuf_ref[j]
        yc = yc + wdw_ref[j:j + 1, :] * tap
        if j >= 1:
            cnew_ref[j - 1] = tap
    cnew_ref[n_hist - 1] = u
    z = _dot(h, win_ref[:, 2 * CONV_W:3 * CONV_W]) + bin_ref[:, 2 * CONV_W:3 * CONV_W]
    act = _layernorm_act(yc, z, lng_ref[...], lnb_ref[...])
    y_ref[...] = _rmsnorm(x + _dot(act, wout_ref[...]) + bout_ref[...], fng_ref[...])


def _const_spec(shape):
    nd = len(shape)
    return pl.BlockSpec(shape, lambda *_: (0,) * nd, pipeline_mode=pl.Buffered(1))


def _row(v):
    return v.reshape(1, -1)


def kernel(x_prompt, x_sample, state_gla, state_sconv, state_cconv, norm_g, w_in_a, w_gate_up, b_gate_up, gla_norm_g, w_sconv, w_out_a, w_in_c, b_in_c, w_dwconv, b_dwconv, ln_g, ln_b, w_out_c, b_out_c, final_norm_g):
    bsz, seq, d = x_prompt.shape
    dec_b = x_sample.shape[0]
    assert d == D_MODEL and seq % EVEN_TILE == 0 and seq % ODD_TILE == 0 and x_sample.shape[1] == 1
    assert EVEN_TILE // MXU_K >= 2 and CONV_W == 4 * MXU_N and ODD_TILE >= DFT_HOP
    assert w_in_a.shape[0] == 1 and w_in_c.shape[0] == 1 and norm_g.shape[0] == 2

    assert w_in_a.shape[2] == MAIN_W
    wmain = w_in_a[0].T.astype(BF16)
    wup = jnp.pad(w_gate_up[0], ((0, GATE_RANK_PAD - GATE_RANK), (0, 0))).astype(BF16)
    bup = _row(b_gate_up[0])
    gng = _row(gla_norm_g[0])
    wsc = w_sconv[0]
    wout_a = w_out_a[0].astype(BF16)
    ng0 = _row(norm_g[0])
    ng1 = _row(norm_g[1])
    win_c = w_in_c[0].astype(BF16)
    bin_c = _row(b_in_c[0])
    wdw = w_dwconv[0]
    bdw = _row(b_dwconv[0])
    lng = _row(ln_g[0])
    lnb = _row(ln_b[0])
    wout_c = w_out_c[0].astype(BF16)
    bout = _row(b_out_c[0])
    fng = _row(final_norm_g)

    even_weights = (ng0, wmain, wup, bup)
    even_prompt_weights = even_weights + (gng, wsc, wout_a)
    odd_weights = (ng1, win_c, bin_c, wdw, bdw, lng, lnb, wout_c, bout, fng)
    odd_weight_specs = [_const_spec(w.shape) for w in odd_weights]
    hf = DFT_HALF
    assert CCONV_HALO >= CCONV_K - 1 and DFT_HOP % 16 == 0
    kk = np.arange(hf)[:, None]

    def packed_basis(pos):
        ang = 2.0 * np.pi * kk * pos[None, :] / DFT_N
        lower = -np.sin(ang)
        lower[0] = np.cos(np.pi * pos)
        return np.cos(ang), lower

    fc, fs = packed_basis(np.arange(DFT_N, dtype=np.float64))
    fwd = np.concatenate([fc, fs], axis=0).astype(np.float32)
    ic, isn = packed_basis(np.arange(CCONV_HALO, DFT_N, dtype=np.float64))
    weight = np.full((hf, 1), 2.0 / DFT_N)
    weight[0] = 1.0 / DFT_N
    inv = np.concatenate([weight * ic, weight * isn], axis=0).T.astype(np.float32)
    hc, hs = packed_basis(np.arange(CCONV_K, dtype=np.float64))
    hs_imag = hs.copy()
    hs_imag[0] = 0.0
    hc_alt = hc.copy()
    hc_alt[0] = hs[0]
    resp = np.concatenate([hc, hs_imag, hc_alt], axis=0).astype(np.float32)
    h_resp = jnp.dot(jnp.asarray(resp), wdw[::-1], precision=lax.Precision.HIGHEST)
    odd_prompt_weights = (ng1, win_c, bin_c, bdw, lng, lnb, wout_c, bout, fng,
                          jnp.asarray(fwd), jnp.asarray(inv), h_resp[:hf], h_resp[hf:2 * hf], h_resp[2 * hf:])

    prompt_params = pltpu.CompilerParams(dimension_semantics=("arbitrary", "arbitrary"),
                                         vmem_limit_bytes=VMEM_LIMIT)

    tm = EVEN_TILE
    tile_spec = pl.BlockSpec((None, tm, D_MODEL), lambda b, t: (b, t, 0))
    x1_p, gla_p, sconv_p = pl.pallas_call(
        _even_prompt_kernel,
        grid=(bsz, seq // tm),
        in_specs=[tile_spec] + [_const_spec(w.shape) for w in even_prompt_weights],
        out_specs=[tile_spec,
                   pl.BlockSpec((None, None, HEADS, DK, DV), lambda b, t: (0, b, 0, 0, 0)),
                   pl.BlockSpec((None, None, SCONV_K - 1, CONV_W), lambda b, t: (0, b, 0, 0))],
        out_shape=[jax.ShapeDtypeStruct((bsz, seq, D_MODEL), F32),
                   jax.ShapeDtypeStruct((1, bsz, HEADS, DK, DV), F32),
                   jax.ShapeDtypeStruct((1, bsz, SCONV_K - 1, CONV_W), F32)],
        scratch_shapes=[pltpu.VMEM((HEADS, DV, DK), F32),
                        pltpu.VMEM((tm + SCONV_HALO, CONV_W), F32),
                        pltpu.VMEM((tm, QK_WIDTH), BF16),
                        pltpu.VMEM((tm, QK_WIDTH), BF16),
                        pltpu.VMEM((tm, QK_WIDTH), BF16),
                        pltpu.VMEM((tm, V_WIDTH), BF16),
                        pltpu.VMEM((tm, QK_WIDTH), F32),
                        pltpu.VMEM((tm, V_WIDTH), F32),
                        pltpu.VMEM((tm, CONV_W), BF16)],
        compiler_params=prompt_params,
        name="even_prompt",
    )(x_prompt, *even_prompt_weights)

    tm = ODD_TILE
    tile_spec = pl.BlockSpec((None, tm, D_MODEL), lambda b, t: (b, t, 0))
    y_p, cconv_p = pl.pallas_call(
        _odd_prompt_kernel,
        grid=(bsz, seq // tm),
        in_specs=[tile_spec] + [_const_spec(w.shape) for w in odd_prompt_weights],
        out_specs=[tile_spec,
                   pl.BlockSpec((None, None, CCONV_K - 1, CONV_W), lambda b, t: (0, b, 0, 0))],
        out_shape=[jax.ShapeDtypeStruct((bsz, seq, D_MODEL), F32),
                   jax.ShapeDtypeStruct((1, bsz, CCONV_K - 1, CONV_W), F32)],
        scratch_shapes=[pltpu.VMEM((tm + CCONV_HALO, CONV_W), BF16),
                        pltpu.VMEM((tm, CONV_W), F32),
                        pltpu.VMEM((CCONV_HALO, CONV_W), F32),
                        pltpu.VMEM((tm, CONV_W), F32)],
        compiler_params=prompt_params,
        name="odd_prompt",
    )(x1_p, *odd_prompt_weights)

    xs = x_sample.reshape(dec_b, D_MODEL)
    sbuf = state_sconv.reshape(dec_b, (SCONV_K - 1) * CONV_W)
    single = pltpu.CompilerParams(vmem_limit_bytes=VMEM_LIMIT)
    q_s, k_s, a_s, v_s, sg_s, ysc_s, sconv_s = pl.pallas_call(
        _even_decode_front_kernel,
        out_shape=[jax.ShapeDtypeStruct((dec_b, QK_WIDTH), F32),
                   jax.ShapeDtypeStruct((dec_b, QK_WIDTH), F32),
                   jax.ShapeDtypeStruct((dec_b, QK_WIDTH), F32),
                   jax.ShapeDtypeStruct((dec_b, V_WIDTH), F32),
                   jax.ShapeDtypeStruct((dec_b, V_WIDTH), F32),
                   jax.ShapeDtypeStruct((dec_b, CONV_W), F32),
                   jax.ShapeDtypeStruct((dec_b, (SCONV_K - 1) * CONV_W), F32)],
        compiler_params=single,
        name="even_decode_front",
    )(xs, *even_weights, wsc, sbuf)

    sb = DECODE_STATE_BLOCK
    assert dec_b % sb == 0
    vec_spec = lambda w: pl.BlockSpec((sb, w), lambda i: (i, 0))
    state_spec = pl.BlockSpec((sb, HEADS, DK, DV), lambda i: (i, 0, 0, 0))
    gla_s, o_s = pl.pallas_call(
        _gla_decode_kernel,
        grid=(dec_b // sb,),
        in_specs=[vec_spec(QK_WIDTH), vec_spec(QK_WIDTH), vec_spec(QK_WIDTH), vec_spec(V_WIDTH), state_spec],
        out_specs=[state_spec, vec_spec(V_WIDTH)],
        out_shape=[jax.ShapeDtypeStruct((dec_b, HEADS, DK, DV), F32),
                   jax.ShapeDtypeStruct((dec_b, V_WIDTH), F32)],
        compiler_params=pltpu.CompilerParams(dimension_semantics=("arbitrary",),
                                             vmem_limit_bytes=VMEM_LIMIT),
        name="gla_decode",
    )(q_s, k_s, a_s, v_s, state_gla[0])

    x1_s = pl.pallas_call(
        _even_decode_out_kernel,
        out_shape=jax.ShapeDtypeStruct((dec_b, D_MODEL), F32),
        compiler_params=single,
        name="even_decode_out",
    )(xs, o_s, sg_s, ysc_s, gng, wout_a)

    ob = DECODE_ODD_BLOCK
    assert dec_b % ob == 0
    rows_spec = pl.BlockSpec((ob, D_MODEL), lambda i: (i, 0))
    hist_spec = pl.BlockSpec((CCONV_K - 1, ob, CONV_W), lambda i: (0, i, 0))
    cbuf = jnp.transpose(state_cconv[0], (1, 0, 2))
    y_s, cconv_t = pl.pallas_call(
        _odd_decode_kernel,
        grid=(dec_b // ob,),
        in_specs=[rows_spec] + odd_weight_specs + [hist_spec],
        out_specs=[rows_spec, hist_spec],
        out_shape=[jax.ShapeDtypeStruct((dec_b, D_MODEL), F32),
                   jax.ShapeDtypeStruct((CCONV_K - 1, dec_b, CONV_W), F32)],
        compiler_params=pltpu.CompilerParams(dimension_semantics=("arbitrary",),
                                             vmem_limit_bytes=VMEM_LIMIT),
        name="odd_decode",
    )(x1_s, *odd_weights, cbuf)
    cconv_s = jnp.transpose(cconv_t, (1, 0, 2))[None]

    return (y_p,
            y_s.reshape(dec_b, 1, D_MODEL),
            gla_p,
            sconv_p,
            cconv_p,
            gla_s.reshape(1, dec_b, HEADS, DK, DV),
            sconv_s.reshape(1, dec_b, SCONV_K - 1, CONV_W),
            cconv_s)
```

```python
import jax
import jax.numpy as jnp
import numpy as np
from jax import lax
from jax.experimental import pallas as pl
from jax.experimental.pallas import tpu as pltpu

F32 = jnp.float32
BF16 = jnp.bfloat16

D_MODEL = 1024
HEADS = 4
DK = 128
DV = 256
QK_WIDTH = HEADS * DK
V_WIDTH = HEADS * DV
GATE_RANK = 16
GATE_RANK_PAD = 128
GATE_TEMP_INV = 1.0 / 16.0
CHUNK = 64
CHUNK_SHIFT = 6
SCONV_K = 3
CCONV_K = 31
CONV_W = 1024
RMS_EPS = 1e-6
LN_EPS = 1e-5
Q_SCALE = DK ** -0.5

COL_Q = 0
COL_K = COL_Q + QK_WIDTH
COL_V = COL_K + QK_WIDTH
COL_G = COL_V + V_WIDTH
COL_A_LOW = COL_G + V_WIDTH
COL_HB = COL_A_LOW + GATE_RANK
COL_GATE_B = COL_HB + CONV_W
COL_GATE_C = COL_GATE_B + CONV_W
COL_ZB = COL_GATE_C + CONV_W
MAIN_W = COL_ZB + CONV_W

MXU_K = 256
MXU_N = 256
EVEN_TILE = 512
ODD_TILE = 1024
CCONV_HALO = 32
SCONV_HALO = 8
DFT_N = MXU_K
DFT_HALF = DFT_N // 2
DFT_HOP = DFT_N - CCONV_HALO
RESP_K_PAD = 128
DECODE_STATE_BLOCK = 16
DECODE_ODD_BLOCK = 64
VMEM_LIMIT = 60 * 1024 * 1024


def _dot(a, b):
    return jnp.dot(a, b, preferred_element_type=F32)


def _dot_nt(a, b):
    return lax.dot_general(a, b, (((1,), (1,)), ((), ())), preferred_element_type=F32)


def _dot_tn(a, b):
    return lax.dot_general(a, b, (((0,), (0,)), ((), ())), preferred_element_type=F32)


def _proj(h, wt_ref, lo, hi):
    return _dot_nt(h, wt_ref[lo:hi, :])


def _rmsnorm(x, g):
    ms = jnp.mean(x * x, axis=-1, keepdims=True)
    return x * lax.rsqrt(ms + RMS_EPS) * g


def _gate(x, y):
    return x / (1.0 + jnp.exp(-y))


def _silu(x):
    return _gate(x, x)


def _log_sigmoid(x):
    return -(jnp.maximum(-x, 0.0) + jnp.log(1.0 + jnp.exp(-jnp.abs(x))))


def _log_decay(h, wmain_ref, wup_ref, bup_ref):
    a_low = _proj(h, wmain_ref, COL_A_LOW, COL_A_LOW + GATE_RANK_PAD).astype(BF16)
    logit = _dot(a_low, wup_ref[...]) + bup_ref[...]
    return _log_sigmoid(logit) * GATE_TEMP_INV


def _head_rmsnorm(o, gng):
    ms = jnp.mean(o * o, axis=-1, keepdims=True)
    return o * lax.rsqrt(ms + RMS_EPS) * gng


def _layernorm_act(yc, z, lng, lnb):
    mu = jnp.mean(yc, axis=-1, keepdims=True)
    xc = yc - mu
    var = jnp.mean(xc * xc, axis=-1, keepdims=True)
    yn = xc * lax.rsqrt(var + LN_EPS) * lng + lnb
    return (_silu(yn) * _silu(z)).astype(BF16)


def _short_conv_gate(u, prev1, prev2, gate_b, z_b, wsc_ref):
    y = wsc_ref[2:3, :] * u + wsc_ref[1:2, :] * prev1 + wsc_ref[0:1, :] * prev2
    return gate_b * y * _silu(z_b)


def _even_prompt_kernel(x_ref, ng_ref, wmain_ref, wup_ref, bup_ref, gng_ref, wsc_ref, wout_ref,
                        x1_ref, sgla_ref, sconv_ref,
                        st_ref, ubuf_ref, qe_ref, ke_ref, kd_ref, v_ref, dec_ref, mix_ref, ysc_ref):
    tm = EVEN_TILE
    t = pl.program_id(1)
    last_t = pl.num_programs(1) - 1

    @pl.when(t == 0)
    def _():
        st_ref[...] = jnp.zeros_like(st_ref)
        ubuf_ref[0:SCONV_HALO, :] = jnp.zeros((SCONV_HALO, CONV_W), F32)

    x = x_ref[...]
    h = _rmsnorm(x, ng_ref[...]).astype(BF16)

    def short_conv_group(g0):
        cols = slice(g0, g0 + MXU_N)
        part = lambda c0: _proj(h, wmain_ref, c0 + g0, c0 + g0 + MXU_N)
        u = part(COL_GATE_C) * part(COL_HB)
        ubuf_ref[SCONV_HALO:SCONV_HALO + tm, cols] = u
        y = (wsc_ref[2:3, cols] * u + wsc_ref[1:2, cols] * ubuf_ref[pl.ds(SCONV_HALO - 1, tm), cols]
             + wsc_ref[0:1, cols] * ubuf_ref[pl.ds(SCONV_HALO - 2, tm), cols])
        ysc_ref[:, cols] = (part(COL_GATE_B) * y * _silu(part(COL_ZB))).astype(BF16)

    q = _proj(h, wmain_ref, COL_Q, COL_K) * Q_SCALE
    k = _proj(h, wmain_ref, COL_K, COL_V)
    v_ref[...] = _proj(h, wmain_ref, COL_V, COL_G).astype(BF16)
    a_low = _proj(h, wmain_ref, COL_A_LOW, COL_A_LOW + GATE_RANK_PAD).astype(BF16)
    short_conv_group(0)
    log_a = _log_sigmoid(_dot(a_low, wup_ref[...]) + bup_ref[...]) * GATE_TEMP_INV
    short_conv_group(MXU_N)

    row = lax.broadcasted_iota(jnp.int32, (MXU_K, MXU_K), 0)
    col = lax.broadcasted_iota(jnp.int32, (MXU_K, MXU_K), 1)
    in_chunk_causal = ((row >> CHUNK_SHIFT) == (col >> CHUNK_SHIFT)) & (col <= row)
    tri = jnp.where(in_chunk_causal, 1.0, 0.0).astype(BF16)
    la_hi = log_a.astype(BF16)
    la_lo = (log_a - la_hi.astype(F32)).astype(BF16)
    for sb in range(tm // MXU_K):
        rows = slice(sb * MXU_K, (sb + 1) * MXU_K)
        b_cum = _dot(tri, la_hi[rows, :]) + _dot(tri, la_lo[rows, :])
        b_tot = jnp.concatenate(
            [jnp.broadcast_to(b_cum[(c + 1) * CHUNK - 1:(c + 1) * CHUNK, :], (CHUNK, QK_WIDTH))
             for c in range(MXU_K // CHUNK)], axis=0)
        qe_ref[rows, :] = (q[rows, :] * jnp.exp(b_cum)).astype(BF16)
        ke_ref[rows, :] = (k[rows, :] * jnp.exp(-b_cum)).astype(BF16)
        kd_ref[rows, :] = (k[rows, :] * jnp.exp(b_tot - b_cum)).astype(BF16)
        dec_ref[rows, :] = jnp.exp(b_tot)
        if sb < 2:
            short_conv_group((2 + sb) * MXU_N)

    gng = gng_ref[...]
    for hh in range(HEADS):
        kcols = slice(hh * DK, (hh + 1) * DK)
        vcols = slice(hh * DV, (hh + 1) * DV)
        st = st_ref[hh]
        for sb in range(tm // MXU_K):
            rows = slice(sb * MXU_K, (sb + 1) * MXU_K)
            sc = jnp.where(in_chunk_causal, _dot_nt(qe_ref[rows, kcols], ke_ref[rows, kcols]), 0.0)
            o_intra = _dot(sc.astype(BF16), v_ref[rows, vcols])
            for c in range(MXU_K // CHUNK):
                r0 = sb * MXU_K + c * CHUNK
                crow = slice(r0, r0 + CHUNK)
                o = o_intra[c * CHUNK:(c + 1) * CHUNK, :] + _dot_nt(qe_ref[crow, kcols], st.astype(BF16))
                mix_ref[crow, vcols] = _head_rmsnorm(o, gng)
                dec = dec_ref[r0:r0 + 1, kcols]
                st = st * dec + _dot_tn(v_ref[crow, vcols], kd_ref[crow, kcols])
        st_ref[hh] = st

    o_mix = (mix_ref[...] * _silu(_proj(h, wmain_ref, COL_G, COL_A_LOW))).astype(BF16)

    ubuf_ref[0:SCONV_HALO, :] = ubuf_ref[tm:tm + SCONV_HALO, :]

    out = _dot(o_mix, wout_ref[0:V_WIDTH, :]) + _dot(ysc_ref[...], wout_ref[V_WIDTH:V_WIDTH + CONV_W, :])
    x1_ref[...] = x + out

    @pl.when(t == last_t)
    def _():
        for hh in range(HEADS):
            sgla_ref[hh] = st_ref[hh].T
        sconv_ref[...] = ubuf_ref[pl.ds(SCONV_HALO + tm - (SCONV_K - 1), SCONV_K - 1), :]


def _odd_prompt_kernel(x_ref, ng_ref, win_ref, bin_ref, bdw_ref, lng_ref, lnb_ref, wout_ref,
                       bout_ref, fng_ref, fwd_ref, inv_ref, basis_ref, taps_ref,
                       y_ref, cconv_ref,
                       ub_ref, yc_ref, tail_ref, z_ref, resp_ref):
    tm = ODD_TILE
    t = pl.program_id(1)
    last_t = pl.num_programs(1) - 1

    @pl.when((pl.program_id(0) == 0) & (t == 0))
    def _():
        resp_ref[...] = jnp.dot(basis_ref[...], taps_ref[...], precision=lax.Precision.HIGHEST,
                                preferred_element_type=F32)

    @pl.when(t == 0)
    def _():
        ub_ref[0:CCONV_HALO, :] = jnp.zeros((CCONV_HALO, CONV_W), BF16)

    x = x_ref[...]
    h = _rmsnorm(x, ng_ref[...]).astype(BF16)
    for g0 in range(0, CONV_W, MXU_N):
        cols = slice(g0, g0 + MXU_N)
        gcols = slice(CONV_W + g0, CONV_W + g0 + MXU_N)
        u = _gate(_dot(h, win_ref[:, cols]) + bin_ref[:, cols], _dot(h, win_ref[:, gcols]) + bin_ref[:, gcols])
        tail_ref[:, cols] = u[tm - CCONV_HALO:, :]
        ub_ref[CCONV_HALO:CCONV_HALO + tm, cols] = u.astype(BF16)

    @pl.when(t == last_t)
    def _():
        cconv_ref[...] = tail_ref[CCONV_HALO - (CCONV_K - 1):, :]

    bdw = bdw_ref[...]
    ha = resp_ref[0:DFT_HALF, :]
    hb = resp_ref[DFT_HALF:2 * DFT_HALF, :]
    ha2 = resp_ref[2 * DFT_HALF:3 * DFT_HALF, :]
    fwd = fwd_ref[...].astype(BF16)
    inv = inv_ref[...].astype(BF16)
    starts = sorted({min(s0, tm - DFT_HOP) for s0 in range(0, tm, DFT_HOP)})
    z_groups = list(range(0, CONV_W, MXU_N))
    for i, start in enumerate(starts):
        spec = _dot(fwd, ub_ref[start:start + DFT_N, :])
        p = spec[0:DFT_HALF, :]
        q = spec[DFT_HALF:DFT_N, :]
        prod = jnp.concatenate([p * ha - q * hb, p * hb + q * ha2], axis=0).astype(BF16)
        yc_ref[start:start + DFT_HOP, :] = _dot(inv, prod) + bdw
        if i < len(z_groups):
            zc = slice(2 * CONV_W + z_groups[i], 2 * CONV_W + z_groups[i] + MXU_N)
            z_ref[:, z_groups[i]:z_groups[i] + MXU_N] = _dot(h, win_ref[:, zc]) + bin_ref[:, zc]
    assert len(starts) >= len(z_groups)

    ub_ref[0:CCONV_HALO, :] = ub_ref[tm:tm + CCONV_HALO, :]

    act = _layernorm_act(yc_ref[...], z_ref[...], lng_ref[...], lnb_ref[...])
    y_ref[...] = _rmsnorm(x + _dot(act, wout_ref[...]) + bout_ref[...], fng_ref[...])


def _even_decode_front_kernel(x_ref, ng_ref, wmain_ref, wup_ref, bup_ref, wsc_ref, sbuf_ref,
                              q_ref, k_ref, a_ref, v_ref, sg_ref, y_ref, snew_ref):
    h = _rmsnorm(x_ref[...], ng_ref[...]).astype(BF16)
    q_ref[...] = _proj(h, wmain_ref, COL_Q, COL_K) * Q_SCALE
    k_ref[...] = _proj(h, wmain_ref, COL_K, COL_V)
    v_ref[...] = _proj(h, wmain_ref, COL_V, COL_G)
    sg_ref[...] = _silu(_proj(h, wmain_ref, COL_G, COL_A_LOW))
    a_ref[...] = jnp.exp(_log_decay(h, wmain_ref, wup_ref, bup_ref))
    u = _proj(h, wmain_ref, COL_GATE_C, COL_ZB) * _proj(h, wmain_ref, COL_HB, COL_GATE_B)
    prev2 = sbuf_ref[:, 0:CONV_W]
    prev1 = sbuf_ref[:, CONV_W:2 * CONV_W]
    y_ref[...] = _short_conv_gate(u, prev1, prev2, _proj(h, wmain_ref, COL_GATE_B, COL_GATE_C),
                                  _proj(h, wmain_ref, COL_ZB, MAIN_W), wsc_ref)
    snew_ref[:, 0:CONV_W] = prev1
    snew_ref[:, CONV_W:2 * CONV_W] = u


def _lane_bcast_column(row):
    return jnp.broadcast_to(row, (DK, DK)).T


def _gla_decode_kernel(q_ref, k_ref, a_ref, v_ref, s_ref, snew_ref, o_ref):
    for b in range(DECODE_STATE_BLOCK):
        for hh in range(HEADS):
            kcols = slice(hh * DK, (hh + 1) * DK)
            vcols = slice(hh * DV, (hh + 1) * DV)
            a_col = _lane_bcast_column(a_ref[b:b + 1, kcols])
            k_col = _lane_bcast_column(k_ref[b:b + 1, kcols])
            q_col = _lane_bcast_column(q_ref[b:b + 1, kcols])
            v_row = v_ref[b:b + 1, vcols]
            halves = []
            for half in range(DV // DK):
                lanes = slice(half * DK, (half + 1) * DK)
                s_new = a_col * s_ref[b, hh, :, lanes] + k_col * v_row[:, lanes]
                snew_ref[b, hh, :, lanes] = s_new
                halves.append(jnp.sum(q_col * s_new, axis=0, keepdims=True))
            o_ref[b:b + 1, vcols] = jnp.concatenate(halves, axis=1)


def _even_decode_out_kernel(x_ref, o_ref, sg_ref, y_ref, gng_ref, wout_ref, x1_ref):
    gng = gng_ref[...]
    parts = [_head_rmsnorm(o_ref[:, hh * DV:(hh + 1) * DV], gng) for hh in range(HEADS)]
    o_mix = (jnp.concatenate(parts, axis=1) * sg_ref[...]).astype(BF16)
    out = _dot(o_mix, wout_ref[0:V_WIDTH, :]) + _dot(y_ref[...].astype(BF16), wout_ref[V_WIDTH:V_WIDTH + CONV_W, :])
    x1_ref[...] = x_ref[...] + out


def _odd_decode_kernel(x_ref, ng_ref, win_ref, bin_ref, wdw_ref, bdw_ref, lng_ref, lnb_ref, wout_ref,
                       bout_ref, fng_ref, cbuf_ref,
                       y_ref, cnew_ref):
    n_hist = CCONV_K - 1
    x = x_ref[...]
    h = _rmsnorm(x, ng_ref[...]).astype(BF16)
    a = _dot(h, win_ref[:, 0:CONV_W]) + bin_ref[:, 0:CONV_W]
    a_gate = _dot(h, win_ref[:, CONV_W:2 * CONV_W]) + bin_ref[:, CONV_W:2 * CONV_W]
    u = _gate(a, a_gate)
    yc = bdw_ref[...] + wdw_ref[n_hist:CCONV_K, :] * u
    for j in range(n_hist):
        tap = cbuf_ref[j]
        yc = yc + wdw_ref[j:j + 1, :] * tap
        if j >= 1:
            cnew_ref[j - 1] = tap
    cnew_ref[n_hist - 1] = u
    z = _dot(h, win_ref[:, 2 * CONV_W:3 * CONV_W]) + bin_ref[:, 2 * CONV_W:3 * CONV_W]
    act = _layernorm_act(yc, z, lng_ref[...], lnb_ref[...])
    y_ref[...] = _rmsnorm(x + _dot(act, wout_ref[...]) + bout_ref[...], fng_ref[...])


def _const_spec(shape):
    nd = len(shape)
    return pl.BlockSpec(shape, lambda *_: (0,) * nd, pipeline_mode=pl.Buffered(1))


def _row(v):
    return v.reshape(1, -1)


def kernel(x_prompt, x_sample, state_gla, state_sconv, state_cconv, norm_g, w_in_a, w_gate_up, b_gate_up, gla_norm_g, w_sconv, w_out_a, w_in_c, b_in_c, w_dwconv, b_dwconv, ln_g, ln_b, w_out_c, b_out_c, final_norm_g):
    bsz, seq, d = x_prompt.shape
    dec_b = x_sample.shape[0]
    assert d == D_MODEL and seq % EVEN_TILE == 0 and seq % ODD_TILE == 0 and x_sample.shape[1] == 1
    assert EVEN_TILE // MXU_K >= 2 and CONV_W == 4 * MXU_N and ODD_TILE >= DFT_HOP
    assert w_in_a.shape[0] == 1 and w_in_c.shape[0] == 1 and norm_g.shape[0] == 2

    assert w_in_a.shape[2] == MAIN_W
    wmain = w_in_a[0].T.astype(BF16)
    wup = jnp.pad(w_gate_up[0], ((0, GATE_RANK_PAD - GATE_RANK), (0, 0))).astype(BF16)
    bup = _row(b_gate_up[0])
    gng = _row(gla_norm_g[0])
    wsc = w_sconv[0]
    wout_a = w_out_a[0].astype(BF16)
    ng0 = _row(norm_g[0])
    ng1 = _row(norm_g[1])
    win_c = w_in_c[0].astype(BF16)
    bin_c = _row(b_in_c[0])
    wdw = w_dwconv[0]
    bdw = _row(b_dwconv[0])
    lng = _row(ln_g[0])
    lnb = _row(ln_b[0])
    wout_c = w_out_c[0].astype(BF16)
    bout = _row(b_out_c[0])
    fng = _row(final_norm_g)

    even_weights = (ng0, wmain, wup, bup)
    even_prompt_weights = even_weights + (gng, wsc, wout_a)
    odd_weights = (ng1, win_c, bin_c, wdw, bdw, lng, lnb, wout_c, bout, fng)
    odd_weight_specs = [_const_spec(w.shape) for w in odd_weights]
    hf = DFT_HALF
    assert CCONV_HALO >= CCONV_K - 1 and DFT_HOP % 16 == 0
    kk = np.arange(hf)[:, None]

    def packed_basis(pos):
        ang = 2.0 * np.pi * kk * pos[None, :] / DFT_N
        lower = -np.sin(ang)
        lower[0] = np.cos(np.pi * pos)
        return np.cos(ang), lower

    fc, fs = packed_basis(np.arange(DFT_N, dtype=np.float64))
    fwd = np.concatenate([fc, fs], axis=0).astype(np.float32)
    ic, isn = packed_basis(np.arange(CCONV_HALO, DFT_N, dtype=np.float64))
    weight = np.full((hf, 1), 2.0 / DFT_N)
    weight[0] = 1.0 / DFT_N
    inv = np.concatenate([weight * ic, weight * isn], axis=0).T.astype(np.float32)
    hc, hs = packed_basis(np.arange(CCONV_K, dtype=np.float64))
    hs_imag = hs.copy()
    hs_imag[0] = 0.0
    hc_alt = hc.copy()
    hc_alt[0] = hs[0]
    basis = np.zeros((3 * hf, RESP_K_PAD), np.float32)
    basis[:, :CCONV_K] = np.concatenate([hc, hs_imag, hc_alt], axis=0)
    taps_by_lag = jnp.pad(wdw[::-1], ((0, RESP_K_PAD - CCONV_K), (0, 0)))
    odd_prompt_weights = (ng1, win_c, bin_c, bdw, lng, lnb, wout_c, bout, fng,
                          jnp.asarray(fwd), jnp.asarray(inv), jnp.asarray(basis), taps_by_lag)

    prompt_params = pltpu.CompilerParams(dimension_semantics=("arbitrary", "arbitrary"),
                                         vmem_limit_bytes=VMEM_LIMIT)

    tm = EVEN_TILE
    tile_spec = pl.BlockSpec((None, tm, D_MODEL), lambda b, t: (b, t, 0))
    x1_p, gla_p, sconv_p = pl.pallas_call(
        _even_prompt_kernel,
        grid=(bsz, seq // tm),
        in_specs=[tile_spec] + [_const_spec(w.shape) for w in even_prompt_weights],
        out_specs=[tile_spec,
                   pl.BlockSpec((None, None, HEADS, DK, DV), lambda b, t: (0, b, 0, 0, 0)),
                   pl.BlockSpec((None, None, SCONV_K - 1, CONV_W), lambda b, t: (0, b, 0, 0))],
        out_shape=[jax.ShapeDtypeStruct((bsz, seq, D_MODEL), F32),
                   jax.ShapeDtypeStruct((1, bsz, HEADS, DK, DV), F32),
                   jax.ShapeDtypeStruct((1, bsz, SCONV_K - 1, CONV_W), F32)],
        scratch_shapes=[pltpu.VMEM((HEADS, DV, DK), F32),
                        pltpu.VMEM((tm + SCONV_HALO, CONV_W), F32),
                        pltpu.VMEM((tm, QK_WIDTH), BF16),
                        pltpu.VMEM((tm, QK_WIDTH), BF16),
                        pltpu.VMEM((tm, QK_WIDTH), BF16),
                        pltpu.VMEM((tm, V_WIDTH), BF16),
                        pltpu.VMEM((tm, QK_WIDTH), F32),
                        pltpu.VMEM((tm, V_WIDTH), F32),
                        pltpu.VMEM((tm, CONV_W), BF16)],
        compiler_params=prompt_params,
        name="even_prompt",
    )(x_prompt, *even_prompt_weights)

    tm = ODD_TILE
    tile_spec = pl.BlockSpec((None, tm, D_MODEL), lambda b, t: (b, t, 0))
    y_p, cconv_p = pl.pallas_call(
        _odd_prompt_kernel,
        grid=(bsz, seq // tm),
        in_specs=[tile_spec] + [_const_spec(w.shape) for w in odd_prompt_weights],
        out_specs=[tile_spec,
                   pl.BlockSpec((None, None, CCONV_K - 1, CONV_W), lambda b, t: (0, b, 0, 0))],
        out_shape=[jax.ShapeDtypeStruct((bsz, seq, D_MODEL), F32),
                   jax.ShapeDtypeStruct((1, bsz, CCONV_K - 1, CONV_W), F32)],
        scratch_shapes=[pltpu.VMEM((tm + CCONV_HALO, CONV_W), BF16),
                        pltpu.VMEM((tm, CONV_W), F32),
                        pltpu.VMEM((CCONV_HALO, CONV_W), F32),
                        pltpu.VMEM((tm, CONV_W), F32),
                        pltpu.VMEM((3 * DFT_HALF, CONV_W), F32)],
        compiler_params=prompt_params,
        name="odd_prompt",
    )(x1_p, *odd_prompt_weights)

    xs = x_sample.reshape(dec_b, D_MODEL)
    sbuf = state_sconv.reshape(dec_b, (SCONV_K - 1) * CONV_W)
    single = pltpu.CompilerParams(vmem_limit_bytes=VMEM_LIMIT)
    q_s, k_s, a_s, v_s, sg_s, ysc_s, sconv_s = pl.pallas_call(
        _even_decode_front_kernel,
        out_shape=[jax.ShapeDtypeStruct((dec_b, QK_WIDTH), F32),
                   jax.ShapeDtypeStruct((dec_b, QK_WIDTH), F32),
                   jax.ShapeDtypeStruct((dec_b, QK_WIDTH), F32),
                   jax.ShapeDtypeStruct((dec_b, V_WIDTH), F32),
                   jax.ShapeDtypeStruct((dec_b, V_WIDTH), F32),
                   jax.ShapeDtypeStruct((dec_b, CONV_W), F32),
                   jax.ShapeDtypeStruct((dec_b, (SCONV_K - 1) * CONV_W), F32)],
        compiler_params=single,
        name="even_decode_front",
    )(xs, *even_weights, wsc, sbuf)

    sb = DECODE_STATE_BLOCK
    assert dec_b % sb == 0
    vec_spec = lambda w: pl.BlockSpec((sb, w), lambda i: (i, 0))
    state_spec = pl.BlockSpec((sb, HEADS, DK, DV), lambda i: (i, 0, 0, 0))
    gla_s, o_s = pl.pallas_call(
        _gla_decode_kernel,
        grid=(dec_b // sb,),
        in_specs=[vec_spec(QK_WIDTH), vec_spec(QK_WIDTH), vec_spec(QK_WIDTH), vec_spec(V_WIDTH), state_spec],
        out_specs=[state_spec, vec_spec(V_WIDTH)],
        out_shape=[jax.ShapeDtypeStruct((dec_b, HEADS, DK, DV), F32),
                   jax.ShapeDtypeStruct((dec_b, V_WIDTH), F32)],
        compiler_params=pltpu.CompilerParams(dimension_semantics=("arbitrary",),
                                             vmem_limit_bytes=VMEM_LIMIT),
        name="gla_decode",
    )(q_s, k_s, a_s, v_s, state_gla[0])

    x1_s = pl.pallas_call(
        _even_decode_out_kernel,
        out_shape=jax.ShapeDtypeStruct((dec_b, D_MODEL), F32),
        compiler_params=single,
        name="even_decode_out",
    )(xs, o_s, sg_s, ysc_s, gng, wout_a)

    ob = DECODE_ODD_BLOCK
    assert dec_b % ob == 0
    rows_spec = pl.BlockSpec((ob, D_MODEL), lambda i: (i, 0))
    hist_spec = pl.BlockSpec((CCONV_K - 1, ob, CONV_W), lambda i: (0, i, 0))
    cbuf = jnp.transpose(state_cconv[0], (1, 0, 2))
    y_s, cconv_t = pl.pallas_call(
        _odd_decode_kernel,
        grid=(dec_b // ob,),
        in_specs=[rows_spec] + odd_weight_specs + [hist_spec],
        out_specs=[rows_spec, hist_spec],
        out_shape=[jax.ShapeDtypeStruct((dec_b, D_MODEL), F32),
                   jax.ShapeDtypeStruct((CCONV_K - 1, dec_b, CONV_W), F32)],
        compiler_params=pltpu.CompilerParams(dimension_semantics=("arbitrary",),
                                             vmem_limit_bytes=VMEM_LIMIT),
        name="odd_decode",
    )(x1_s, *odd_weights, cbuf)
    cconv_s = jnp.transpose(cconv_t, (1, 0, 2))[None]

    return (y_p,
            y_s.reshape(dec_b, 1, D_MODEL),
            gla_p,
            sconv_p,
            cconv_p,
            gla_s.reshape(1, dec_b, HEADS, DK, DV),
            sconv_s.reshape(1, dec_b, SCONV_K - 1, CONV_W),
            cconv_s)
```

```python
import jax
import jax.numpy as jnp
import numpy as np
from jax import lax
from jax.experimental import pallas as pl
from jax.experimental.pallas import tpu as pltpu

F32 = jnp.float32
BF16 = jnp.bfloat16

D_MODEL = 1024
HEADS = 4
DK = 128
DV = 256
QK_WIDTH = HEADS * DK
V_WIDTH = HEADS * DV
GATE_RANK = 16
GATE_RANK_PAD = 128
GATE_TEMP_INV = 1.0 / 16.0
CHUNK = 64
CHUNK_SHIFT = 6
SCONV_K = 3
CCONV_K = 31
CONV_W = 1024
RMS_EPS = 1e-6
LN_EPS = 1e-5
Q_SCALE = DK ** -0.5

COL_Q = 0
COL_K = COL_Q + QK_WIDTH
COL_V = COL_K + QK_WIDTH
COL_G = COL_V + V_WIDTH
COL_A_LOW = COL_G + V_WIDTH
COL_HB = COL_A_LOW + GATE_RANK
COL_GATE_B = COL_HB + CONV_W
COL_GATE_C = COL_GATE_B + CONV_W
COL_ZB = COL_GATE_C + CONV_W
MAIN_W = COL_ZB + CONV_W

MXU_K = 256
MXU_N = 256
EVEN_TILE = 512
ODD_TILE = 1024
CCONV_HALO = 32
SCONV_HALO = 8
DFT_N = MXU_K
DFT_HALF = DFT_N // 2
DFT_HOP = DFT_N - CCONV_HALO
RESP_K_PAD = 128
DECODE_STATE_BLOCK = 8
DECODE_STATE_BUFFERS = 3
DECODE_ODD_BLOCK = 64
VMEM_LIMIT = 60 * 1024 * 1024


def _dot(a, b):
    return jnp.dot(a, b, preferred_element_type=F32)


def _dot_nt(a, b):
    return lax.dot_general(a, b, (((1,), (1,)), ((), ())), preferred_element_type=F32)


def _dot_tn(a, b):
    return lax.dot_general(a, b, (((0,), (0,)), ((), ())), preferred_element_type=F32)


def _proj(h, wt_ref, lo, hi):
    return _dot_nt(h, wt_ref[lo:hi, :])


def _rmsnorm(x, g):
    ms = jnp.mean(x * x, axis=-1, keepdims=True)
    return x * lax.rsqrt(ms + RMS_EPS) * g


def _gate(x, y):
    return x / (1.0 + jnp.exp(-y))


def _silu(x):
    return _gate(x, x)


def _log_sigmoid(x):
    return -(jnp.maximum(-x, 0.0) + jnp.log(1.0 + jnp.exp(-jnp.abs(x))))


def _log_decay(h, wmain_ref, wup_ref, bup_ref):
    a_low = _proj(h, wmain_ref, COL_A_LOW, COL_A_LOW + GATE_RANK_PAD).astype(BF16)
    logit = _dot(a_low, wup_ref[...]) + bup_ref[...]
    return _log_sigmoid(logit) * GATE_TEMP_INV


def _head_rmsnorm(o, gng):
    ms = jnp.mean(o * o, axis=-1, keepdims=True)
    return o * lax.rsqrt(ms + RMS_EPS) * gng


def _layernorm_act(yc, z, lng, lnb):
    mu = jnp.mean(yc, axis=-1, keepdims=True)
    xc = yc - mu
    var = jnp.mean(xc * xc, axis=-1, keepdims=True)
    yn = xc * lax.rsqrt(var + LN_EPS) * lng + lnb
    return (_silu(yn) * _silu(z)).astype(BF16)


def _short_conv_gate(u, prev1, prev2, gate_b, z_b, wsc_ref):
    y = wsc_ref[2:3, :] * u + wsc_ref[1:2, :] * prev1 + wsc_ref[0:1, :] * prev2
    return gate_b * y * _silu(z_b)


def _even_prompt_kernel(x_ref, ng_ref, wmain_ref, wup_ref, bup_ref, gng_ref, wsc_ref, wout_ref,
                        x1_ref, sgla_ref, sconv_ref,
                        st_ref, ubuf_ref, qe_ref, ke_ref, kd_ref, v_ref, dec_ref, mix_ref, ysc_ref):
    tm = EVEN_TILE
    t = pl.program_id(1)
    last_t = pl.num_programs(1) - 1

    @pl.when(t == 0)
    def _():
        st_ref[...] = jnp.zeros_like(st_ref)
        ubuf_ref[0:SCONV_HALO, :] = jnp.zeros((SCONV_HALO, CONV_W), F32)

    x = x_ref[...]
    h = _rmsnorm(x, ng_ref[...]).astype(BF16)

    def short_conv_group(g0):
        cols = slice(g0, g0 + MXU_N)
        part = lambda c0: _proj(h, wmain_ref, c0 + g0, c0 + g0 + MXU_N)
        u = part(COL_GATE_C) * part(COL_HB)
        ubuf_ref[SCONV_HALO:SCONV_HALO + tm, cols] = u
        y = (wsc_ref[2:3, cols] * u + wsc_ref[1:2, cols] * ubuf_ref[pl.ds(SCONV_HALO - 1, tm), cols]
             + wsc_ref[0:1, cols] * ubuf_ref[pl.ds(SCONV_HALO - 2, tm), cols])
        ysc_ref[:, cols] = (part(COL_GATE_B) * y * _silu(part(COL_ZB))).astype(BF16)

    q = _proj(h, wmain_ref, COL_Q, COL_K) * Q_SCALE
    k = _proj(h, wmain_ref, COL_K, COL_V)
    v_ref[...] = _proj(h, wmain_ref, COL_V, COL_G).astype(BF16)
    a_low = _proj(h, wmain_ref, COL_A_LOW, COL_A_LOW + GATE_RANK_PAD).astype(BF16)
    short_conv_group(0)
    log_a = _log_sigmoid(_dot(a_low, wup_ref[...]) + bup_ref[...]) * GATE_TEMP_INV
    short_conv_group(MXU_N)

    row = lax.broadcasted_iota(jnp.int32, (MXU_K, MXU_K), 0)
    col = lax.broadcasted_iota(jnp.int32, (MXU_K, MXU_K), 1)
    in_chunk_causal = ((row >> CHUNK_SHIFT) == (col >> CHUNK_SHIFT)) & (col <= row)
    tri = jnp.where(in_chunk_causal, 1.0, 0.0).astype(BF16)
    la_hi = log_a.astype(BF16)
    la_lo = (log_a - la_hi.astype(F32)).astype(BF16)
    for sb in range(tm // MXU_K):
        rows = slice(sb * MXU_K, (sb + 1) * MXU_K)
        b_cum = _dot(tri, la_hi[rows, :]) + _dot(tri, la_lo[rows, :])
        b_tot = jnp.concatenate(
            [jnp.broadcast_to(b_cum[(c + 1) * CHUNK - 1:(c + 1) * CHUNK, :], (CHUNK, QK_WIDTH))
             for c in range(MXU_K // CHUNK)], axis=0)
        qe_ref[rows, :] = (q[rows, :] * jnp.exp(b_cum)).astype(BF16)
        ke_ref[rows, :] = (k[rows, :] * jnp.exp(-b_cum)).astype(BF16)
        kd_ref[rows, :] = (k[rows, :] * jnp.exp(b_tot - b_cum)).astype(BF16)
        dec_ref[rows, :] = jnp.exp(b_tot)
        if sb < 2:
            short_conv_group((2 + sb) * MXU_N)

    gng = gng_ref[...]
    for hh in range(HEADS):
        kcols = slice(hh * DK, (hh + 1) * DK)
        vcols = slice(hh * DV, (hh + 1) * DV)
        st = st_ref[hh]
        for sb in range(tm // MXU_K):
            rows = slice(sb * MXU_K, (sb + 1) * MXU_K)
            sc = jnp.where(in_chunk_causal, _dot_nt(qe_ref[rows, kcols], ke_ref[rows, kcols]), 0.0)
            o_intra = _dot(sc.astype(BF16), v_ref[rows, vcols])
            for c in range(MXU_K // CHUNK):
                r0 = sb * MXU_K + c * CHUNK
                crow = slice(r0, r0 + CHUNK)
                o = o_intra[c * CHUNK:(c + 1) * CHUNK, :] + _dot_nt(qe_ref[crow, kcols], st.astype(BF16))
                mix_ref[crow, vcols] = _head_rmsnorm(o, gng)
                dec = dec_ref[r0:r0 + 1, kcols]
                st = st * dec + _dot_tn(v_ref[crow, vcols], kd_ref[crow, kcols])
        st_ref[hh] = st

    o_mix = (mix_ref[...] * _silu(_proj(h, wmain_ref, COL_G, COL_A_LOW))).astype(BF16)

    ubuf_ref[0:SCONV_HALO, :] = ubuf_ref[tm:tm + SCONV_HALO, :]

    out = _dot(o_mix, wout_ref[0:V_WIDTH, :]) + _dot(ysc_ref[...], wout_ref[V_WIDTH:V_WIDTH + CONV_W, :])
    x1_ref[...] = x + out

    @pl.when(t == last_t)
    def _():
        for hh in range(HEADS):
            sgla_ref[hh] = st_ref[hh].T
        sconv_ref[...] = ubuf_ref[pl.ds(SCONV_HALO + tm - (SCONV_K - 1), SCONV_K - 1), :]


def _odd_prompt_kernel(x_ref, ng_ref, win_ref, bin_ref, bdw_ref, lng_ref, lnb_ref, wout_ref,
                       bout_ref, fng_ref, fwd_ref, inv_ref, basis_ref, taps_ref,
                       y_ref, cconv_ref,
                       ub_ref, yc_ref, tail_ref, z_ref, resp_ref):
    tm = ODD_TILE
    t = pl.program_id(1)
    last_t = pl.num_programs(1) - 1

    @pl.when((pl.program_id(0) == 0) & (t == 0))
    def _():
        resp_ref[...] = jnp.dot(basis_ref[...], taps_ref[...], precision=lax.Precision.HIGHEST,
                                preferred_element_type=F32)

    @pl.when(t == 0)
    def _():
        ub_ref[0:CCONV_HALO, :] = jnp.zeros((CCONV_HALO, CONV_W), BF16)

    x = x_ref[...]
    h = _rmsnorm(x, ng_ref[...]).astype(BF16)
    for g0 in range(0, CONV_W, MXU_N):
        cols = slice(g0, g0 + MXU_N)
        gcols = slice(CONV_W + g0, CONV_W + g0 + MXU_N)
        u = _gate(_dot(h, win_ref[:, cols]) + bin_ref[:, cols], _dot(h, win_ref[:, gcols]) + bin_ref[:, gcols])
        tail_ref[:, cols] = u[tm - CCONV_HALO:, :]
        ub_ref[CCONV_HALO:CCONV_HALO + tm, cols] = u.astype(BF16)

    @pl.when(t == last_t)
    def _():
        cconv_ref[...] = tail_ref[CCONV_HALO - (CCONV_K - 1):, :]

    bdw = bdw_ref[...]
    ha = resp_ref[0:DFT_HALF, :]
    hb = resp_ref[DFT_HALF:2 * DFT_HALF, :]
    ha2 = resp_ref[2 * DFT_HALF:3 * DFT_HALF, :]
    fwd = fwd_ref[...].astype(BF16)
    inv = inv_ref[...].astype(BF16)
    starts = sorted({min(s0, tm - DFT_HOP) for s0 in range(0, tm, DFT_HOP)})
    z_groups = list(range(0, CONV_W, MXU_N))
    for i, start in enumerate(starts):
        spec = _dot(fwd, ub_ref[start:start + DFT_N, :])
        p = spec[0:DFT_HALF, :]
        q = spec[DFT_HALF:DFT_N, :]
        prod = jnp.concatenate([p * ha - q * hb, p * hb + q * ha2], axis=0).astype(BF16)
        yc_ref[start:start + DFT_HOP, :] = _dot(inv, prod) + bdw
        if i < len(z_groups):
            zc = slice(2 * CONV_W + z_groups[i], 2 * CONV_W + z_groups[i] + MXU_N)
            z_ref[:, z_groups[i]:z_groups[i] + MXU_N] = _dot(h, win_ref[:, zc]) + bin_ref[:, zc]
    assert len(starts) >= len(z_groups)

    ub_ref[0:CCONV_HALO, :] = ub_ref[tm:tm + CCONV_HALO, :]

    act = _layernorm_act(yc_ref[...], z_ref[...], lng_ref[...], lnb_ref[...])
    y_ref[...] = _rmsnorm(x + _dot(act, wout_ref[...]) + bout_ref[...], fng_ref[...])


def _even_decode_front_kernel(x_ref, ng_ref, wmain_ref, wup_ref, bup_ref, wsc_ref, sbuf_ref,
                              q_ref, k_ref, a_ref, v_ref, sg_ref, y_ref, snew_ref):
    h = _rmsnorm(x_ref[...], ng_ref[...]).astype(BF16)
    q_ref[...] = _proj(h, wmain_ref, COL_Q, COL_K) * Q_SCALE
    k_ref[...] = _proj(h, wmain_ref, COL_K, COL_V)
    v_ref[...] = _proj(h, wmain_ref, COL_V, COL_G)
    sg_ref[...] = _silu(_proj(h, wmain_ref, COL_G, COL_A_LOW))
    a_ref[...] = jnp.exp(_log_decay(h, wmain_ref, wup_ref, bup_ref))
    u = _proj(h, wmain_ref, COL_GATE_C, COL_ZB) * _proj(h, wmain_ref, COL_HB, COL_GATE_B)
    prev2 = sbuf_ref[:, 0:CONV_W]
    prev1 = sbuf_ref[:, CONV_W:2 * CONV_W]
    y_ref[...] = _short_conv_gate(u, prev1, prev2, _proj(h, wmain_ref, COL_GATE_B, COL_GATE_C),
                                  _proj(h, wmain_ref, COL_ZB, MAIN_W), wsc_ref)
    snew_ref[:, 0:CONV_W] = prev1
    snew_ref[:, CONV_W:2 * CONV_W] = u


def _lane_bcast_column(row):
    return jnp.broadcast_to(row, (DK, DK)).T


def _gla_decode_kernel(q_ref, k_ref, a_ref, v_ref, s_ref, snew_ref, o_ref):
    for b in range(DECODE_STATE_BLOCK):
        for hh in range(HEADS):
            kcols = slice(hh * DK, (hh + 1) * DK)
            vcols = slice(hh * DV, (hh + 1) * DV)
            a_col = _lane_bcast_column(a_ref[b:b + 1, kcols])
            k_col = _lane_bcast_column(k_ref[b:b + 1, kcols])
            q_col = _lane_bcast_column(q_ref[b:b + 1, kcols])
            v_row = v_ref[b:b + 1, vcols]
            halves = []
            for half in range(DV // DK):
                lanes = slice(half * DK, (half + 1) * DK)
                s_new = a_col * s_ref[b, hh, :, lanes] + k_col * v_row[:, lanes]
                snew_ref[b, hh, :, lanes] = s_new
                halves.append(jnp.sum(q_col * s_new, axis=0, keepdims=True))
            o_ref[b:b + 1, vcols] = jnp.concatenate(halves, axis=1)


def _even_decode_out_kernel(x_ref, o_ref, sg_ref, y_ref, gng_ref, wout_ref, x1_ref):
    gng = gng_ref[...]
    parts = [_head_rmsnorm(o_ref[:, hh * DV:(hh + 1) * DV], gng) for hh in range(HEADS)]
    o_mix = (jnp.concatenate(parts, axis=1) * sg_ref[...]).astype(BF16)
    out = _dot(o_mix, wout_ref[0:V_WIDTH, :]) + _dot(y_ref[...].astype(BF16), wout_ref[V_WIDTH:V_WIDTH + CONV_W, :])
    x1_ref[...] = x_ref[...] + out


def _odd_decode_kernel(x_ref, ng_ref, win_ref, bin_ref, wdw_ref, bdw_ref, lng_ref, lnb_ref, wout_ref,
                       bout_ref, fng_ref, cbuf_ref,
                       y_ref, cnew_ref):
    n_hist = CCONV_K - 1
    x = x_ref[...]
    h = _rmsnorm(x, ng_ref[...]).astype(BF16)
    a = _dot(h, win_ref[:, 0:CONV_W]) + bin_ref[:, 0:CONV_W]
    a_gate = _dot(h, win_ref[:, CONV_W:2 * CONV_W]) + bin_ref[:, CONV_W:2 * CONV_W]
    u = _gate(a, a_gate)
    yc = bdw_ref[...] + wdw_ref[n_hist:CCONV_K, :] * u
    for j in range(n_hist):
        tap = cbuf_ref[j]
        yc = yc + wdw_ref[j:j + 1, :] * tap
        if j >= 1:
            cnew_ref[j - 1] = tap
    cnew_ref[n_hist - 1] = u
    z = _dot(h, win_ref[:, 2 * CONV_W:3 * CONV_W]) + bin_ref[:, 2 * CONV_W:3 * CONV_W]
    act = _layernorm_act(yc, z, lng_ref[...], lnb_ref[...])
    y_ref[...] = _rmsnorm(x + _dot(act, wout_ref[...]) + bout_ref[...], fng_ref[...])


def _const_spec(shape):
    nd = len(shape)
    return pl.BlockSpec(shape, lambda *_: (0,) * nd, pipeline_mode=pl.Buffered(1))


def _row(v):
    return v.reshape(1, -1)


def kernel(x_prompt, x_sample, state_gla, state_sconv, state_cconv, norm_g, w_in_a, w_gate_up, b_gate_up, gla_norm_g, w_sconv, w_out_a, w_in_c, b_in_c, w_dwconv, b_dwconv, ln_g, ln_b, w_out_c, b_out_c, final_norm_g):
    bsz, seq, d = x_prompt.shape
    dec_b = x_sample.shape[0]
    assert d == D_MODEL and seq % EVEN_TILE == 0 and seq % ODD_TILE == 0 and x_sample.shape[1] == 1
    assert EVEN_TILE // MXU_K >= 2 and CONV_W == 4 * MXU_N and ODD_TILE >= DFT_HOP
    assert w_in_a.shape[0] == 1 and w_in_c.shape[0] == 1 and norm_g.shape[0] == 2

    assert w_in_a.shape[2] == MAIN_W
    wmain = w_in_a[0].T.astype(BF16)
    wup = jnp.pad(w_gate_up[0], ((0, GATE_RANK_PAD - GATE_RANK), (0, 0))).astype(BF16)
    bup = _row(b_gate_up[0])
    gng = _row(gla_norm_g[0])
    wsc = w_sconv[0]
    wout_a = w_out_a[0].astype(BF16)
    ng0 = _row(norm_g[0])
    ng1 = _row(norm_g[1])
    win_c = w_in_c[0].astype(BF16)
    bin_c = _row(b_in_c[0])
    wdw = w_dwconv[0]
    bdw = _row(b_dwconv[0])
    lng = _row(ln_g[0])
    lnb = _row(ln_b[0])
    wout_c = w_out_c[0].astype(BF16)
    bout = _row(b_out_c[0])
    fng = _row(final_norm_g)

    even_weights = (ng0, wmain, wup, bup)
    even_prompt_weights = even_weights + (gng, wsc, wout_a)
    odd_weights = (ng1, win_c, bin_c, wdw, bdw, lng, lnb, wout_c, bout, fng)
    odd_weight_specs = [_const_spec(w.shape) for w in odd_weights]
    hf = DFT_HALF
    assert CCONV_HALO >= CCONV_K - 1 and DFT_HOP % 16 == 0
    kk = np.arange(hf)[:, None]

    def packed_basis(pos):
        ang = 2.0 * np.pi * kk * pos[None, :] / DFT_N
        lower = -np.sin(ang)
        lower[0] = np.cos(np.pi * pos)
        return np.cos(ang), lower

    fc, fs = packed_basis(np.arange(DFT_N, dtype=np.float64))
    fwd = np.concatenate([fc, fs], axis=0).astype(np.float32)
    ic, isn = packed_basis(np.arange(CCONV_HALO, DFT_N, dtype=np.float64))
    weight = np.full((hf, 1), 2.0 / DFT_N)
    weight[0] = 1.0 / DFT_N
    inv = np.concatenate([weight * ic, weight * isn], axis=0).T.astype(np.float32)
    hc, hs = packed_basis(np.arange(CCONV_K, dtype=np.float64))
    hs_imag = hs.copy()
    hs_imag[0] = 0.0
    hc_alt = hc.copy()
    hc_alt[0] = hs[0]
    basis = np.zeros((3 * hf, RESP_K_PAD), np.float32)
    basis[:, :CCONV_K] = np.concatenate([hc, hs_imag, hc_alt], axis=0)
    taps_by_lag = jnp.pad(wdw[::-1], ((0, RESP_K_PAD - CCONV_K), (0, 0)))
    odd_prompt_weights = (ng1, win_c, bin_c, bdw, lng, lnb, wout_c, bout, fng,
                          jnp.asarray(fwd), jnp.asarray(inv), jnp.asarray(basis), taps_by_lag)

    prompt_params = pltpu.CompilerParams(dimension_semantics=("arbitrary", "arbitrary"),
                                         vmem_limit_bytes=VMEM_LIMIT)

    tm = EVEN_TILE
    tile_spec = pl.BlockSpec((None, tm, D_MODEL), lambda b, t: (b, t, 0))
    x1_p, gla_p, sconv_p = pl.pallas_call(
        _even_prompt_kernel,
        grid=(bsz, seq // tm),
        in_specs=[tile_spec] + [_const_spec(w.shape) for w in even_prompt_weights],
        out_specs=[tile_spec,
                   pl.BlockSpec((None, None, HEADS, DK, DV), lambda b, t: (0, b, 0, 0, 0)),
                   pl.BlockSpec((None, None, SCONV_K - 1, CONV_W), lambda b, t: (0, b, 0, 0))],
        out_shape=[jax.ShapeDtypeStruct((bsz, seq, D_MODEL), F32),
                   jax.ShapeDtypeStruct((1, bsz, HEADS, DK, DV), F32),
                   jax.ShapeDtypeStruct((1, bsz, SCONV_K - 1, CONV_W), F32)],
        scratch_shapes=[pltpu.VMEM((HEADS, DV, DK), F32),
                        pltpu.VMEM((tm + SCONV_HALO, CONV_W), F32),
                        pltpu.VMEM((tm, QK_WIDTH), BF16),
                        pltpu.VMEM((tm, QK_WIDTH), BF16),
                        pltpu.VMEM((tm, QK_WIDTH), BF16),
                        pltpu.VMEM((tm, V_WIDTH), BF16),
                        pltpu.VMEM((tm, QK_WIDTH), F32),
                        pltpu.VMEM((tm, V_WIDTH), F32),
                        pltpu.VMEM((tm, CONV_W), BF16)],
        compiler_params=prompt_params,
        name="even_prompt",
    )(x_prompt, *even_prompt_weights)

    tm = ODD_TILE
    tile_spec = pl.BlockSpec((None, tm, D_MODEL), lambda b, t: (b, t, 0))
    y_p, cconv_p = pl.pallas_call(
        _odd_prompt_kernel,
        grid=(bsz, seq // tm),
        in_specs=[tile_spec] + [_const_spec(w.shape) for w in odd_prompt_weights],
        out_specs=[tile_spec,
                   pl.BlockSpec((None, None, CCONV_K - 1, CONV_W), lambda b, t: (0, b, 0, 0))],
        out_shape=[jax.ShapeDtypeStruct((bsz, seq, D_MODEL), F32),
                   jax.ShapeDtypeStruct((1, bsz, CCONV_K - 1, CONV_W), F32)],
        scratch_shapes=[pltpu.VMEM((tm + CCONV_HALO, CONV_W), BF16),
                        pltpu.VMEM((tm, CONV_W), F32),
                        pltpu.VMEM((CCONV_HALO, CONV_W), F32),
                        pltpu.VMEM((tm, CONV_W), F32),
                        pltpu.VMEM((3 * DFT_HALF, CONV_W), F32)],
        compiler_params=prompt_params,
        name="odd_prompt",
    )(x1_p, *odd_prompt_weights)

    xs = x_sample.reshape(dec_b, D_MODEL)
    sbuf = state_sconv.reshape(dec_b, (SCONV_K - 1) * CONV_W)
    single = pltpu.CompilerParams(vmem_limit_bytes=VMEM_LIMIT)
    q_s, k_s, a_s, v_s, sg_s, ysc_s, sconv_s = pl.pallas_call(
        _even_decode_front_kernel,
        out_shape=[jax.ShapeDtypeStruct((dec_b, QK_WIDTH), F32),
                   jax.ShapeDtypeStruct((dec_b, QK_WIDTH), F32),
                   jax.ShapeDtypeStruct((dec_b, QK_WIDTH), F32),
                   jax.ShapeDtypeStruct((dec_b, V_WIDTH), F32),
                   jax.ShapeDtypeStruct((dec_b, V_WIDTH), F32),
                   jax.ShapeDtypeStruct((dec_b, CONV_W), F32),
                   jax.ShapeDtypeStruct((dec_b, (SCONV_K - 1) * CONV_W), F32)],
        compiler_params=single,
        name="even_decode_front",
    )(xs, *even_weights, wsc, sbuf)

    sb = DECODE_STATE_BLOCK
    assert dec_b % sb == 0
    vec_spec = lambda w: pl.BlockSpec((sb, w), lambda i: (i, 0))
    state_spec = pl.BlockSpec((sb, HEADS, DK, DV), lambda i: (i, 0, 0, 0))
    state_in_spec = pl.BlockSpec((sb, HEADS, DK, DV), lambda i: (i, 0, 0, 0),
                                 pipeline_mode=pl.Buffered(DECODE_STATE_BUFFERS))
    def gla_decode_streamed(*hbm_refs):
        pltpu.emit_pipeline(
            _gla_decode_kernel,
            grid=(dec_b // sb,),
            in_specs=[vec_spec(QK_WIDTH), vec_spec(QK_WIDTH), vec_spec(QK_WIDTH), vec_spec(V_WIDTH),
                      state_in_spec],
            out_specs=[state_spec, vec_spec(V_WIDTH)],
        )(*hbm_refs)

    any_spec = pl.BlockSpec(memory_space=pl.ANY)
    gla_s, o_s = pl.pallas_call(
        gla_decode_streamed,
        in_specs=[any_spec] * 5,
        out_specs=[any_spec] * 2,
        out_shape=[jax.ShapeDtypeStruct((dec_b, HEADS, DK, DV), F32),
                   jax.ShapeDtypeStruct((dec_b, V_WIDTH), F32)],
        compiler_params=pltpu.CompilerParams(vmem_limit_bytes=VMEM_LIMIT),
        name="gla_decode",
    )(q_s, k_s, a_s, v_s, state_gla[0])

    x1_s = pl.pallas_call(
        _even_decode_out_kernel,
        out_shape=jax.ShapeDtypeStruct((dec_b, D_MODEL), F32),
        compiler_params=single,
        name="even_decode_out",
    )(xs, o_s, sg_s, ysc_s, gng, wout_a)

    ob = DECODE_ODD_BLOCK
    assert dec_b % ob == 0
    rows_spec = pl.BlockSpec((ob, D_MODEL), lambda i: (i, 0))
    hist_spec = pl.BlockSpec((CCONV_K - 1, ob, CONV_W), lambda i: (0, i, 0))
    cbuf = jnp.transpose(state_cconv[0], (1, 0, 2))
    y_s, cconv_t = pl.pallas_call(
        _odd_decode_kernel,
        grid=(dec_b // ob,),
        in_specs=[rows_spec] + odd_weight_specs + [hist_spec],
        out_specs=[rows_spec, hist_spec],
        out_shape=[jax.ShapeDtypeStruct((dec_b, D_MODEL), F32),
                   jax.ShapeDtypeStruct((CCONV_K - 1, dec_b, CONV_W), F32)],
        compiler_params=pltpu.CompilerParams(dimension_semantics=("arbitrary",),
                                             vmem_limit_bytes=VMEM_LIMIT),
        name="odd_decode",
    )(x1_s, *odd_weights, cbuf)
    cconv_s = jnp.transpose(cconv_t, (1, 0, 2))[None]

    return (y_p,
            y_s.reshape(dec_b, 1, D_MODEL),
            gla_p,
            sconv_p,
            cconv_p,
            gla_s.reshape(1, dec_b, HEADS, DK, DV),
            sconv_s.reshape(1, dec_b, SCONV_K - 1, CONV_W),
            cconv_s)
```

```python
import jax
import jax.numpy as jnp
import numpy as np
from jax import lax
from jax.experimental import pallas as pl
from jax.experimental.pallas import tpu as pltpu

F32 = jnp.float32
BF16 = jnp.bfloat16

D_MODEL = 1024
HEADS = 4
DK = 128
DV = 256
QK_WIDTH = HEADS * DK
V_WIDTH = HEADS * DV
GATE_RANK = 16
GATE_RANK_PAD = 128
GATE_TEMP_INV = 1.0 / 16.0
CHUNK = 64
CHUNK_SHIFT = 6
SCONV_K = 3
CCONV_K = 31
CONV_W = 1024
RMS_EPS = 1e-6
LN_EPS = 1e-5
Q_SCALE = DK ** -0.5

COL_Q = 0
COL_K = COL_Q + QK_WIDTH
COL_V = COL_K + QK_WIDTH
COL_G = COL_V + V_WIDTH
COL_A_LOW = COL_G + V_WIDTH
COL_HB = COL_A_LOW + GATE_RANK
COL_GATE_B = COL_HB + CONV_W
COL_GATE_C = COL_GATE_B + CONV_W
COL_ZB = COL_GATE_C + CONV_W
MAIN_W = COL_ZB + CONV_W

MXU_K = 256
MXU_N = 256
EVEN_TILE = 512
ODD_TILE = 1024
CCONV_HALO = 32
SCONV_HALO = 8
DFT_N = MXU_K
DFT_HALF = DFT_N // 2
DFT_HOP = DFT_N - CCONV_HALO
RESP_K_PAD = 128
DECODE_STATE_BLOCK = 8
DECODE_STATE_BUFFERS = 4
DECODE_ODD_BLOCK = 64
VMEM_LIMIT = 60 * 1024 * 1024


def _dot(a, b):
    return jnp.dot(a, b, preferred_element_type=F32)


def _dot_nt(a, b):
    return lax.dot_general(a, b, (((1,), (1,)), ((), ())), preferred_element_type=F32)


def _dot_tn(a, b):
    return lax.dot_general(a, b, (((0,), (0,)), ((), ())), preferred_element_type=F32)


def _proj(h, wt_ref, lo, hi):
    return _dot_nt(h, wt_ref[lo:hi, :])


def _rmsnorm(x, g):
    ms = jnp.mean(x * x, axis=-1, keepdims=True)
    return x * lax.rsqrt(ms + RMS_EPS) * g


def _gate(x, y):
    return x / (1.0 + jnp.exp(-y))


def _silu(x):
    return _gate(x, x)


def _log_sigmoid(x):
    return -(jnp.maximum(-x, 0.0) + jnp.log(1.0 + jnp.exp(-jnp.abs(x))))


def _log_decay(h, wmain_ref, wup_ref, bup_ref):
    a_low = _proj(h, wmain_ref, COL_A_LOW, COL_A_LOW + GATE_RANK_PAD).astype(BF16)
    logit = _dot(a_low, wup_ref[...]) + bup_ref[...]
    return _log_sigmoid(logit) * GATE_TEMP_INV


def _head_rmsnorm(o, gng):
    ms = jnp.mean(o * o, axis=-1, keepdims=True)
    return o * lax.rsqrt(ms + RMS_EPS) * gng


def _layernorm_act(yc, z, lng, lnb):
    mu = jnp.mean(yc, axis=-1, keepdims=True)
    xc = yc - mu
    var = jnp.mean(xc * xc, axis=-1, keepdims=True)
    yn = xc * lax.rsqrt(var + LN_EPS) * lng + lnb
    return (_silu(yn) * _silu(z)).astype(BF16)


def _short_conv_gate(u, prev1, prev2, gate_b, z_b, wsc_ref):
    y = wsc_ref[2:3, :] * u + wsc_ref[1:2, :] * prev1 + wsc_ref[0:1, :] * prev2
    return gate_b * y * _silu(z_b)


def _even_prompt_kernel(x_ref, ng_ref, wmain_ref, wup_ref, bup_ref, gng_ref, wsc_ref, wout_ref,
                        x1_ref, sgla_ref, sconv_ref,
                        st_ref, ubuf_ref, qe_ref, ke_ref, kd_ref, v_ref, dec_ref, mix_ref, ysc_ref):
    tm = EVEN_TILE
    t = pl.program_id(1)
    last_t = pl.num_programs(1) - 1

    @pl.when(t == 0)
    def _():
        st_ref[...] = jnp.zeros_like(st_ref)
        ubuf_ref[0:SCONV_HALO, :] = jnp.zeros((SCONV_HALO, CONV_W), F32)

    x = x_ref[...]
    h = _rmsnorm(x, ng_ref[...]).astype(BF16)

    def short_conv_group(g0):
        cols = slice(g0, g0 + MXU_N)
        part = lambda c0: _proj(h, wmain_ref, c0 + g0, c0 + g0 + MXU_N)
        u = part(COL_GATE_C) * part(COL_HB)
        ubuf_ref[SCONV_HALO:SCONV_HALO + tm, cols] = u
        y = (wsc_ref[2:3, cols] * u + wsc_ref[1:2, cols] * ubuf_ref[pl.ds(SCONV_HALO - 1, tm), cols]
             + wsc_ref[0:1, cols] * ubuf_ref[pl.ds(SCONV_HALO - 2, tm), cols])
        ysc_ref[:, cols] = (part(COL_GATE_B) * y * _silu(part(COL_ZB))).astype(BF16)

    q = _proj(h, wmain_ref, COL_Q, COL_K) * Q_SCALE
    k = _proj(h, wmain_ref, COL_K, COL_V)
    v_ref[...] = _proj(h, wmain_ref, COL_V, COL_G).astype(BF16)
    a_low = _proj(h, wmain_ref, COL_A_LOW, COL_A_LOW + GATE_RANK_PAD).astype(BF16)
    short_conv_group(0)
    log_a = _log_sigmoid(_dot(a_low, wup_ref[...]) + bup_ref[...]) * GATE_TEMP_INV
    short_conv_group(MXU_N)

    row = lax.broadcasted_iota(jnp.int32, (MXU_K, MXU_K), 0)
    col = lax.broadcasted_iota(jnp.int32, (MXU_K, MXU_K), 1)
    in_chunk_causal = ((row >> CHUNK_SHIFT) == (col >> CHUNK_SHIFT)) & (col <= row)
    tri = jnp.where(in_chunk_causal, 1.0, 0.0).astype(BF16)
    la_hi = log_a.astype(BF16)
    la_lo = (log_a - la_hi.astype(F32)).astype(BF16)
    for sb in range(tm // MXU_K):
        rows = slice(sb * MXU_K, (sb + 1) * MXU_K)
        b_cum = _dot(tri, la_hi[rows, :]) + _dot(tri, la_lo[rows, :])
        b_tot = jnp.concatenate(
            [jnp.broadcast_to(b_cum[(c + 1) * CHUNK - 1:(c + 1) * CHUNK, :], (CHUNK, QK_WIDTH))
             for c in range(MXU_K // CHUNK)], axis=0)
        qe_ref[rows, :] = (q[rows, :] * jnp.exp(b_cum)).astype(BF16)
        ke_ref[rows, :] = (k[rows, :] * jnp.exp(-b_cum)).astype(BF16)
        kd_ref[rows, :] = (k[rows, :] * jnp.exp(b_tot - b_cum)).astype(BF16)
        dec_ref[rows, :] = jnp.exp(b_tot)
        if sb < 2:
            short_conv_group((2 + sb) * MXU_N)

    gng = gng_ref[...]
    for hh in range(HEADS):
        kcols = slice(hh * DK, (hh + 1) * DK)
        vcols = slice(hh * DV, (hh + 1) * DV)
        st = st_ref[hh]
        for sb in range(tm // MXU_K):
            rows = slice(sb * MXU_K, (sb + 1) * MXU_K)
            sc = jnp.where(in_chunk_causal, _dot_nt(qe_ref[rows, kcols], ke_ref[rows, kcols]), 0.0)
            o_intra = _dot(sc.astype(BF16), v_ref[rows, vcols])
            for c in range(MXU_K // CHUNK):
                r0 = sb * MXU_K + c * CHUNK
                crow = slice(r0, r0 + CHUNK)
                o = o_intra[c * CHUNK:(c + 1) * CHUNK, :] + _dot_nt(qe_ref[crow, kcols], st.astype(BF16))
                mix_ref[crow, vcols] = _head_rmsnorm(o, gng)
                dec = dec_ref[r0:r0 + 1, kcols]
                st = st * dec + _dot_tn(v_ref[crow, vcols], kd_ref[crow, kcols])
        st_ref[hh] = st

    o_mix = (mix_ref[...] * _silu(_proj(h, wmain_ref, COL_G, COL_A_LOW))).astype(BF16)

    ubuf_ref[0:SCONV_HALO, :] = ubuf_ref[tm:tm + SCONV_HALO, :]

    out = _dot(o_mix, wout_ref[0:V_WIDTH, :]) + _dot(ysc_ref[...], wout_ref[V_WIDTH:V_WIDTH + CONV_W, :])
    x1_ref[...] = x + out

    @pl.when(t == last_t)
    def _():
        for hh in range(HEADS):
            sgla_ref[hh] = st_ref[hh].T
        sconv_ref[...] = ubuf_ref[pl.ds(SCONV_HALO + tm - (SCONV_K - 1), SCONV_K - 1), :]


def _odd_prompt_kernel(x_ref, ng_ref, win_ref, bin_ref, bdw_ref, lng_ref, lnb_ref, wout_ref,
                       bout_ref, fng_ref, fwd_ref, inv_ref, basis_ref, taps_ref,
                       y_ref, cconv_ref,
                       ub_ref, yc_ref, tail_ref, z_ref, resp_ref):
    tm = ODD_TILE
    t = pl.program_id(1)
    last_t = pl.num_programs(1) - 1

    @pl.when((pl.program_id(0) == 0) & (t == 0))
    def _():
        resp_ref[...] = jnp.dot(basis_ref[...], taps_ref[...], precision=lax.Precision.HIGHEST,
                                preferred_element_type=F32)

    @pl.when(t == 0)
    def _():
        ub_ref[0:CCONV_HALO, :] = jnp.zeros((CCONV_HALO, CONV_W), BF16)

    x = x_ref[...]
    h = _rmsnorm(x, ng_ref[...]).astype(BF16)
    for g0 in range(0, CONV_W, MXU_N):
        cols = slice(g0, g0 + MXU_N)
        gcols = slice(CONV_W + g0, CONV_W + g0 + MXU_N)
        u = _gate(_dot(h, win_ref[:, cols]) + bin_ref[:, cols], _dot(h, win_ref[:, gcols]) + bin_ref[:, gcols])
        tail_ref[:, cols] = u[tm - CCONV_HALO:, :]
        ub_ref[CCONV_HALO:CCONV_HALO + tm, cols] = u.astype(BF16)

    @pl.when(t == last_t)
    def _():
        cconv_ref[...] = tail_ref[CCONV_HALO - (CCONV_K - 1):, :]

    bdw = bdw_ref[...]
    ha = resp_ref[0:DFT_HALF, :]
    hb = resp_ref[DFT_HALF:2 * DFT_HALF, :]
    ha2 = resp_ref[2 * DFT_HALF:3 * DFT_HALF, :]
    fwd = fwd_ref[...].astype(BF16)
    inv = inv_ref[...].astype(BF16)
    starts = sorted({min(s0, tm - DFT_HOP) for s0 in range(0, tm, DFT_HOP)})
    z_groups = list(range(0, CONV_W, MXU_N))
    for i, start in enumerate(starts):
        spec = _dot(fwd, ub_ref[start:start + DFT_N, :])
        p = spec[0:DFT_HALF, :]
        q = spec[DFT_HALF:DFT_N, :]
        prod = jnp.concatenate([p * ha - q * hb, p * hb + q * ha2], axis=0).astype(BF16)
        yc_ref[start:start + DFT_HOP, :] = _dot(inv, prod) + bdw
        if i < len(z_groups):
            zc = slice(2 * CONV_W + z_groups[i], 2 * CONV_W + z_groups[i] + MXU_N)
            z_ref[:, z_groups[i]:z_groups[i] + MXU_N] = _dot(h, win_ref[:, zc]) + bin_ref[:, zc]
    assert len(starts) >= len(z_groups)

    ub_ref[0:CCONV_HALO, :] = ub_ref[tm:tm + CCONV_HALO, :]

    act = _layernorm_act(yc_ref[...], z_ref[...], lng_ref[...], lnb_ref[...])
    y_ref[...] = _rmsnorm(x + _dot(act, wout_ref[...]) + bout_ref[...], fng_ref[...])


def _even_decode_front_kernel(x_ref, ng_ref, wmain_ref, wup_ref, bup_ref, wsc_ref, sbuf_ref,
                              q_ref, k_ref, a_ref, v_ref, sg_ref, y_ref, snew_ref):
    h = _rmsnorm(x_ref[...], ng_ref[...]).astype(BF16)
    q_ref[...] = _proj(h, wmain_ref, COL_Q, COL_K) * Q_SCALE
    k_ref[...] = _proj(h, wmain_ref, COL_K, COL_V)
    v_ref[...] = _proj(h, wmain_ref, COL_V, COL_G)
    sg_ref[...] = _silu(_proj(h, wmain_ref, COL_G, COL_A_LOW))
    a_ref[...] = jnp.exp(_log_decay(h, wmain_ref, wup_ref, bup_ref))
    u = _proj(h, wmain_ref, COL_GATE_C, COL_ZB) * _proj(h, wmain_ref, COL_HB, COL_GATE_B)
    prev2 = sbuf_ref[:, 0:CONV_W]
    prev1 = sbuf_ref[:, CONV_W:2 * CONV_W]
    y_ref[...] = _short_conv_gate(u, prev1, prev2, _proj(h, wmain_ref, COL_GATE_B, COL_GATE_C),
                                  _proj(h, wmain_ref, COL_ZB, MAIN_W), wsc_ref)
    snew_ref[:, 0:CONV_W] = prev1
    snew_ref[:, CONV_W:2 * CONV_W] = u


def _lane_bcast_column(row):
    return jnp.broadcast_to(row, (DK, DK)).T


def _gla_decode_kernel(q_ref, k_ref, a_ref, v_ref, s_ref, snew_ref, o_ref):
    for b in range(DECODE_STATE_BLOCK):
        for hh in range(HEADS):
            kcols = slice(hh * DK, (hh + 1) * DK)
            vcols = slice(hh * DV, (hh + 1) * DV)
            a_col = _lane_bcast_column(a_ref[b:b + 1, kcols])
            k_col = _lane_bcast_column(k_ref[b:b + 1, kcols])
            q_col = _lane_bcast_column(q_ref[b:b + 1, kcols])
            v_row = v_ref[b:b + 1, vcols]
            halves = []
            for half in range(DV // DK):
                lanes = slice(half * DK, (half + 1) * DK)
                s_new = a_col * s_ref[b, hh, :, lanes] + k_col * v_row[:, lanes]
                snew_ref[b, hh, :, lanes] = s_new
                halves.append(jnp.sum(q_col * s_new, axis=0, keepdims=True))
            o_ref[b:b + 1, vcols] = jnp.concatenate(halves, axis=1)


def _even_decode_out_kernel(x_ref, o_ref, sg_ref, y_ref, gng_ref, wout_ref, x1_ref):
    gng = gng_ref[...]
    parts = [_head_rmsnorm(o_ref[:, hh * DV:(hh + 1) * DV], gng) for hh in range(HEADS)]
    o_mix = (jnp.concatenate(parts, axis=1) * sg_ref[...]).astype(BF16)
    out = _dot(o_mix, wout_ref[0:V_WIDTH, :]) + _dot(y_ref[...].astype(BF16), wout_ref[V_WIDTH:V_WIDTH + CONV_W, :])
    x1_ref[...] = x_ref[...] + out


def _odd_decode_kernel(x_ref, ng_ref, win_ref, bin_ref, wdw_ref, bdw_ref, lng_ref, lnb_ref, wout_ref,
                       bout_ref, fng_ref, cbuf_ref,
                       y_ref, cnew_ref):
    n_hist = CCONV_K - 1
    x = x_ref[...]
    h = _rmsnorm(x, ng_ref[...]).astype(BF16)
    a = _dot(h, win_ref[:, 0:CONV_W]) + bin_ref[:, 0:CONV_W]
    a_gate = _dot(h, win_ref[:, CONV_W:2 * CONV_W]) + bin_ref[:, CONV_W:2 * CONV_W]
    u = _gate(a, a_gate)
    yc = bdw_ref[...] + wdw_ref[n_hist:CCONV_K, :] * u
    for j in range(n_hist):
        tap = cbuf_ref[j]
        yc = yc + wdw_ref[j:j + 1, :] * tap
        if j >= 1:
            cnew_ref[j - 1] = tap
    cnew_ref[n_hist - 1] = u
    z = _dot(h, win_ref[:, 2 * CONV_W:3 * CONV_W]) + bin_ref[:, 2 * CONV_W:3 * CONV_W]
    act = _layernorm_act(yc, z, lng_ref[...], lnb_ref[...])
    y_ref[...] = _rmsnorm(x + _dot(act, wout_ref[...]) + bout_ref[...], fng_ref[...])


def _const_spec(shape):
    nd = len(shape)
    return pl.BlockSpec(shape, lambda *_: (0,) * nd, pipeline_mode=pl.Buffered(1))


def _row(v):
    return v.reshape(1, -1)


def kernel(x_prompt, x_sample, state_gla, state_sconv, state_cconv, norm_g, w_in_a, w_gate_up, b_gate_up, gla_norm_g, w_sconv, w_out_a, w_in_c, b_in_c, w_dwconv, b_dwconv, ln_g, ln_b, w_out_c, b_out_c, final_norm_g):
    bsz, seq, d = x_prompt.shape
    dec_b = x_sample.shape[0]
    assert d == D_MODEL and seq % EVEN_TILE == 0 and seq % ODD_TILE == 0 and x_sample.shape[1] == 1
    assert EVEN_TILE // MXU_K >= 2 and CONV_W == 4 * MXU_N and ODD_TILE >= DFT_HOP
    assert w_in_a.shape[0] == 1 and w_in_c.shape[0] == 1 and norm_g.shape[0] == 2

    assert w_in_a.shape[2] == MAIN_W
    wmain = w_in_a[0].T.astype(BF16)
    wup = jnp.pad(w_gate_up[0], ((0, GATE_RANK_PAD - GATE_RANK), (0, 0))).astype(BF16)
    bup = _row(b_gate_up[0])
    gng = _row(gla_norm_g[0])
    wsc = w_sconv[0]
    wout_a = w_out_a[0].astype(BF16)
    ng0 = _row(norm_g[0])
    ng1 = _row(norm_g[1])
    win_c = w_in_c[0].astype(BF16)
    bin_c = _row(b_in_c[0])
    wdw = w_dwconv[0]
    bdw = _row(b_dwconv[0])
    lng = _row(ln_g[0])
    lnb = _row(ln_b[0])
    wout_c = w_out_c[0].astype(BF16)
    bout = _row(b_out_c[0])
    fng = _row(final_norm_g)

    even_weights = (ng0, wmain, wup, bup)
    even_prompt_weights = even_weights + (gng, wsc, wout_a)
    odd_weights = (ng1, win_c, bin_c, wdw, bdw, lng, lnb, wout_c, bout, fng)
    odd_weight_specs = [_const_spec(w.shape) for w in odd_weights]
    hf = DFT_HALF
    assert CCONV_HALO >= CCONV_K - 1 and DFT_HOP % 16 == 0
    kk = np.arange(hf)[:, None]

    def packed_basis(pos):
        ang = 2.0 * np.pi * kk * pos[None, :] / DFT_N
        lower = -np.sin(ang)
        lower[0] = np.cos(np.pi * pos)
        return np.cos(ang), lower

    fc, fs = packed_basis(np.arange(DFT_N, dtype=np.float64))
    fwd = np.concatenate([fc, fs], axis=0).astype(np.float32)
    ic, isn = packed_basis(np.arange(CCONV_HALO, DFT_N, dtype=np.float64))
    weight = np.full((hf, 1), 2.0 / DFT_N)
    weight[0] = 1.0 / DFT_N
    inv = np.concatenate([weight * ic, weight * isn], axis=0).T.astype(np.float32)
    hc, hs = packed_basis(np.arange(CCONV_K, dtype=np.float64))
    hs_imag = hs.copy()
    hs_imag[0] = 0.0
    hc_alt = hc.copy()
    hc_alt[0] = hs[0]
    basis = np.zeros((3 * hf, RESP_K_PAD), np.float32)
    basis[:, :CCONV_K] = np.concatenate([hc, hs_imag, hc_alt], axis=0)
    taps_by_lag = jnp.pad(wdw[::-1], ((0, RESP_K_PAD - CCONV_K), (0, 0)))
    odd_prompt_weights = (ng1, win_c, bin_c, bdw, lng, lnb, wout_c, bout, fng,
                          jnp.asarray(fwd), jnp.asarray(inv), jnp.asarray(basis), taps_by_lag)

    prompt_params = pltpu.CompilerParams(dimension_semantics=("arbitrary", "arbitrary"),
                                         vmem_limit_bytes=VMEM_LIMIT)

    tm = EVEN_TILE
    tile_spec = pl.BlockSpec((None, tm, D_MODEL), lambda b, t: (b, t, 0))
    x1_p, gla_p, sconv_p = pl.pallas_call(
        _even_prompt_kernel,
        grid=(bsz, seq // tm),
        in_specs=[tile_spec] + [_const_spec(w.shape) for w in even_prompt_weights],
        out_specs=[tile_spec,
                   pl.BlockSpec((None, None, HEADS, DK, DV), lambda b, t: (0, b, 0, 0, 0)),
                   pl.BlockSpec((None, None, SCONV_K - 1, CONV_W), lambda b, t: (0, b, 0, 0))],
        out_shape=[jax.ShapeDtypeStruct((bsz, seq, D_MODEL), F32),
                   jax.ShapeDtypeStruct((1, bsz, HEADS, DK, DV), F32),
                   jax.ShapeDtypeStruct((1, bsz, SCONV_K - 1, CONV_W), F32)],
        scratch_shapes=[pltpu.VMEM((HEADS, DV, DK), F32),
                        pltpu.VMEM((tm + SCONV_HALO, CONV_W), F32),
                        pltpu.VMEM((tm, QK_WIDTH), BF16),
                        pltpu.VMEM((tm, QK_WIDTH), BF16),
                        pltpu.VMEM((tm, QK_WIDTH), BF16),
                        pltpu.VMEM((tm, V_WIDTH), BF16),
                        pltpu.VMEM((tm, QK_WIDTH), F32),
                        pltpu.VMEM((tm, V_WIDTH), F32),
                        pltpu.VMEM((tm, CONV_W), BF16)],
        compiler_params=prompt_params,
        name="even_prompt",
    )(x_prompt, *even_prompt_weights)

    tm = ODD_TILE
    tile_spec = pl.BlockSpec((None, tm, D_MODEL), lambda b, t: (b, t, 0))
    y_p, cconv_p = pl.pallas_call(
        _odd_prompt_kernel,
        grid=(bsz, seq // tm),
        in_specs=[tile_spec] + [_const_spec(w.shape) for w in odd_prompt_weights],
        out_specs=[tile_spec,
                   pl.BlockSpec((None, None, CCONV_K - 1, CONV_W), lambda b, t: (0, b, 0, 0))],
        out_shape=[jax.ShapeDtypeStruct((bsz, seq, D_MODEL), F32),
                   jax.ShapeDtypeStruct((1, bsz, CCONV_K - 1, CONV_W), F32)],
        scratch_shapes=[pltpu.VMEM((tm + CCONV_HALO, CONV_W), BF16),
                        pltpu.VMEM((tm, CONV_W), F32),
                        pltpu.VMEM((CCONV_HALO, CONV_W), F32),
                        pltpu.VMEM((tm, CONV_W), F32),
                        pltpu.VMEM((3 * DFT_HALF, CONV_W), F32)],
        compiler_params=prompt_params,
        name="odd_prompt",
    )(x1_p, *odd_prompt_weights)

    xs = x_sample.reshape(dec_b, D_MODEL)
    sbuf = state_sconv.reshape(dec_b, (SCONV_K - 1) * CONV_W)
    single = pltpu.CompilerParams(vmem_limit_bytes=VMEM_LIMIT)
    q_s, k_s, a_s, v_s, sg_s, ysc_s, sconv_s = pl.pallas_call(
        _even_decode_front_kernel,
        out_shape=[jax.ShapeDtypeStruct((dec_b, QK_WIDTH), F32),
                   jax.ShapeDtypeStruct((dec_b, QK_WIDTH), F32),
                   jax.ShapeDtypeStruct((dec_b, QK_WIDTH), F32),
                   jax.ShapeDtypeStruct((dec_b, V_WIDTH), F32),
                   jax.ShapeDtypeStruct((dec_b, V_WIDTH), F32),
                   jax.ShapeDtypeStruct((dec_b, CONV_W), F32),
                   jax.ShapeDtypeStruct((dec_b, (SCONV_K - 1) * CONV_W), F32)],
        compiler_params=single,
        name="even_decode_front",
    )(xs, *even_weights, wsc, sbuf)

    sb = DECODE_STATE_BLOCK
    assert dec_b % sb == 0
    vec_spec = lambda w: pl.BlockSpec((sb, w), lambda i: (i, 0))
    state_spec = pl.BlockSpec((sb, HEADS, DK, DV), lambda i: (i, 0, 0, 0))
    state_in_spec = pl.BlockSpec((sb, HEADS, DK, DV), lambda i: (i, 0, 0, 0),
                                 pipeline_mode=pl.Buffered(DECODE_STATE_BUFFERS))
    def gla_decode_streamed(*hbm_refs):
        pltpu.emit_pipeline(
            _gla_decode_kernel,
            grid=(dec_b // sb,),
            in_specs=[vec_spec(QK_WIDTH), vec_spec(QK_WIDTH), vec_spec(QK_WIDTH), vec_spec(V_WIDTH),
                      state_in_spec],
            out_specs=[state_spec, vec_spec(V_WIDTH)],
        )(*hbm_refs)

    any_spec = pl.BlockSpec(memory_space=pl.ANY)
    gla_s, o_s = pl.pallas_call(
        gla_decode_streamed,
        in_specs=[any_spec] * 5,
        out_specs=[any_spec] * 2,
        out_shape=[jax.ShapeDtypeStruct((dec_b, HEADS, DK, DV), F32),
                   jax.ShapeDtypeStruct((dec_b, V_WIDTH), F32)],
        compiler_params=pltpu.CompilerParams(vmem_limit_bytes=VMEM_LIMIT),
        name="gla_decode",
    )(q_s, k_s, a_s, v_s, state_gla[0])

    x1_s = pl.pallas_call(
        _even_decode_out_kernel,
        out_shape=jax.ShapeDtypeStruct((dec_b, D_MODEL), F32),
        compiler_params=single,
        name="even_decode_out",
    )(xs, o_s, sg_s, ysc_s, gng, wout_a)

    ob = DECODE_ODD_BLOCK
    assert dec_b % ob == 0
    rows_spec = pl.BlockSpec((ob, D_MODEL), lambda i: (i, 0))
    hist_spec = pl.BlockSpec((CCONV_K - 1, ob, CONV_W), lambda i: (0, i, 0))
    cbuf = jnp.transpose(state_cconv[0], (1, 0, 2))
    y_s, cconv_t = pl.pallas_call(
        _odd_decode_kernel,
        grid=(dec_b // ob,),
        in_specs=[rows_spec] + odd_weight_specs + [hist_spec],
        out_specs=[rows_spec, hist_spec],
        out_shape=[jax.ShapeDtypeStruct((dec_b, D_MODEL), F32),
                   jax.ShapeDtypeStruct((CCONV_K - 1, dec_b, CONV_W), F32)],
        compiler_params=pltpu.CompilerParams(dimension_semantics=("arbitrary",),
                                             vmem_limit_bytes=VMEM_LIMIT),
        name="odd_decode",
    )(x1_s, *odd_weights, cbuf)
    cconv_s = jnp.transpose(cconv_t, (1, 0, 2))[None]

    return (y_p,
            y_s.reshape(dec_b, 1, D_MODEL),
            gla_p,
            sconv_p,
            cconv_p,
            gla_s.reshape(1, dec_b, HEADS, DK, DV),
            sconv_s.reshape(1, dec_b, SCONV_K - 1, CONV_W),
            cconv_s)
```

```python
import jax
import jax.numpy as jnp
import numpy as np
from jax import lax
from jax.experimental import pallas as pl
from jax.experimental.pallas import tpu as pltpu

F32 = jnp.float32
BF16 = jnp.bfloat16

D_MODEL = 1024
HEADS = 4
DK = 128
DV = 256
QK_WIDTH = HEADS * DK
V_WIDTH = HEADS * DV
GATE_RANK = 16
GATE_RANK_PAD = 128
GATE_TEMP_INV = 1.0 / 16.0
CHUNK = 64
CHUNK_SHIFT = 6
SCONV_K = 3
CCONV_K = 31
CONV_W = 1024
RMS_EPS = 1e-6
LN_EPS = 1e-5
Q_SCALE = DK ** -0.5

COL_Q = 0
COL_K = COL_Q + QK_WIDTH
COL_V = COL_K + QK_WIDTH
COL_G = COL_V + V_WIDTH
COL_A_LOW = COL_G + V_WIDTH
COL_HB = COL_A_LOW + GATE_RANK
COL_GATE_B = COL_HB + CONV_W
COL_GATE_C = COL_GATE_B + CONV_W
COL_ZB = COL_GATE_C + CONV_W
MAIN_W = COL_ZB + CONV_W

MXU_K = 256
MXU_N = 256
EVEN_TILE = 512
ODD_TILE = 1024
CCONV_HALO = 32
SCONV_HALO = 8
DFT_N = MXU_K
DFT_HALF = DFT_N // 2
DFT_HOP = DFT_N - CCONV_HALO
RESP_K_PAD = 128
DECODE_STATE_BLOCK = 8
DECODE_STATE_BUFFERS = 4
DECODE_ODD_BLOCK = 32
DECODE_HIST_BUFFERS = 3
VMEM_LIMIT = 60 * 1024 * 1024


def _dot(a, b):
    return jnp.dot(a, b, preferred_element_type=F32)


def _dot_nt(a, b):
    return lax.dot_general(a, b, (((1,), (1,)), ((), ())), preferred_element_type=F32)


def _dot_tn(a, b):
    return lax.dot_general(a, b, (((0,), (0,)), ((), ())), preferred_element_type=F32)


def _proj(h, wt_ref, lo, hi):
    return _dot_nt(h, wt_ref[lo:hi, :])


def _rmsnorm(x, g):
    ms = jnp.mean(x * x, axis=-1, keepdims=True)
    return x * lax.rsqrt(ms + RMS_EPS) * g


def _gate(x, y):
    return x / (1.0 + jnp.exp(-y))


def _silu(x):
    return _gate(x, x)


def _log_sigmoid(x):
    return -(jnp.maximum(-x, 0.0) + jnp.log(1.0 + jnp.exp(-jnp.abs(x))))


def _log_decay(h, wmain_ref, wup_ref, bup_ref):
    a_low = _proj(h, wmain_ref, COL_A_LOW, COL_A_LOW + GATE_RANK_PAD).astype(BF16)
    logit = _dot(a_low, wup_ref[...]) + bup_ref[...]
    return _log_sigmoid(logit) * GATE_TEMP_INV


def _head_rmsnorm(o, gng):
    ms = jnp.mean(o * o, axis=-1, keepdims=True)
    return o * lax.rsqrt(ms + RMS_EPS) * gng


def _layernorm_act(yc, z, lng, lnb):
    mu = jnp.mean(yc, axis=-1, keepdims=True)
    xc = yc - mu
    var = jnp.mean(xc * xc, axis=-1, keepdims=True)
    yn = xc * lax.rsqrt(var + LN_EPS) * lng + lnb
    return (_silu(yn) * _silu(z)).astype(BF16)


def _short_conv_gate(u, prev1, prev2, gate_b, z_b, wsc_ref):
    y = wsc_ref[2:3, :] * u + wsc_ref[1:2, :] * prev1 + wsc_ref[0:1, :] * prev2
    return gate_b * y * _silu(z_b)


def _even_prompt_kernel(x_ref, ng_ref, wmain_ref, wup_ref, bup_ref, gng_ref, wsc_ref, wout_ref,
                        x1_ref, sgla_ref, sconv_ref,
                        st_ref, ubuf_ref, qe_ref, ke_ref, kd_ref, v_ref, dec_ref, mix_ref, ysc_ref):
    tm = EVEN_TILE
    t = pl.program_id(1)
    last_t = pl.num_programs(1) - 1

    @pl.when(t == 0)
    def _():
        st_ref[...] = jnp.zeros_like(st_ref)
        ubuf_ref[0:SCONV_HALO, :] = jnp.zeros((SCONV_HALO, CONV_W), F32)

    x = x_ref[...]
    h = _rmsnorm(x, ng_ref[...]).astype(BF16)

    def short_conv_group(g0):
        cols = slice(g0, g0 + MXU_N)
        part = lambda c0: _proj(h, wmain_ref, c0 + g0, c0 + g0 + MXU_N)
        u = part(COL_GATE_C) * part(COL_HB)
        ubuf_ref[SCONV_HALO:SCONV_HALO + tm, cols] = u
        y = (wsc_ref[2:3, cols] * u + wsc_ref[1:2, cols] * ubuf_ref[pl.ds(SCONV_HALO - 1, tm), cols]
             + wsc_ref[0:1, cols] * ubuf_ref[pl.ds(SCONV_HALO - 2, tm), cols])
        ysc_ref[:, cols] = (part(COL_GATE_B) * y * _silu(part(COL_ZB))).astype(BF16)

    q = _proj(h, wmain_ref, COL_Q, COL_K) * Q_SCALE
    k = _proj(h, wmain_ref, COL_K, COL_V)
    v_ref[...] = _proj(h, wmain_ref, COL_V, COL_G).astype(BF16)
    a_low = _proj(h, wmain_ref, COL_A_LOW, COL_A_LOW + GATE_RANK_PAD).astype(BF16)
    short_conv_group(0)
    log_a = _log_sigmoid(_dot(a_low, wup_ref[...]) + bup_ref[...]) * GATE_TEMP_INV
    short_conv_group(MXU_N)

    row = lax.broadcasted_iota(jnp.int32, (MXU_K, MXU_K), 0)
    col = lax.broadcasted_iota(jnp.int32, (MXU_K, MXU_K), 1)
    in_chunk_causal = ((row >> CHUNK_SHIFT) == (col >> CHUNK_SHIFT)) & (col <= row)
    tri = jnp.where(in_chunk_causal, 1.0, 0.0).astype(BF16)
    la_hi = log_a.astype(BF16)
    la_lo = (log_a - la_hi.astype(F32)).astype(BF16)
    for sb in range(tm // MXU_K):
        rows = slice(sb * MXU_K, (sb + 1) * MXU_K)
        b_cum = _dot(tri, la_hi[rows, :]) + _dot(tri, la_lo[rows, :])
        b_tot = jnp.concatenate(
            [jnp.broadcast_to(b_cum[(c + 1) * CHUNK - 1:(c + 1) * CHUNK, :], (CHUNK, QK_WIDTH))
             for c in range(MXU_K // CHUNK)], axis=0)
        qe_ref[rows, :] = (q[rows, :] * jnp.exp(b_cum)).astype(BF16)
        ke_ref[rows, :] = (k[rows, :] * jnp.exp(-b_cum)).astype(BF16)
        kd_ref[rows, :] = (k[rows, :] * jnp.exp(b_tot - b_cum)).astype(BF16)
        dec_ref[rows, :] = jnp.exp(b_tot)
        if sb < 2:
            short_conv_group((2 + sb) * MXU_N)

    gng = gng_ref[...]
    for hh in range(HEADS):
        kcols = slice(hh * DK, (hh + 1) * DK)
        vcols = slice(hh * DV, (hh + 1) * DV)
        st = st_ref[hh]
        for sb in range(tm // MXU_K):
            rows = slice(sb * MXU_K, (sb + 1) * MXU_K)
            sc = jnp.where(in_chunk_causal, _dot_nt(qe_ref[rows, kcols], ke_ref[rows, kcols]), 0.0)
            o_intra = _dot(sc.astype(BF16), v_ref[rows, vcols])
            for c in range(MXU_K // CHUNK):
                r0 = sb * MXU_K + c * CHUNK
                crow = slice(r0, r0 + CHUNK)
                o = o_intra[c * CHUNK:(c + 1) * CHUNK, :] + _dot_nt(qe_ref[crow, kcols], st.astype(BF16))
                mix_ref[crow, vcols] = _head_rmsnorm(o, gng)
                dec = dec_ref[r0:r0 + 1, kcols]
                st = st * dec + _dot_tn(v_ref[crow, vcols], kd_ref[crow, kcols])
        st_ref[hh] = st

    o_mix = (mix_ref[...] * _silu(_proj(h, wmain_ref, COL_G, COL_A_LOW))).astype(BF16)

    ubuf_ref[0:SCONV_HALO, :] = ubuf_ref[tm:tm + SCONV_HALO, :]

    out = _dot(o_mix, wout_ref[0:V_WIDTH, :]) + _dot(ysc_ref[...], wout_ref[V_WIDTH:V_WIDTH + CONV_W, :])
    x1_ref[...] = x + out

    @pl.when(t == last_t)
    def _():
        for hh in range(HEADS):
            sgla_ref[hh] = st_ref[hh].T
        sconv_ref[...] = ubuf_ref[pl.ds(SCONV_HALO + tm - (SCONV_K - 1), SCONV_K - 1), :]


def _odd_prompt_kernel(x_ref, ng_ref, win_ref, bin_ref, bdw_ref, lng_ref, lnb_ref, wout_ref,
                       bout_ref, fng_ref, fwd_ref, inv_ref, basis_ref, taps_ref,
                       y_ref, cconv_ref,
                       ub_ref, yc_ref, tail_ref, z_ref, resp_ref):
    tm = ODD_TILE
    t = pl.program_id(1)
    last_t = pl.num_programs(1) - 1

    @pl.when((pl.program_id(0) == 0) & (t == 0))
    def _():
        resp_ref[...] = jnp.dot(basis_ref[...], taps_ref[...], precision=lax.Precision.HIGHEST,
                                preferred_element_type=F32)

    @pl.when(t == 0)
    def _():
        ub_ref[0:CCONV_HALO, :] = jnp.zeros((CCONV_HALO, CONV_W), BF16)

    x = x_ref[...]
    h = _rmsnorm(x, ng_ref[...]).astype(BF16)
    for g0 in range(0, CONV_W, MXU_N):
        cols = slice(g0, g0 + MXU_N)
        gcols = slice(CONV_W + g0, CONV_W + g0 + MXU_N)
        u = _gate(_dot(h, win_ref[:, cols]) + bin_ref[:, cols], _dot(h, win_ref[:, gcols]) + bin_ref[:, gcols])
        tail_ref[:, cols] = u[tm - CCONV_HALO:, :]
        ub_ref[CCONV_HALO:CCONV_HALO + tm, cols] = u.astype(BF16)

    @pl.when(t == last_t)
    def _():
        cconv_ref[...] = tail_ref[CCONV_HALO - (CCONV_K - 1):, :]

    bdw = bdw_ref[...]
    ha = resp_ref[0:DFT_HALF, :]
    hb = resp_ref[DFT_HALF:2 * DFT_HALF, :]
    ha2 = resp_ref[2 * DFT_HALF:3 * DFT_HALF, :]
    fwd = fwd_ref[...].astype(BF16)
    inv = inv_ref[...].astype(BF16)
    starts = sorted({min(s0, tm - DFT_HOP) for s0 in range(0, tm, DFT_HOP)})
    z_groups = list(range(0, CONV_W, MXU_N))
    for i, start in enumerate(starts):
        spec = _dot(fwd, ub_ref[start:start + DFT_N, :])
        p = spec[0:DFT_HALF, :]
        q = spec[DFT_HALF:DFT_N, :]
        prod = jnp.concatenate([p * ha - q * hb, p * hb + q * ha2], axis=0).astype(BF16)
        yc_ref[start:start + DFT_HOP, :] = _dot(inv, prod) + bdw
        if i < len(z_groups):
            zc = slice(2 * CONV_W + z_groups[i], 2 * CONV_W + z_groups[i] + MXU_N)
            z_ref[:, z_groups[i]:z_groups[i] + MXU_N] = _dot(h, win_ref[:, zc]) + bin_ref[:, zc]
    assert len(starts) >= len(z_groups)

    ub_ref[0:CCONV_HALO, :] = ub_ref[tm:tm + CCONV_HALO, :]

    act = _layernorm_act(yc_ref[...], z_ref[...], lng_ref[...], lnb_ref[...])
    y_ref[...] = _rmsnorm(x + _dot(act, wout_ref[...]) + bout_ref[...], fng_ref[...])


def _even_decode_front_kernel(x_ref, ng_ref, wmain_ref, wup_ref, bup_ref, wsc_ref, sbuf_ref,
                              q_ref, k_ref, a_ref, v_ref, sg_ref, y_ref, snew_ref):
    h = _rmsnorm(x_ref[...], ng_ref[...]).astype(BF16)
    q_ref[...] = _proj(h, wmain_ref, COL_Q, COL_K) * Q_SCALE
    k_ref[...] = _proj(h, wmain_ref, COL_K, COL_V)
    v_ref[...] = _proj(h, wmain_ref, COL_V, COL_G)
    sg_ref[...] = _silu(_proj(h, wmain_ref, COL_G, COL_A_LOW))
    a_ref[...] = jnp.exp(_log_decay(h, wmain_ref, wup_ref, bup_ref))
    u = _proj(h, wmain_ref, COL_GATE_C, COL_ZB) * _proj(h, wmain_ref, COL_HB, COL_GATE_B)
    prev2 = sbuf_ref[:, 0:CONV_W]
    prev1 = sbuf_ref[:, CONV_W:2 * CONV_W]
    y_ref[...] = _short_conv_gate(u, prev1, prev2, _proj(h, wmain_ref, COL_GATE_B, COL_GATE_C),
                                  _proj(h, wmain_ref, COL_ZB, MAIN_W), wsc_ref)
    snew_ref[:, 0:CONV_W] = prev1
    snew_ref[:, CONV_W:2 * CONV_W] = u


def _lane_bcast_column(row):
    return jnp.broadcast_to(row, (DK, DK)).T


def _gla_decode_kernel(q_ref, k_ref, a_ref, v_ref, s_ref, snew_ref, o_ref):
    for b in range(DECODE_STATE_BLOCK):
        for hh in range(HEADS):
            kcols = slice(hh * DK, (hh + 1) * DK)
            vcols = slice(hh * DV, (hh + 1) * DV)
            a_col = _lane_bcast_column(a_ref[b:b + 1, kcols])
            k_col = _lane_bcast_column(k_ref[b:b + 1, kcols])
            q_col = _lane_bcast_column(q_ref[b:b + 1, kcols])
            v_row = v_ref[b:b + 1, vcols]
            halves = []
            for half in range(DV // DK):
                lanes = slice(half * DK, (half + 1) * DK)
                s_new = a_col * s_ref[b, hh, :, lanes] + k_col * v_row[:, lanes]
                snew_ref[b, hh, :, lanes] = s_new
                halves.append(jnp.sum(q_col * s_new, axis=0, keepdims=True))
            o_ref[b:b + 1, vcols] = jnp.concatenate(halves, axis=1)


def _even_decode_out_kernel(x_ref, o_ref, sg_ref, y_ref, gng_ref, wout_ref, x1_ref):
    gng = gng_ref[...]
    parts = [_head_rmsnorm(o_ref[:, hh * DV:(hh + 1) * DV], gng) for hh in range(HEADS)]
    o_mix = (jnp.concatenate(parts, axis=1) * sg_ref[...]).astype(BF16)
    out = _dot(o_mix, wout_ref[0:V_WIDTH, :]) + _dot(y_ref[...].astype(BF16), wout_ref[V_WIDTH:V_WIDTH + CONV_W, :])
    x1_ref[...] = x_ref[...] + out


def _odd_decode_kernel(x_ref, ng_ref, win_ref, bin_ref, wdw_ref, bdw_ref, lng_ref, lnb_ref, wout_ref,
                       bout_ref, fng_ref, cbuf_ref,
                       y_ref, cnew_ref):
    n_hist = CCONV_K - 1
    x = x_ref[...]
    h = _rmsnorm(x, ng_ref[...]).astype(BF16)
    a = _dot(h, win_ref[:, 0:CONV_W]) + bin_ref[:, 0:CONV_W]
    a_gate = _dot(h, win_ref[:, CONV_W:2 * CONV_W]) + bin_ref[:, CONV_W:2 * CONV_W]
    u = _gate(a, a_gate)
    yc = bdw_ref[...] + wdw_ref[n_hist:CCONV_K, :] * u
    for j in range(n_hist):
        tap = cbuf_ref[j]
        yc = yc + wdw_ref[j:j + 1, :] * tap
        if j >= 1:
            cnew_ref[j - 1] = tap
    cnew_ref[n_hist - 1] = u
    z = _dot(h, win_ref[:, 2 * CONV_W:3 * CONV_W]) + bin_ref[:, 2 * CONV_W:3 * CONV_W]
    act = _layernorm_act(yc, z, lng_ref[...], lnb_ref[...])
    y_ref[...] = _rmsnorm(x + _dot(act, wout_ref[...]) + bout_ref[...], fng_ref[...])


def _const_spec(shape):
    nd = len(shape)
    return pl.BlockSpec(shape, lambda *_: (0,) * nd, pipeline_mode=pl.Buffered(1))


def _row(v):
    return v.reshape(1, -1)


def kernel(x_prompt, x_sample, state_gla, state_sconv, state_cconv, norm_g, w_in_a, w_gate_up, b_gate_up, gla_norm_g, w_sconv, w_out_a, w_in_c, b_in_c, w_dwconv, b_dwconv, ln_g, ln_b, w_out_c, b_out_c, final_norm_g):
    bsz, seq, d = x_prompt.shape
    dec_b = x_sample.shape[0]
    assert d == D_MODEL and seq % EVEN_TILE == 0 and seq % ODD_TILE == 0 and x_sample.shape[1] == 1
    assert EVEN_TILE // MXU_K >= 2 and CONV_W == 4 * MXU_N and ODD_TILE >= DFT_HOP
    assert w_in_a.shape[0] == 1 and w_in_c.shape[0] == 1 and norm_g.shape[0] == 2

    assert w_in_a.shape[2] == MAIN_W
    wmain = w_in_a[0].T.astype(BF16)
    wup = jnp.pad(w_gate_up[0], ((0, GATE_RANK_PAD - GATE_RANK), (0, 0))).astype(BF16)
    bup = _row(b_gate_up[0])
    gng = _row(gla_norm_g[0])
    wsc = w_sconv[0]
    wout_a = w_out_a[0].astype(BF16)
    ng0 = _row(norm_g[0])
    ng1 = _row(norm_g[1])
    win_c = w_in_c[0].astype(BF16)
    bin_c = _row(b_in_c[0])
    wdw = w_dwconv[0]
    bdw = _row(b_dwconv[0])
    lng = _row(ln_g[0])
    lnb = _row(ln_b[0])
    wout_c = w_out_c[0].astype(BF16)
    bout = _row(b_out_c[0])
    fng = _row(final_norm_g)

    even_weights = (ng0, wmain, wup, bup)
    even_prompt_weights = even_weights + (gng, wsc, wout_a)
    odd_weights = (ng1, win_c, bin_c, wdw, bdw, lng, lnb, wout_c, bout, fng)
    odd_weight_specs = [_const_spec(w.shape) for w in odd_weights]
    hf = DFT_HALF
    assert CCONV_HALO >= CCONV_K - 1 and DFT_HOP % 16 == 0
    kk = np.arange(hf)[:, None]

    def packed_basis(pos):
        ang = 2.0 * np.pi * kk * pos[None, :] / DFT_N
        lower = -np.sin(ang)
        lower[0] = np.cos(np.pi * pos)
        return np.cos(ang), lower

    fc, fs = packed_basis(np.arange(DFT_N, dtype=np.float64))
    fwd = np.concatenate([fc, fs], axis=0).astype(np.float32)
    ic, isn = packed_basis(np.arange(CCONV_HALO, DFT_N, dtype=np.float64))
    weight = np.full((hf, 1), 2.0 / DFT_N)
    weight[0] = 1.0 / DFT_N
    inv = np.concatenate([weight * ic, weight * isn], axis=0).T.astype(np.float32)
    hc, hs = packed_basis(np.arange(CCONV_K, dtype=np.float64))
    hs_imag = hs.copy()
    hs_imag[0] = 0.0
    hc_alt = hc.copy()
    hc_alt[0] = hs[0]
    basis = np.zeros((3 * hf, RESP_K_PAD), np.float32)
    basis[:, :CCONV_K] = np.concatenate([hc, hs_imag, hc_alt], axis=0)
    taps_by_lag = jnp.pad(wdw[::-1], ((0, RESP_K_PAD - CCONV_K), (0, 0)))
    odd_prompt_weights = (ng1, win_c, bin_c, bdw, lng, lnb, wout_c, bout, fng,
                          jnp.asarray(fwd), jnp.asarray(inv), jnp.asarray(basis), taps_by_lag)

    prompt_params = pltpu.CompilerParams(dimension_semantics=("arbitrary", "arbitrary"),
                                         vmem_limit_bytes=VMEM_LIMIT)

    tm = EVEN_TILE
    tile_spec = pl.BlockSpec((None, tm, D_MODEL), lambda b, t: (b, t, 0))
    x1_p, gla_p, sconv_p = pl.pallas_call(
        _even_prompt_kernel,
        grid=(bsz, seq // tm),
        in_specs=[tile_spec] + [_const_spec(w.shape) for w in even_prompt_weights],
        out_specs=[tile_spec,
                   pl.BlockSpec((None, None, HEADS, DK, DV), lambda b, t: (0, b, 0, 0, 0)),
                   pl.BlockSpec((None, None, SCONV_K - 1, CONV_W), lambda b, t: (0, b, 0, 0))],
        out_shape=[jax.ShapeDtypeStruct((bsz, seq, D_MODEL), F32),
                   jax.ShapeDtypeStruct((1, bsz, HEADS, DK, DV), F32),
                   jax.ShapeDtypeStruct((1, bsz, SCONV_K - 1, CONV_W), F32)],
        scratch_shapes=[pltpu.VMEM((HEADS, DV, DK), F32),
                        pltpu.VMEM((tm + SCONV_HALO, CONV_W), F32),
                        pltpu.VMEM((tm, QK_WIDTH), BF16),
                        pltpu.VMEM((tm, QK_WIDTH), BF16),
                        pltpu.VMEM((tm, QK_WIDTH), BF16),
                        pltpu.VMEM((tm, V_WIDTH), BF16),
                        pltpu.VMEM((tm, QK_WIDTH), F32),
                        pltpu.VMEM((tm, V_WIDTH), F32),
                        pltpu.VMEM((tm, CONV_W), BF16)],
        compiler_params=prompt_params,
        name="even_prompt",
    )(x_prompt, *even_prompt_weights)

    tm = ODD_TILE
    tile_spec = pl.BlockSpec((None, tm, D_MODEL), lambda b, t: (b, t, 0))
    y_p, cconv_p = pl.pallas_call(
        _odd_prompt_kernel,
        grid=(bsz, seq // tm),
        in_specs=[tile_spec] + [_const_spec(w.shape) for w in odd_prompt_weights],
        out_specs=[tile_spec,
                   pl.BlockSpec((None, None, CCONV_K - 1, CONV_W), lambda b, t: (0, b, 0, 0))],
        out_shape=[jax.ShapeDtypeStruct((bsz, seq, D_MODEL), F32),
                   jax.ShapeDtypeStruct((1, bsz, CCONV_K - 1, CONV_W), F32)],
        scratch_shapes=[pltpu.VMEM((tm + CCONV_HALO, CONV_W), BF16),
                        pltpu.VMEM((tm, CONV_W), F32),
                        pltpu.VMEM((CCONV_HALO, CONV_W), F32),
                        pltpu.VMEM((tm, CONV_W), F32),
                        pltpu.VMEM((3 * DFT_HALF, CONV_W), F32)],
        compiler_params=prompt_params,
        name="odd_prompt",
    )(x1_p, *odd_prompt_weights)

    xs = x_sample.reshape(dec_b, D_MODEL)
    sbuf = state_sconv.reshape(dec_b, (SCONV_K - 1) * CONV_W)
    single = pltpu.CompilerParams(vmem_limit_bytes=VMEM_LIMIT)
    q_s, k_s, a_s, v_s, sg_s, ysc_s, sconv_s = pl.pallas_call(
        _even_decode_front_kernel,
        out_shape=[jax.ShapeDtypeStruct((dec_b, QK_WIDTH), F32),
                   jax.ShapeDtypeStruct((dec_b, QK_WIDTH), F32),
                   jax.ShapeDtypeStruct((dec_b, QK_WIDTH), F32),
                   jax.ShapeDtypeStruct((dec_b, V_WIDTH), F32),
                   jax.ShapeDtypeStruct((dec_b, V_WIDTH), F32),
                   jax.ShapeDtypeStruct((dec_b, CONV_W), F32),
                   jax.ShapeDtypeStruct((dec_b, (SCONV_K - 1) * CONV_W), F32)],
        compiler_params=single,
        name="even_decode_front",
    )(xs, *even_weights, wsc, sbuf)

    sb = DECODE_STATE_BLOCK
    assert dec_b % sb == 0
    vec_spec = lambda w: pl.BlockSpec((sb, w), lambda i: (i, 0))
    state_spec = pl.BlockSpec((sb, HEADS, DK, DV), lambda i: (i, 0, 0, 0))
    state_in_spec = pl.BlockSpec((sb, HEADS, DK, DV), lambda i: (i, 0, 0, 0),
                                 pipeline_mode=pl.Buffered(DECODE_STATE_BUFFERS))
    def gla_decode_streamed(*hbm_refs):
        pltpu.emit_pipeline(
            _gla_decode_kernel,
            grid=(dec_b // sb,),
            in_specs=[vec_spec(QK_WIDTH), vec_spec(QK_WIDTH), vec_spec(QK_WIDTH), vec_spec(V_WIDTH),
                      state_in_spec],
            out_specs=[state_spec, vec_spec(V_WIDTH)],
        )(*hbm_refs)

    any_spec = pl.BlockSpec(memory_space=pl.ANY)
    gla_s, o_s = pl.pallas_call(
        gla_decode_streamed,
        in_specs=[any_spec] * 5,
        out_specs=[any_spec] * 2,
        out_shape=[jax.ShapeDtypeStruct((dec_b, HEADS, DK, DV), F32),
                   jax.ShapeDtypeStruct((dec_b, V_WIDTH), F32)],
        compiler_params=pltpu.CompilerParams(vmem_limit_bytes=VMEM_LIMIT),
        name="gla_decode",
    )(q_s, k_s, a_s, v_s, state_gla[0])

    x1_s = pl.pallas_call(
        _even_decode_out_kernel,
        out_shape=jax.ShapeDtypeStruct((dec_b, D_MODEL), F32),
        compiler_params=single,
        name="even_decode_out",
    )(xs, o_s, sg_s, ysc_s, gng, wout_a)

    ob = DECODE_ODD_BLOCK
    assert dec_b % ob == 0
    rows_spec = pl.BlockSpec((ob, D_MODEL), lambda i: (i, 0))
    hist_spec = pl.BlockSpec((CCONV_K - 1, ob, CONV_W), lambda i: (0, i, 0))
    cbuf = jnp.transpose(state_cconv[0], (1, 0, 2))
    hist_in_spec = pl.BlockSpec((CCONV_K - 1, ob, CONV_W), lambda i: (0, i, 0),
                                pipeline_mode=pl.Buffered(DECODE_HIST_BUFFERS))

    def odd_decode_streamed(*hbm_refs):
        pltpu.emit_pipeline(
            _odd_decode_kernel,
            grid=(dec_b // ob,),
            in_specs=[rows_spec] + odd_weight_specs + [hist_in_spec],
            out_specs=[rows_spec, hist_spec],
        )(*hbm_refs)

    y_s, cconv_t = pl.pallas_call(
        odd_decode_streamed,
        in_specs=[any_spec] * (2 + len(odd_weights)),
        out_specs=[any_spec] * 2,
        out_shape=[jax.ShapeDtypeStruct((dec_b, D_MODEL), F32),
                   jax.ShapeDtypeStruct((CCONV_K - 1, dec_b, CONV_W), F32)],
        compiler_params=pltpu.CompilerParams(vmem_limit_bytes=VMEM_LIMIT),
        name="odd_decode",
    )(x1_s, *odd_weights, cbuf)
    cconv_s = jnp.transpose(cconv_t, (1, 0, 2))[None]

    return (y_p,
            y_s.reshape(dec_b, 1, D_MODEL),
            gla_p,
            sconv_p,
            cconv_p,
            gla_s.reshape(1, dec_b, HEADS, DK, DV),
            sconv_s.reshape(1, dec_b, SCONV_K - 1, CONV_W),
            cconv_s)
```

```python
import jax
import jax.numpy as jnp
import numpy as np
from jax import lax
from jax.experimental import pallas as pl
from jax.experimental.pallas import tpu as pltpu

F32 = jnp.float32
BF16 = jnp.bfloat16

D_MODEL = 1024
HEADS = 4
DK = 128
DV = 256
QK_WIDTH = HEADS * DK
V_WIDTH = HEADS * DV
GATE_RANK = 16
GATE_RANK_PAD = 128
GATE_TEMP_INV = 1.0 / 16.0
CHUNK = 64
CHUNK_SHIFT = 6
SCONV_K = 3
CCONV_K = 31
CONV_W = 1024
RMS_EPS = 1e-6
LN_EPS = 1e-5
Q_SCALE = DK ** -0.5

COL_Q = 0
COL_K = COL_Q + QK_WIDTH
COL_V = COL_K + QK_WIDTH
COL_G = COL_V + V_WIDTH
COL_A_LOW = COL_G + V_WIDTH
COL_HB = COL_A_LOW + GATE_RANK
COL_GATE_B = COL_HB + CONV_W
COL_GATE_C = COL_GATE_B + CONV_W
COL_ZB = COL_GATE_C + CONV_W
MAIN_W = COL_ZB + CONV_W

MXU_K = 256
MXU_N = 256
EVEN_TILE = 512
ODD_TILE = 1024
CCONV_HALO = 32
SCONV_HALO = 8
DFT_N = MXU_K
DFT_HALF = DFT_N // 2
DFT_HOP = DFT_N - CCONV_HALO
RESP_K_PAD = 128
DECODE_STATE_BLOCK = 16
DECODE_STATE_BUFFERS = 3
DECODE_ODD_BLOCK = 64
VMEM_LIMIT = 60 * 1024 * 1024


def _dot(a, b):
    return jnp.dot(a, b, preferred_element_type=F32)


def _dot_nt(a, b):
    return lax.dot_general(a, b, (((1,), (1,)), ((), ())), preferred_element_type=F32)


def _dot_tn(a, b):
    return lax.dot_general(a, b, (((0,), (0,)), ((), ())), preferred_element_type=F32)


def _proj(h, wt_ref, lo, hi):
    return _dot_nt(h, wt_ref[lo:hi, :])


def _rmsnorm(x, g):
    ms = jnp.mean(x * x, axis=-1, keepdims=True)
    return x * lax.rsqrt(ms + RMS_EPS) * g


def _gate(x, y):
    return x / (1.0 + jnp.exp(-y))


def _silu(x):
    return _gate(x, x)


def _log_sigmoid(x):
    return -(jnp.maximum(-x, 0.0) + jnp.log(1.0 + jnp.exp(-jnp.abs(x))))


def _log_decay(h, wmain_ref, wup_ref, bup_ref):
    a_low = _proj(h, wmain_ref, COL_A_LOW, COL_A_LOW + GATE_RANK_PAD).astype(BF16)
    logit = _dot(a_low, wup_ref[...]) + bup_ref[...]
    return _log_sigmoid(logit) * GATE_TEMP_INV


def _head_rmsnorm(o, gng):
    ms = jnp.mean(o * o, axis=-1, keepdims=True)
    return o * lax.rsqrt(ms + RMS_EPS) * gng


def _layernorm_act(yc, z, lng, lnb):
    mu = jnp.mean(yc, axis=-1, keepdims=True)
    xc = yc - mu
    var = jnp.mean(xc * xc, axis=-1, keepdims=True)
    yn = xc * lax.rsqrt(var + LN_EPS) * lng + lnb
    return (_silu(yn) * _silu(z)).astype(BF16)


def _short_conv_gate(u, prev1, prev2, gate_b, z_b, wsc_ref):
    y = wsc_ref[2:3, :] * u + wsc_ref[1:2, :] * prev1 + wsc_ref[0:1, :] * prev2
    return gate_b * y * _silu(z_b)


def _even_prompt_kernel(x_ref, ng_ref, wmain_ref, wup_ref, bup_ref, gng_ref, wsc_ref, wout_ref,
                        x1_ref, sgla_ref, sconv_ref,
                        st_ref, ubuf_ref, qe_ref, ke_ref, kd_ref, v_ref, dec_ref, mix_ref, ysc_ref):
    tm = EVEN_TILE
    t = pl.program_id(1)
    last_t = pl.num_programs(1) - 1

    @pl.when(t == 0)
    def _():
        st_ref[...] = jnp.zeros_like(st_ref)
        ubuf_ref[0:SCONV_HALO, :] = jnp.zeros((SCONV_HALO, CONV_W), F32)

    x = x_ref[...]
    h = _rmsnorm(x, ng_ref[...]).astype(BF16)

    def short_conv_group(g0):
        cols = slice(g0, g0 + MXU_N)
        part = lambda c0: _proj(h, wmain_ref, c0 + g0, c0 + g0 + MXU_N)
        u = part(COL_GATE_C) * part(COL_HB)
        ubuf_ref[SCONV_HALO:SCONV_HALO + tm, cols] = u
        y = (wsc_ref[2:3, cols] * u + wsc_ref[1:2, cols] * ubuf_ref[pl.ds(SCONV_HALO - 1, tm), cols]
             + wsc_ref[0:1, cols] * ubuf_ref[pl.ds(SCONV_HALO - 2, tm), cols])
        ysc_ref[:, cols] = (part(COL_GATE_B) * y * _silu(part(COL_ZB))).astype(BF16)

    q = _proj(h, wmain_ref, COL_Q, COL_K) * Q_SCALE
    k = _proj(h, wmain_ref, COL_K, COL_V)
    v_ref[...] = _proj(h, wmain_ref, COL_V, COL_G).astype(BF16)
    a_low = _proj(h, wmain_ref, COL_A_LOW, COL_A_LOW + GATE_RANK_PAD).astype(BF16)
    short_conv_group(0)
    log_a = _log_sigmoid(_dot(a_low, wup_ref[...]) + bup_ref[...]) * GATE_TEMP_INV
    short_conv_group(MXU_N)

    row = lax.broadcasted_iota(jnp.int32, (MXU_K, MXU_K), 0)
    col = lax.broadcasted_iota(jnp.int32, (MXU_K, MXU_K), 1)
    in_chunk_causal = ((row >> CHUNK_SHIFT) == (col >> CHUNK_SHIFT)) & (col <= row)
    tri = jnp.where(in_chunk_causal, 1.0, 0.0).astype(BF16)
    la_hi = log_a.astype(BF16)
    la_lo = (log_a - la_hi.astype(F32)).astype(BF16)
    for sb in range(tm // MXU_K):
        rows = slice(sb * MXU_K, (sb + 1) * MXU_K)
        b_cum = _dot(tri, la_hi[rows, :]) + _dot(tri, la_lo[rows, :])
        b_tot = jnp.concatenate(
            [jnp.broadcast_to(b_cum[(c + 1) * CHUNK - 1:(c + 1) * CHUNK, :], (CHUNK, QK_WIDTH))
             for c in range(MXU_K // CHUNK)], axis=0)
        qe_ref[rows, :] = (q[rows, :] * jnp.exp(b_cum)).astype(BF16)
        ke_ref[rows, :] = (k[rows, :] * jnp.exp(-b_cum)).astype(BF16)
        kd_ref[rows, :] = (k[rows, :] * jnp.exp(b_tot - b_cum)).astype(BF16)
        dec_ref[rows, :] = jnp.exp(b_tot)
        if sb < 2:
            short_conv_group((2 + sb) * MXU_N)

    gng = gng_ref[...]
    for hh in range(HEADS):
        kcols = slice(hh * DK, (hh + 1) * DK)
        vcols = slice(hh * DV, (hh + 1) * DV)
        st = st_ref[hh]
        for sb in range(tm // MXU_K):
            rows = slice(sb * MXU_K, (sb + 1) * MXU_K)
            sc = jnp.where(in_chunk_causal, _dot_nt(qe_ref[rows, kcols], ke_ref[rows, kcols]), 0.0)
            o_intra = _dot(sc.astype(BF16), v_ref[rows, vcols])
            for c in range(MXU_K // CHUNK):
                r0 = sb * MXU_K + c * CHUNK
                crow = slice(r0, r0 + CHUNK)
                o = o_intra[c * CHUNK:(c + 1) * CHUNK, :] + _dot_nt(qe_ref[crow, kcols], st.astype(BF16))
                mix_ref[crow, vcols] = _head_rmsnorm(o, gng)
                dec = dec_ref[r0:r0 + 1, kcols]
                st = st * dec + _dot_tn(v_ref[crow, vcols], kd_ref[crow, kcols])
        st_ref[hh] = st

    o_mix = (mix_ref[...] * _silu(_proj(h, wmain_ref, COL_G, COL_A_LOW))).astype(BF16)

    ubuf_ref[0:SCONV_HALO, :] = ubuf_ref[tm:tm + SCONV_HALO, :]

    out = _dot(o_mix, wout_ref[0:V_WIDTH, :]) + _dot(ysc_ref[...], wout_ref[V_WIDTH:V_WIDTH + CONV_W, :])
    x1_ref[...] = x + out

    @pl.when(t == last_t)
    def _():
        for hh in range(HEADS):
            sgla_ref[hh] = st_ref[hh].T
        sconv_ref[...] = ubuf_ref[pl.ds(SCONV_HALO + tm - (SCONV_K - 1), SCONV_K - 1), :]


def _odd_prompt_kernel(x_ref, ng_ref, win_ref, bin_ref, bdw_ref, lng_ref, lnb_ref, wout_ref,
                       bout_ref, fng_ref, fwd_ref, inv_ref, basis_ref, taps_ref,
                       y_ref, cconv_ref,
                       ub_ref, yc_ref, tail_ref, z_ref, resp_ref):
    tm = ODD_TILE
    t = pl.program_id(1)
    last_t = pl.num_programs(1) - 1

    @pl.when((pl.program_id(0) == 0) & (t == 0))
    def _():
        resp_ref[...] = jnp.dot(basis_ref[...], taps_ref[...], precision=lax.Precision.HIGHEST,
                                preferred_element_type=F32)

    @pl.when(t == 0)
    def _():
        ub_ref[0:CCONV_HALO, :] = jnp.zeros((CCONV_HALO, CONV_W), BF16)

    x = x_ref[...]
    h = _rmsnorm(x, ng_ref[...]).astype(BF16)
    for g0 in range(0, CONV_W, MXU_N):
        cols = slice(g0, g0 + MXU_N)
        gcols = slice(CONV_W + g0, CONV_W + g0 + MXU_N)
        u = _gate(_dot(h, win_ref[:, cols]) + bin_ref[:, cols], _dot(h, win_ref[:, gcols]) + bin_ref[:, gcols])
        tail_ref[:, cols] = u[tm - CCONV_HALO:, :]
        ub_ref[CCONV_HALO:CCONV_HALO + tm, cols] = u.astype(BF16)

    @pl.when(t == last_t)
    def _():
        cconv_ref[...] = tail_ref[CCONV_HALO - (CCONV_K - 1):, :]

    bdw = bdw_ref[...]
    ha = resp_ref[0:DFT_HALF, :]
    hb = resp_ref[DFT_HALF:2 * DFT_HALF, :]
    ha2 = resp_ref[2 * DFT_HALF:3 * DFT_HALF, :]
    fwd = fwd_ref[...].astype(BF16)
    inv = inv_ref[...].astype(BF16)
    starts = sorted({min(s0, tm - DFT_HOP) for s0 in range(0, tm, DFT_HOP)})
    z_groups = list(range(0, CONV_W, MXU_N))
    for i, start in enumerate(starts):
        spec = _dot(fwd, ub_ref[start:start + DFT_N, :])
        p = spec[0:DFT_HALF, :]
        q = spec[DFT_HALF:DFT_N, :]
        prod = jnp.concatenate([p * ha - q * hb, p * hb + q * ha2], axis=0).astype(BF16)
        yc_ref[start:start + DFT_HOP, :] = _dot(inv, prod) + bdw
        if i < len(z_groups):
            zc = slice(2 * CONV_W + z_groups[i], 2 * CONV_W + z_groups[i] + MXU_N)
            z_ref[:, z_groups[i]:z_groups[i] + MXU_N] = _dot(h, win_ref[:, zc]) + bin_ref[:, zc]
    assert len(starts) >= len(z_groups)

    ub_ref[0:CCONV_HALO, :] = ub_ref[tm:tm + CCONV_HALO, :]

    act = _layernorm_act(yc_ref[...], z_ref[...], lng_ref[...], lnb_ref[...])
    y_ref[...] = _rmsnorm(x + _dot(act, wout_ref[...]) + bout_ref[...], fng_ref[...])


def _even_decode_front_kernel(x_ref, ng_ref, wmain_ref, wup_ref, bup_ref, wsc_ref, sbuf_ref,
                              q_ref, k_ref, a_ref, v_ref, sg_ref, y_ref, snew_ref):
    h = _rmsnorm(x_ref[...], ng_ref[...]).astype(BF16)
    q_ref[...] = _proj(h, wmain_ref, COL_Q, COL_K) * Q_SCALE
    k_ref[...] = _proj(h, wmain_ref, COL_K, COL_V)
    v_ref[...] = _proj(h, wmain_ref, COL_V, COL_G)
    sg_ref[...] = _silu(_proj(h, wmain_ref, COL_G, COL_A_LOW))
    a_ref[...] = jnp.exp(_log_decay(h, wmain_ref, wup_ref, bup_ref))
    u = _proj(h, wmain_ref, COL_GATE_C, COL_ZB) * _proj(h, wmain_ref, COL_HB, COL_GATE_B)
    prev2 = sbuf_ref[:, 0:CONV_W]
    prev1 = sbuf_ref[:, CONV_W:2 * CONV_W]
    y_ref[...] = _short_conv_gate(u, prev1, prev2, _proj(h, wmain_ref, COL_GATE_B, COL_GATE_C),
                                  _proj(h, wmain_ref, COL_ZB, MAIN_W), wsc_ref)
    snew_ref[:, 0:CONV_W] = prev1
    snew_ref[:, CONV_W:2 * CONV_W] = u


def _lane_bcast_column(row):
    return jnp.broadcast_to(row, (DK, DK)).T


def _gla_decode_kernel(q_ref, k_ref, a_ref, v_ref, s_ref, snew_ref, o_ref):
    for b in range(DECODE_STATE_BLOCK):
        for hh in range(HEADS):
            kcols = slice(hh * DK, (hh + 1) * DK)
            vcols = slice(hh * DV, (hh + 1) * DV)
            a_col = _lane_bcast_column(a_ref[b:b + 1, kcols])
            k_col = _lane_bcast_column(k_ref[b:b + 1, kcols])
            q_col = _lane_bcast_column(q_ref[b:b + 1, kcols])
            v_row = v_ref[b:b + 1, vcols]
            halves = []
            for half in range(DV // DK):
                lanes = slice(half * DK, (half + 1) * DK)
                s_new = a_col * s_ref[b, hh, :, lanes] + k_col * v_row[:, lanes]
                snew_ref[b, hh, :, lanes] = s_new
                halves.append(jnp.sum(q_col * s_new, axis=0, keepdims=True))
            o_ref[b:b + 1, vcols] = jnp.concatenate(halves, axis=1)


def _even_decode_out_kernel(x_ref, o_ref, sg_ref, y_ref, gng_ref, wout_ref, x1_ref):
    gng = gng_ref[...]
    parts = [_head_rmsnorm(o_ref[:, hh * DV:(hh + 1) * DV], gng) for hh in range(HEADS)]
    o_mix = (jnp.concatenate(parts, axis=1) * sg_ref[...]).astype(BF16)
    out = _dot(o_mix, wout_ref[0:V_WIDTH, :]) + _dot(y_ref[...].astype(BF16), wout_ref[V_WIDTH:V_WIDTH + CONV_W, :])
    x1_ref[...] = x_ref[...] + out


def _odd_decode_kernel(x_ref, ng_ref, win_ref, bin_ref, wdw_ref, bdw_ref, lng_ref, lnb_ref, wout_ref,
                       bout_ref, fng_ref, cbuf_ref,
                       y_ref, cnew_ref):
    n_hist = CCONV_K - 1
    x = x_ref[...]
    h = _rmsnorm(x, ng_ref[...]).astype(BF16)
    a = _dot(h, win_ref[:, 0:CONV_W]) + bin_ref[:, 0:CONV_W]
    a_gate = _dot(h, win_ref[:, CONV_W:2 * CONV_W]) + bin_ref[:, CONV_W:2 * CONV_W]
    u = _gate(a, a_gate)
    yc = bdw_ref[...] + wdw_ref[n_hist:CCONV_K, :] * u
    for j in range(n_hist):
        tap = cbuf_ref[j]
        yc = yc + wdw_ref[j:j + 1, :] * tap
        if j >= 1:
            cnew_ref[j - 1] = tap
    cnew_ref[n_hist - 1] = u
    z = _dot(h, win_ref[:, 2 * CONV_W:3 * CONV_W]) + bin_ref[:, 2 * CONV_W:3 * CONV_W]
    act = _layernorm_act(yc, z, lng_ref[...], lnb_ref[...])
    y_ref[...] = _rmsnorm(x + _dot(act, wout_ref[...]) + bout_ref[...], fng_ref[...])


def _const_spec(shape):
    nd = len(shape)
    return pl.BlockSpec(shape, lambda *_: (0,) * nd, pipeline_mode=pl.Buffered(1))


def _row(v):
    return v.reshape(1, -1)


def kernel(x_prompt, x_sample, state_gla, state_sconv, state_cconv, norm_g, w_in_a, w_gate_up, b_gate_up, gla_norm_g, w_sconv, w_out_a, w_in_c, b_in_c, w_dwconv, b_dwconv, ln_g, ln_b, w_out_c, b_out_c, final_norm_g):
    bsz, seq, d = x_prompt.shape
    dec_b = x_sample.shape[0]
    assert d == D_MODEL and seq % EVEN_TILE == 0 and seq % ODD_TILE == 0 and x_sample.shape[1] == 1
    assert EVEN_TILE // MXU_K >= 2 and CONV_W == 4 * MXU_N and ODD_TILE >= DFT_HOP
    assert w_in_a.shape[0] == 1 and w_in_c.shape[0] == 1 and norm_g.shape[0] == 2

    assert w_in_a.shape[2] == MAIN_W
    wmain = w_in_a[0].T.astype(BF16)
    wup = jnp.pad(w_gate_up[0], ((0, GATE_RANK_PAD - GATE_RANK), (0, 0))).astype(BF16)
    bup = _row(b_gate_up[0])
    gng = _row(gla_norm_g[0])
    wsc = w_sconv[0]
    wout_a = w_out_a[0].astype(BF16)
    ng0 = _row(norm_g[0])
    ng1 = _row(norm_g[1])
    win_c = w_in_c[0].astype(BF16)
    bin_c = _row(b_in_c[0])
    wdw = w_dwconv[0]
    bdw = _row(b_dwconv[0])
    lng = _row(ln_g[0])
    lnb = _row(ln_b[0])
    wout_c = w_out_c[0].astype(BF16)
    bout = _row(b_out_c[0])
    fng = _row(final_norm_g)

    even_weights = (ng0, wmain, wup, bup)
    even_prompt_weights = even_weights + (gng, wsc, wout_a)
    odd_weights = (ng1, win_c, bin_c, wdw, bdw, lng, lnb, wout_c, bout, fng)
    odd_weight_specs = [_const_spec(w.shape) for w in odd_weights]
    hf = DFT_HALF
    assert CCONV_HALO >= CCONV_K - 1 and DFT_HOP % 16 == 0
    kk = np.arange(hf)[:, None]

    def packed_basis(pos):
        ang = 2.0 * np.pi * kk * pos[None, :] / DFT_N
        lower = -np.sin(ang)
        lower[0] = np.cos(np.pi * pos)
        return np.cos(ang), lower

    fc, fs = packed_basis(np.arange(DFT_N, dtype=np.float64))
    fwd = np.concatenate([fc, fs], axis=0).astype(np.float32)
    ic, isn = packed_basis(np.arange(CCONV_HALO, DFT_N, dtype=np.float64))
    weight = np.full((hf, 1), 2.0 / DFT_N)
    weight[0] = 1.0 / DFT_N
    inv = np.concatenate([weight * ic, weight * isn], axis=0).T.astype(np.float32)
    hc, hs = packed_basis(np.arange(CCONV_K, dtype=np.float64))
    hs_imag = hs.copy()
    hs_imag[0] = 0.0
    hc_alt = hc.copy()
    hc_alt[0] = hs[0]
    basis = np.zeros((3 * hf, RESP_K_PAD), np.float32)
    basis[:, :CCONV_K] = np.concatenate([hc, hs_imag, hc_alt], axis=0)
    taps_by_lag = jnp.pad(wdw[::-1], ((0, RESP_K_PAD - CCONV_K), (0, 0)))
    odd_prompt_weights = (ng1, win_c, bin_c, bdw, lng, lnb, wout_c, bout, fng,
                          jnp.asarray(fwd), jnp.asarray(inv), jnp.asarray(basis), taps_by_lag)

    prompt_params = pltpu.CompilerParams(dimension_semantics=("arbitrary", "arbitrary"),
                                         vmem_limit_bytes=VMEM_LIMIT)

    tm = EVEN_TILE
    tile_spec = pl.BlockSpec((None, tm, D_MODEL), lambda b, t: (b, t, 0))
    x1_p, gla_p, sconv_p = pl.pallas_call(
        _even_prompt_kernel,
        grid=(bsz, seq // tm),
        in_specs=[tile_spec] + [_const_spec(w.shape) for w in even_prompt_weights],
        out_specs=[tile_spec,
                   pl.BlockSpec((None, None, HEADS, DK, DV), lambda b, t: (0, b, 0, 0, 0)),
                   pl.BlockSpec((None, None, SCONV_K - 1, CONV_W), lambda b, t: (0, b, 0, 0))],
        out_shape=[jax.ShapeDtypeStruct((bsz, seq, D_MODEL), F32),
                   jax.ShapeDtypeStruct((1, bsz, HEADS, DK, DV), F32),
                   jax.ShapeDtypeStruct((1, bsz, SCONV_K - 1, CONV_W), F32)],
        scratch_shapes=[pltpu.VMEM((HEADS, DV, DK), F32),
                        pltpu.VMEM((tm + SCONV_HALO, CONV_W), F32),
                        pltpu.VMEM((tm, QK_WIDTH), BF16),
                        pltpu.VMEM((tm, QK_WIDTH), BF16),
                        pltpu.VMEM((tm, QK_WIDTH), BF16),
                        pltpu.VMEM((tm, V_WIDTH), BF16),
                        pltpu.VMEM((tm, QK_WIDTH), F32),
                        pltpu.VMEM((tm, V_WIDTH), F32),
                        pltpu.VMEM((tm, CONV_W), BF16)],
        compiler_params=prompt_params,
        name="even_prompt",
    )(x_prompt, *even_prompt_weights)

    tm = ODD_TILE
    tile_spec = pl.BlockSpec((None, tm, D_MODEL), lambda b, t: (b, t, 0))
    y_p, cconv_p = pl.pallas_call(
        _odd_prompt_kernel,
        grid=(bsz, seq // tm),
        in_specs=[tile_spec] + [_const_spec(w.shape) for w in odd_prompt_weights],
        out_specs=[tile_spec,
                   pl.BlockSpec((None, None, CCONV_K - 1, CONV_W), lambda b, t: (0, b, 0, 0))],
        out_shape=[jax.ShapeDtypeStruct((bsz, seq, D_MODEL), F32),
                   jax.ShapeDtypeStruct((1, bsz, CCONV_K - 1, CONV_W), F32)],
        scratch_shapes=[pltpu.VMEM((tm + CCONV_HALO, CONV_W), BF16),
                        pltpu.VMEM((tm, CONV_W), F32),
                        pltpu.VMEM((CCONV_HALO, CONV_W), F32),
                        pltpu.VMEM((tm, CONV_W), F32),
                        pltpu.VMEM((3 * DFT_HALF, CONV_W), F32)],
        compiler_params=prompt_params,
        name="odd_prompt",
    )(x1_p, *odd_prompt_weights)

    xs = x_sample.reshape(dec_b, D_MODEL)
    sbuf = state_sconv.reshape(dec_b, (SCONV_K - 1) * CONV_W)
    single = pltpu.CompilerParams(vmem_limit_bytes=VMEM_LIMIT)
    q_s, k_s, a_s, v_s, sg_s, ysc_s, sconv_s = pl.pallas_call(
        _even_decode_front_kernel,
        out_shape=[jax.ShapeDtypeStruct((dec_b, QK_WIDTH), F32),
                   jax.ShapeDtypeStruct((dec_b, QK_WIDTH), F32),
                   jax.ShapeDtypeStruct((dec_b, QK_WIDTH), F32),
                   jax.ShapeDtypeStruct((dec_b, V_WIDTH), F32),
                   jax.ShapeDtypeStruct((dec_b, V_WIDTH), F32),
                   jax.ShapeDtypeStruct((dec_b, CONV_W), F32),
                   jax.ShapeDtypeStruct((dec_b, (SCONV_K - 1) * CONV_W), F32)],
        compiler_params=single,
        name="even_decode_front",
    )(xs, *even_weights, wsc, sbuf)

    sb = DECODE_STATE_BLOCK
    assert dec_b % sb == 0
    vec_spec = lambda w: pl.BlockSpec((sb, w), lambda i: (i, 0))
    state_spec = pl.BlockSpec((sb, HEADS, DK, DV), lambda i: (i, 0, 0, 0))
    state_in_spec = pl.BlockSpec((sb, HEADS, DK, DV), lambda i: (i, 0, 0, 0),
                                 pipeline_mode=pl.Buffered(DECODE_STATE_BUFFERS))
    def gla_decode_streamed(*hbm_refs):
        pltpu.emit_pipeline(
            _gla_decode_kernel,
            grid=(dec_b // sb,),
            in_specs=[vec_spec(QK_WIDTH), vec_spec(QK_WIDTH), vec_spec(QK_WIDTH), vec_spec(V_WIDTH),
                      state_in_spec],
            out_specs=[state_spec, vec_spec(V_WIDTH)],
        )(*hbm_refs)

    any_spec = pl.BlockSpec(memory_space=pl.ANY)
    gla_s, o_s = pl.pallas_call(
        gla_decode_streamed,
        in_specs=[any_spec] * 5,
        out_specs=[any_spec] * 2,
        out_shape=[jax.ShapeDtypeStruct((dec_b, HEADS, DK, DV), F32),
                   jax.ShapeDtypeStruct((dec_b, V_WIDTH), F32)],
        compiler_params=pltpu.CompilerParams(vmem_limit_bytes=VMEM_LIMIT),
        name="gla_decode",
    )(q_s, k_s, a_s, v_s, state_gla[0])

    x1_s = pl.pallas_call(
        _even_decode_out_kernel,
        out_shape=jax.ShapeDtypeStruct((dec_b, D_MODEL), F32),
        compiler_params=single,
        name="even_decode_out",
    )(xs, o_s, sg_s, ysc_s, gng, wout_a)

    ob = DECODE_ODD_BLOCK
    assert dec_b % ob == 0
    rows_spec = pl.BlockSpec((ob, D_MODEL), lambda i: (i, 0))
    hist_spec = pl.BlockSpec((CCONV_K - 1, ob, CONV_W), lambda i: (0, i, 0))
    cbuf = jnp.transpose(state_cconv[0], (1, 0, 2))
    y_s, cconv_t = pl.pallas_call(
        _odd_decode_kernel,
        grid=(dec_b // ob,),
        in_specs=[rows_spec] + odd_weight_specs + [hist_spec],
        out_specs=[rows_spec, hist_spec],
        out_shape=[jax.ShapeDtypeStruct((dec_b, D_MODEL), F32),
                   jax.ShapeDtypeStruct((CCONV_K - 1, dec_b, CONV_W), F32)],
        compiler_params=pltpu.CompilerParams(dimension_semantics=("arbitrary",),
                                             vmem_limit_bytes=VMEM_LIMIT),
        name="odd_decode",
    )(x1_s, *odd_weights, cbuf)
    cconv_s = jnp.transpose(cconv_t, (1, 0, 2))[None]

    return (y_p,
            y_s.reshape(dec_b, 1, D_MODEL),
            gla_p,
            sconv_p,
            cconv_p,
            gla_s.reshape(1, dec_b, HEADS, DK, DV),
            sconv_s.reshape(1, dec_b, SCONV_K - 1, CONV_W),
            cconv_s)
```
